```python
import jax, jax.numpy as jnp
from jax import lax
import numpy as np

D_MODEL = 1024
BATCH = 8
SEQ = 4096
DEPTH = 4
DEC_BATCH = 16
DEC_SEQ = 2048
PAST_LEN = 128

GRID_W = 64
MEM_LEN = 256
N_GROUPS = 4
BRANCH_W = D_MODEL // N_GROUPS
N_HEADS = 4
HEAD_DIM = BRANCH_W // N_HEADS
ATT_KV_HEADS = 2
ROPE_AXIS_DIM = HEAD_DIM // 2
ROPE_THETA = 10000.0
Q_BLOCK = 128
CHUNK = 64
CONV_K = 3
N_DIRS = 2
GLA_DK = HEAD_DIM // 2
GLA_RANK = 16
GLA_NORMALIZER = 16.0
EPS = 1e-6

SPLITS = (
    N_HEADS * HEAD_DIM,
    ATT_KV_HEADS * HEAD_DIM,
    ATT_KV_HEADS * HEAD_DIM,
    BRANCH_W,
    3 * N_HEADS * HEAD_DIM,
    N_DIRS * N_HEADS,
    N_DIRS * N_HEADS,
    BRANCH_W,
    N_HEADS * GLA_DK,
    N_HEADS * GLA_DK,
    N_HEADS * HEAD_DIM,
    N_DIRS * GLA_RANK,
    BRANCH_W,
    N_HEADS * HEAD_DIM,
    BRANCH_W,
)
IN_COLS = sum(SPLITS)

kernel_name = 'hybrid_parallel_head_encoder'


def rms_norm(x, g):
    xf = x.astype(jnp.float32)
    y = xf * lax.rsqrt(jnp.mean(xf * xf, axis=-1, keepdims=True) + EPS)
    return (y * g.astype(jnp.float32)).astype(x.dtype)


def l2_norm(x):
    xf = x.astype(jnp.float32)
    return (xf * lax.rsqrt(jnp.sum(xf * xf, axis=-1, keepdims=True) + EPS)).astype(x.dtype)


def axial_rope_tables(n_tokens):
    rows = n_tokens // GRID_W
    row_pos = jnp.repeat(jnp.arange(rows, dtype=jnp.float32), GRID_W)
    col_pos = jnp.tile(jnp.arange(GRID_W, dtype=jnp.float32), rows)
    inv_freq = ROPE_THETA ** (-jnp.arange(0, ROPE_AXIS_DIM, 2, dtype=jnp.float32) / ROPE_AXIS_DIM)
    ang = jnp.stack([row_pos, col_pos], axis=-1)[..., None] * inv_freq
    return jnp.cos(ang), jnp.sin(ang)


def apply_axial_rope(x, cos, sin):
    B, S, H, D = x.shape
    half = ROPE_AXIS_DIM // 2
    xf = x.astype(jnp.float32).reshape(B, S, H, 2, 2, half)
    x1, x2 = xf[..., 0, :], xf[..., 1, :]
    c, s = cos[:, None], sin[:, None]
    out = jnp.stack([x1 * c - x2 * s, x2 * c + x1 * s], axis=-2)
    return out.reshape(B, S, H, D).astype(x.dtype)


def block_attention(q, k, v):
    B, S, H, D = q.shape
    kvh = k.shape[2]
    nb = S // Q_BLOCK
    qb = q.reshape(B, nb, Q_BLOCK, kvh, H // kvh, D).transpose(1, 0, 2, 3, 4, 5)
    scale = D ** -0.5

    def one_block(qblk):
        s = jnp.einsum('bqkgd,blkd->bkgql', qblk, k).astype(jnp.float32) * scale
        p = jax.nn.softmax(s, axis=-1).astype(v.dtype)
        return jnp.einsum('bkgql,blkd->bqkgd', p, v)

    o = lax.map(one_block, qb)
    return o.transpose(1, 0, 2, 3, 4, 5).reshape(B, S, H * D)


def short_conv_centered(x, w):
    S = x.shape[1]
    pad = CONV_K // 2
    xp = jnp.pad(x, ((0, 0), (pad, pad), (0, 0)))
    out = xp[:, 0:S] * w[0]
    for i in range(1, CONV_K):
        out = out + xp[:, i:i + S] * w[i]
    return out


def _to_chunks(x):
    B, S, H = x.shape[:3]
    x = x.reshape(B, S // CHUNK, CHUNK, H, *x.shape[3:])
    return jnp.moveaxis(x, 3, 1)


def _from_chunks(x):
    x = jnp.moveaxis(x, 1, 3)
    B, N, C, H = x.shape[:4]
    return x.reshape(B, N * C, H, *x.shape[4:])


def gated_delta_rule(q, k, v, beta, g):
    out_dtype = v.dtype
    q, k, v, beta, g = [_to_chunks(t.astype(jnp.float32)) for t in (q, k, v, beta, g)]
    dk, dv = q.shape[-1], v.shape[-1]
    q = q * dk ** -0.5
    b = jnp.cumsum(g, axis=-1)
    incl = jnp.tril(jnp.ones((CHUNK, CHUNK), bool))
    strict = jnp.tril(jnp.ones((CHUNK, CHUNK), bool), -1)
    diff = b[..., :, None] - b[..., None, :]
    decay = jnp.where(incl, jnp.exp(jnp.where(incl, diff, 0.0)), 0.0)
    kk = jnp.einsum('bhnid,bhnjd->bhnij', k, k)
    lower = jnp.where(strict, beta[..., :, None] * kk * decay, 0.0)
    t_mat = lower + jnp.eye(CHUNK, dtype=jnp.float32)
    rhs = jnp.concatenate([v * beta[..., None], k * (beta * jnp.exp(b))[..., None]], axis=-1)
    sol = lax.linalg.triangular_solve(t_mat, rhs, left_side=True, lower=True)
    u, w = sol[..., :dv], sol[..., dv:]
    qk = jnp.einsum('bhnid,bhnjd->bhnij', q, k) * decay
    q_dec = q * jnp.exp(b)[..., None]
    b_last = b[..., -1:]
    k_dec = k * jnp.exp(b_last - b)[..., None]
    chunk_decay = jnp.exp(b_last[..., 0])
    xs = tuple(jnp.moveaxis(t, 2, 0) for t in (q_dec, qk, u, w, k_dec, chunk_decay))

    def step(state, inp):
        qd, a, u_c, w_c, kd, cd = inp
        v_new = u_c - jnp.einsum('bhcd,bhde->bhce', w_c, state)
        o = jnp.einsum('bhcd,bhde->bhce', qd, state) + jnp.einsum('bhij,bhje->bhie', a, v_new)
        state = cd[..., None, None] * state + jnp.einsum('bhcd,bhce->bhde', kd, v_new)
        return state, o

    B, H = q.shape[:2]
    s0 = jnp.zeros((B, H, dk, dv), jnp.float32)
    _, o = lax.scan(step, s0, xs)
    return _from_chunks(jnp.moveaxis(o, 0, 2)).astype(out_dtype)


def gla_chunked(q, k, v, gk):
    out_dtype = v.dtype
    q, k, v, gk = [_to_chunks(t.astype(jnp.float32)) for t in (q, k, v, gk)]
    dk, dv = q.shape[-1], v.shape[-1]
    q = q * dk ** -0.5
    b = jnp.cumsum(gk, axis=3)
    q_e = q * jnp.exp(b)
    k_e = k * jnp.exp(-b)
    incl = jnp.tril(jnp.ones((CHUNK, CHUNK), bool))
    attn = jnp.where(incl, jnp.einsum('bhnid,bhnjd->bhnij', q_e, k_e), 0.0)
    o = jnp.einsum('bhnij,bhnje->bhnie', attn, v)
    b_last = b[..., -1:, :]
    k_dec = k * jnp.exp(b_last - b)
    d_state = jnp.einsum('bhncd,bhnce->bhnde', k_dec, v)
    chunk_decay = jnp.exp(b_last[..., 0, :])

    def step(state, inp):
        ds, cd = inp
        return cd[..., None] * state + ds, state

    B, H = q.shape[:2]
    s0 = jnp.zeros((B, H, dk, dv), jnp.float32)
    _, starts = lax.scan(step, s0, (jnp.moveaxis(d_state, 2, 0), jnp.moveaxis(chunk_decay, 2, 0)))
    o = o + jnp.einsum('bhncd,nbhde->bhnce', q_e, starts)
    return _from_chunks(o).astype(out_dtype)


def encoder_layer(h, mem, cos, sin, norm_g, w_in, att_q_norm_g, att_k_norm_g, gdn_conv_w, gdn_a_log,
                  gdn_dt_bias, gdn_out_norm_g, gla_w_gate_up, gla_b_gate, gla_out_norm_g, mem_norm_g,
                  w_mem_kv, w_out):
    B, S, _ = h.shape
    f32 = jnp.float32
    rev = lambda t: jnp.flip(t, axis=1)
    heads = lambda t, d: t.reshape(B, S, -1, d)

    xn = rms_norm(h, norm_g)
    proj = jnp.einsum('bsd,de->bse', xn, w_in)
    offsets = np.cumsum(SPLITS)[:-1].tolist()
    (a_q, a_k, a_v, a_gate, d_qkv, d_beta, d_alpha, d_gate,
     l_q, l_k, l_v, l_low, l_gate, m_q, m_gate) = jnp.split(proj, offsets, axis=-1)

    qa = apply_axial_rope(rms_norm(heads(a_q, HEAD_DIM), att_q_norm_g), cos, sin)
    ka = apply_axial_rope(rms_norm(heads(a_k, HEAD_DIM), att_k_norm_g), cos, sin)
    ya = block_attention(qa, ka, heads(a_v, HEAD_DIM)) * jax.nn.silu(a_gate)

    qkv = jax.nn.silu(short_conv_centered(d_qkv, gdn_conv_w))
    qb, kb, vb = jnp.split(qkv, 3, axis=-1)
    qb, kb, vb = l2_norm(heads(qb, HEAD_DIM)), l2_norm(heads(kb, HEAD_DIM)), heads(vb, HEAD_DIM)
    beta = jax.nn.sigmoid(d_beta.astype(f32)).reshape(B, S, N_DIRS, N_HEADS)
    alpha = -jnp.exp(gdn_a_log.astype(f32)) * jax.nn.softplus(
        d_alpha.astype(f32).reshape(B, S, N_DIRS, N_HEADS) + gdn_dt_bias.astype(f32))
    ob = gated_delta_rule(qb, kb, vb, beta[:, :, 0], alpha[:, :, 0]) + rev(
        gated_delta_rule(rev(qb), rev(kb), rev(vb), rev(beta[:, :, 1]), rev(alpha[:, :, 1])))
    yb = (rms_norm(ob, gdn_out_norm_g) * jax.nn.silu(heads(d_gate, HEAD_DIM))).reshape(B, S, BRANCH_W)

    gate_logit = jnp.einsum('bsnr,nrk->bsnk', l_low.reshape(B, S, N_DIRS, GLA_RANK), gla_w_gate_up) + gla_b_gate
    gk = (jax.nn.log_sigmoid(gate_logit.astype(f32)) / GLA_NORMALIZER).reshape(B, S, N_DIRS, N_HEADS, GLA_DK)
    qc, kc, vc = heads(l_q, GLA_DK), heads(l_k, GLA_DK), heads(l_v, HEAD_DIM)
    oc = gla_chunked(qc, kc, vc, gk[:, :, 0]) + rev(gla_chunked(rev(qc), rev(kc), rev(vc), rev(gk[:, :, 1])))
    yc = (rms_norm(oc, gla_out_norm_g) * jax.nn.silu(heads(l_gate, HEAD_DIM))).reshape(B, S, BRANCH_W)

    n_mem = mem.shape[1]
    kv = jnp.einsum('bmd,de->bme', rms_norm(mem, mem_norm_g), w_mem_kv)
    km, vm = jnp.split(kv, 2, axis=-1)
    km = km.reshape(B, n_mem, N_HEADS, HEAD_DIM)
    vm = vm.reshape(B, n_mem, N_HEADS, HEAD_DIM)
    ym = block_attention(heads(m_q, HEAD_DIM), km, vm) * jax.nn.silu(m_gate)

    y = jnp.einsum('bse,ed->bsd', jnp.concatenate([ya, yb, yc, ym], axis=-1), w_out)
    return h + y


def encoder_trunk(x, mem, weights, final_norm_g):
    cos, sin = axial_rope_tables(x.shape[1])
    h = x
    for d in range(DEPTH):
        h = encoder_layer(h, mem, cos, sin, *[w[d] for w in weights])
    return rms_norm(h, final_norm_g)


def setup_inputs(seed: int = 0) -> dict:
    key = jax.random.key(seed)
    ks = jax.random.split(key, 24)
    nrm = jax.random.normal
    f32 = jnp.float32
    dt = jnp.exp(jax.random.uniform(ks[10], (DEPTH, N_DIRS, N_HEADS), f32, np.log(1e-3), np.log(1e-1)))
    return {
        'x_prompt': nrm(ks[0], (BATCH, SEQ, D_MODEL), f32),
        'x_sample': nrm(ks[1], (DEC_BATCH, DEC_SEQ, D_MODEL), f32),
        'mem_prompt': nrm(ks[2], (BATCH, MEM_LEN, D_MODEL), f32),
        'mem_sample': nrm(ks[3], (DEC_BATCH, MEM_LEN, D_MODEL), f32),
        'norm_g': 1.0 + 0.02 * nrm(ks[4], (DEPTH, D_MODEL), f32),
        'w_in': nrm(ks[5], (DEPTH, D_MODEL, IN_COLS), f32) * D_MODEL ** -0.5,
        'att_q_norm_g': 1.0 + 0.02 * nrm(ks[6], (DEPTH, HEAD_DIM), f32),
        'att_k_norm_g': 1.0 + 0.02 * nrm(ks[7], (DEPTH, HEAD_DIM), f32),
        'gdn_conv_w': nrm(ks[8], (DEPTH, CONV_K, 3 * N_HEADS * HEAD_DIM), f32) * CONV_K ** -0.5,
        'gdn_a_log': jnp.log(jax.random.uniform(ks[9], (DEPTH, N_DIRS, N_HEADS), f32, 1.0, 16.0)),
        'gdn_dt_bias': dt + jnp.log(-jnp.expm1(-dt)),
        'gdn_out_norm_g': 1.0 + 0.02 * nrm(ks[11], (DEPTH, HEAD_DIM), f32),
        'gla_w_gate_up': nrm(ks[12], (DEPTH, N_DIRS, GLA_RANK, N_HEADS * GLA_DK), f32) * GLA_RANK ** -0.5,
        'gla_b_gate': 0.1 * nrm(ks[13], (DEPTH, N_DIRS, N_HEADS * GLA_DK), f32),
        'gla_out_norm_g': 1.0 + 0.02 * nrm(ks[14], (DEPTH, HEAD_DIM), f32),
        'mem_norm_g': 1.0 + 0.02 * nrm(ks[15], (DEPTH, D_MODEL), f32),
        'w_mem_kv': nrm(ks[16], (DEPTH, D_MODEL, 2 * N_HEADS * HEAD_DIM), f32) * D_MODEL ** -0.5,
        'w_out': nrm(ks[17], (DEPTH, N_GROUPS * BRANCH_W, D_MODEL), f32) * (N_GROUPS * BRANCH_W) ** -0.5 * (2 * DEPTH) ** -0.5,
        'final_norm_g': 1.0 + 0.02 * nrm(ks[18], (D_MODEL,), f32),
    }


def reference(x_prompt, x_sample, mem_prompt, mem_sample, norm_g, w_in, att_q_norm_g, att_k_norm_g,
              gdn_conv_w, gdn_a_log, gdn_dt_bias, gdn_out_norm_g, gla_w_gate_up, gla_b_gate,
              gla_out_norm_g, mem_norm_g, w_mem_kv, w_out, final_norm_g):
    weights = (norm_g, w_in, att_q_norm_g, att_k_norm_g, gdn_conv_w, gdn_a_log, gdn_dt_bias,
               gdn_out_norm_g, gla_w_gate_up, gla_b_gate, gla_out_norm_g, mem_norm_g, w_mem_kv, w_out)
    y_prompt = encoder_trunk(x_prompt, mem_prompt, weights, final_norm_g)
    y_sample = encoder_trunk(x_sample, mem_sample, weights, final_norm_g)
    return (y_prompt, y_sample)
```

```python
import functools

import numpy as np
import jax
import jax.numpy as jnp
from jax import lax
from jax.experimental import pallas as pl
from jax.experimental.pallas import tpu as pltpu

F32 = jnp.float32
BF16 = jnp.bfloat16

N_HEADS = 4
HEAD_DIM = 64
KV_HEADS = 2
GRID_W = 64
ROPE_THETA = 10000.0
CHUNK = 64
GLA_DK = 32
GLA_RANK = 16
GLA_NORMALIZER = 16.0
EPS = 1e-6
N_DIRS = 2
LANES = 128
GDN_GROUP = 2
VMEM_LIMIT = 56 * 1024 * 1024

C_AQ, C_AK, C_AV, C_AG = 0, 256, 384, 512
C_DQKV, C_DG = 768, 1536
C_LQ, C_LK, C_LV, C_LG = 1792, 1920, 2048, 2304
C_MQ, C_MG = 2560, 2816
C_SM = 3072
N_COLS = 3200
ATT_HEAD_PERM = (0, 2, 1, 3)


def _bf(x):
    return x.astype(BF16)


def _mm(a, b):
    return jnp.dot(_bf(a), _bf(b), preferred_element_type=F32)


def _mm_nt(a, b):
    return lax.dot_general(_bf(a), _bf(b), (((1,), (1,)), ((), ())), preferred_element_type=F32)


def _mm_tn(a, b):
    return lax.dot_general(_bf(a), _bf(b), (((0,), (0,)), ((), ())), preferred_element_type=F32)


def _split2(x):
    hi = _bf(x)
    lo = _bf(x - hi.astype(F32))
    return hi, lo


def _split3(x):
    hi = _bf(x).astype(F32)
    r = x - hi
    mid = _bf(r).astype(F32)
    lo = _bf(r - mid).astype(F32)
    return hi, mid, lo


def _group_sum(x, ones_blocks):
    hi, lo = _split2(x)
    return (jnp.dot(hi, ones_blocks, preferred_element_type=F32)
            + jnp.dot(lo, ones_blocks, preferred_element_type=F32))


def _silu(x):
    return x * jax.nn.sigmoid(x)


def _softplus(x):
    return jnp.maximum(x, 0.0) + jnp.log1p(jnp.exp(-jnp.abs(x)))


def _cumsum_rows(x, reverse):
    n = x.shape[0]
    row = lax.broadcasted_iota(jnp.int32, x.shape, 0)
    s = 1
    while s < n:
        if reverse:
            x = x + jnp.where(row < n - s, pltpu.roll(x, n - s, axis=0), 0.0)
        else:
            x = x + jnp.where(row >= s, pltpu.roll(x, s, axis=0), 0.0)
        s *= 2
    return x


def _block_diag(x, mask, reps):
    xb = _bf(x)
    t = jnp.concatenate([xb] * reps, axis=0)
    return jnp.where(mask, t, jnp.zeros_like(t))


def _inproj_kernel(h_ref, g_ref, w_ref, cos_ref, sin_ref, qkg_ref, ones_ref,
                   qkv_ref, gates_ref, gdn_ref, gla_ref, mq_ref, small_ref):
    x = h_ref[...]
    xn = x * lax.rsqrt(jnp.mean(x * x, axis=-1, keepdims=True) + EPS) * g_ref[...]
    proj = jnp.dot(_bf(xn), w_ref[...], preferred_element_type=F32)

    qk = proj[:, C_AQ:C_AV]
    ms = _group_sum(qk * qk, ones_ref[...]) * (1.0 / HEAD_DIM)
    qkn = qk * lax.rsqrt(ms + EPS) * qkg_ref[...]
    cos = cos_ref[...]
    sin = sin_ref[...]
    lane = lax.broadcasted_iota(jnp.int32, (1, LANES), 1)
    first_half = (lane % 32) < 16
    outs = []
    for c in range(3):
        xc = qkn[:, c * LANES:(c + 1) * LANES]
        partner = jnp.where(first_half, pltpu.roll(xc, LANES - 16, axis=1), pltpu.roll(xc, 16, axis=1))
        outs.append(xc * cos + partner * sin)
    qkv_ref[:, 0:128] = _bf(outs[0] * (HEAD_DIM ** -0.5))
    qkv_ref[:, 128:256] = _bf(outs[1] * (HEAD_DIM ** -0.5))
    qkv_ref[:, 256:384] = _bf(outs[2])
    qkv_ref[:, 384:512] = _bf(proj[:, C_AV:C_AG])

    gates_ref[:, 0:256] = _silu(proj[:, C_AG:C_AG + 256])
    gates_ref[:, 256:512] = _silu(proj[:, C_MG:C_MG + 256])
    gates_ref[:, 512:768] = _silu(proj[:, C_DG:C_DG + 256])
    gates_ref[:, 768:1024] = _silu(proj[:, C_LG:C_LG + 256])
    gdn_ref[...] = proj[:, C_DQKV:C_DG]
    gla_ref[...] = proj[:, C_LQ:C_LG]
    mq_ref[...] = _bf(proj[:, C_MQ:C_MG] * (HEAD_DIM ** -0.5))
    small_ref[...] = proj[:, C_SM:N_COLS]


def _inproj(h, g, w, cos_t, sin_t, qkg, ones384, seq, tm):
    T, D = h.shape
    nseq = seq // tm
    row = lambda i: (i, 0)
    const = lambda i: (0, 0)
    return pl.pallas_call(
        _inproj_kernel,
        grid=(T // tm,),
        in_specs=[
            pl.BlockSpec((tm, D), row),
            pl.BlockSpec((1, D), const),
            pl.BlockSpec((D, N_COLS), const),
            pl.BlockSpec((tm, LANES), lambda i: (i % nseq, 0)),
            pl.BlockSpec((tm, LANES), lambda i: (i % nseq, 0)),
            pl.BlockSpec((1, 384), const),
            pl.BlockSpec((384, 384), const),
        ],
        out_specs=[
            pl.BlockSpec((tm, 512), row),
            pl.BlockSpec((tm, 1024), row),
            pl.BlockSpec((tm, 768), row),
            pl.BlockSpec((tm, 512), row),
            pl.BlockSpec((tm, 256), row),
            pl.BlockSpec((tm, LANES), row),
        ],
        out_shape=[
            jax.ShapeDtypeStruct((T, 512), BF16),
            jax.ShapeDtypeStruct((T, 1024), F32),
            jax.ShapeDtypeStruct((T, 768), F32),
            jax.ShapeDtypeStruct((T, 512), F32),
            jax.ShapeDtypeStruct((T, 256), BF16),
            jax.ShapeDtypeStruct((T, LANES), F32),
        ],
        compiler_params=pltpu.CompilerParams(
            dimension_semantics=("arbitrary",), vmem_limit_bytes=VMEM_LIMIT),
        name="inproj",
    )(h, g, w, cos_t, sin_t, qkg, ones384)


def _memkv_kernel(m_ref, g_ref, w_ref, o_ref):
    x = m_ref[...]
    xn = x * lax.rsqrt(jnp.mean(x * x, axis=-1, keepdims=True) + EPS) * g_ref[0]
    o_ref[0] = _bf(jnp.dot(_bf(xn), w_ref[0], preferred_element_type=F32))


def _memkv(mem2d, g, w, tr):
    R, D = mem2d.shape
    depth = w.shape[0]
    return pl.pallas_call(
        _memkv_kernel,
        grid=(depth, R // tr),
        in_specs=[
            pl.BlockSpec((tr, D), lambda d, i: (i, 0)),
            pl.BlockSpec((1, 1, D), lambda d, i: (d, 0, 0)),
            pl.BlockSpec((1, D, 512), lambda d, i: (d, 0, 0)),
        ],
        out_specs=pl.BlockSpec((1, tr, 512), lambda d, i: (d, i, 0)),
        out_shape=jax.ShapeDtypeStruct((depth, R, 512), BF16),
        compiler_params=pltpu.CompilerParams(
            dimension_semantics=("arbitrary", "arbitrary"), vmem_limit_bytes=VMEM_LIMIT),
        name="memkv",
    )(mem2d, g, w)


def _attn_kernel(q_ref, k_ref, v_ref, mq_ref, km_ref, vm_ref, ga_ref, gm_ref, ya_ref, ym_ref, *, kvc):
    tq = q_ref.shape[0]
    seq = k_ref.shape[0]
    lane = lax.broadcasted_iota(jnp.int32, (1, LANES), 1)
    lo = lane < HEAD_DIM

    def stack_heads(x):
        z = jnp.zeros_like(x)
        return [jnp.where(lo, x, z), jnp.where(lo, z, x)]

    q = q_ref[...]
    qs = jnp.concatenate(stack_heads(q[:, :LANES]) + stack_heads(q[:, LANES:]), axis=0)
    rows = qs.shape[0]

    def body(j, carry):
        m, l, acc = carry
        kj = k_ref[pl.ds(pl.multiple_of(j * kvc, kvc), kvc), :]
        vj = v_ref[pl.ds(pl.multiple_of(j * kvc, kvc), kvc), :]
        s = lax.dot_general(qs, kj, (((1,), (1,)), ((), ())), preferred_element_type=F32)
        m_new = jnp.maximum(m, jnp.max(s, axis=-1, keepdims=True))
        alpha = jnp.exp(m - m_new)
        p = jnp.exp(s - m_new)
        l = alpha * l + jnp.sum(p, axis=-1, keepdims=True)
        acc = alpha * acc + jnp.dot(_bf(p), vj, preferred_element_type=F32)
        return m_new, l, acc

    init = (jnp.full((rows, 1), -jnp.inf, F32), jnp.zeros((rows, 1), F32), jnp.zeros((rows, LANES), F32))
    _, l, acc = lax.fori_loop(0, seq // kvc, body, init)
    o = acc / l
    ya = jnp.concatenate([jnp.where(lo, o[0:tq], o[tq:2 * tq]),
                          jnp.where(lo, o[2 * tq:3 * tq], o[3 * tq:4 * tq])], axis=1)
    ya_ref[...] = ya * ga_ref[...]

    mq = mq_ref[...]
    outs = []
    for c in range(2):
        qc = jnp.concatenate(stack_heads(mq[:, c * LANES:(c + 1) * LANES]), axis=0)
        kc = km_ref[:, c * LANES:(c + 1) * LANES]
        vc = vm_ref[:, c * LANES:(c + 1) * LANES]
        s = lax.dot_general(qc, kc, (((1,), (1,)), ((), ())), preferred_element_type=F32)
        p = jnp.exp(s - jnp.max(s, axis=-1, keepdims=True))
        om = jnp.dot(_bf(p), vc, preferred_element_type=F32) / jnp.sum(p, axis=-1, keepdims=True)
        outs.append(jnp.where(lo, om[0:tq], om[tq:2 * tq]))
    ym_ref[...] = jnp.concatenate(outs, axis=1) * gm_ref[...]


def _attention(qkv, mq, memkv, gates, batch, seq, tq, kvc):
    T = qkv.shape[0]
    nq = seq // tq
    mem_len = memkv.shape[0] // batch
    qrow = lambda b, i: (b * nq + i, 0)
    return pl.pallas_call(
        functools.partial(_attn_kernel, kvc=kvc),
        grid=(batch, nq),
        in_specs=[
            pl.BlockSpec((tq, 256), qrow),
            pl.BlockSpec((seq, LANES), lambda b, i: (b, 2)),
            pl.BlockSpec((seq, LANES), lambda b, i: (b, 3)),
            pl.BlockSpec((tq, 256), qrow),
            pl.BlockSpec((mem_len, 256), lambda b, i: (b, 0)),
            pl.BlockSpec((mem_len, 256), lambda b, i: (b, 1)),
            pl.BlockSpec((tq, 256), lambda b, i: (b * nq + i, 0)),
            pl.BlockSpec((tq, 256), lambda b, i: (b * nq + i, 1)),
        ],
        out_specs=[pl.BlockSpec((tq, 256), qrow), pl.BlockSpec((tq, 256), qrow)],
        out_shape=[jax.ShapeDtypeStruct((T, 256), F32), jax.ShapeDtypeStruct((T, 256), F32)],
        compiler_params=pltpu.CompilerParams(
            dimension_semantics=("arbitrary", "arbitrary"), vmem_limit_bytes=VMEM_LIMIT),
        name="attention",
    )(qkv, qkv, qkv, mq, memkv, memkv, gates, gates)


GDN_W = GDN_GROUP * HEAD_DIM
GDN_NGRP = N_HEADS // GDN_GROUP
N_COMBO = N_DIRS * N_HEADS
N_QTY = 4


def _gdn_constants():
    w = GDN_W
    i = np.arange(CHUNK)[:, None]
    lane = np.arange(w)[None, :]
    j = lane % CHUNK
    tri = np.stack([i >= j, i <= j]).astype(np.float32)
    stri = np.stack([i > j, i < j]).astype(np.float32)
    eye = (i == j).astype(np.float32)[None]
    tri_c = np.concatenate([tri, stri, eye], axis=0)
    r = np.arange(w)[:, None]
    bd = ((r // CHUNK) == (lane // CHUNK)).astype(np.float32)
    expand = np.zeros((N_DIRS * GDN_NGRP, LANES, N_QTY * w), np.float32)
    for d in range(N_DIRS):
        for g in range(GDN_NGRP):
            for qty in range(N_QTY):
                for hh in range(GDN_GROUP):
                    combo = d * N_HEADS + g * GDN_GROUP + hh
                    for piece in range(3):
                        src = piece * (N_QTY * N_COMBO) + qty * N_COMBO + combo
                        expand[d * GDN_NGRP + g, src, qty * w + hh * CHUNK: qty * w + (hh + 1) * CHUNK] = 1.0
    cm = np.zeros((N_COMBO, LANES), np.float32)
    for c in range(N_COMBO):
        for t in range(6):
            cm[c, t * N_COMBO + c] = 1.0
    return tri_c, bd, expand, cm


def _gdn_kernel(x_ref, sm_ref, cw_ref, par_ref, ones_ref, tri_ref, bd_ref, exp_ref, cm_ref,
                o_ref, qkv_s, sm3_s, xs_s, ys_s, st_s):
    seq = x_ref.shape[0]
    nchunk = seq // CHUNK
    w = GDN_W
    lane = lax.broadcasted_iota(jnp.int32, (1, LANES), 1)
    row = lax.broadcasted_iota(jnp.int32, (CHUNK, 1), 0)
    lane8 = lane < N_COMBO
    fwd_lane = lane < N_HEADS
    ones256 = ones_ref[...]

    def prep(c, carry):
        r0 = pl.multiple_of(c * CHUNK, CHUNK)
        xc = x_ref[pl.ds(r0, CHUNK), :]
        pstart = pl.multiple_of(jnp.maximum(r0 - 8, 0), 8)
        nstart = pl.multiple_of(jnp.minimum(r0 + CHUNK, seq - 8), 8)
        prev_row = x_ref[pl.ds(pstart, 8), :][7:8] * (c > 0).astype(F32)
        next_row = x_ref[pl.ds(nstart, 8), :][0:1] * (c < nchunk - 1).astype(F32)
        x_prev = jnp.where(row == 0, prev_row, pltpu.roll(xc, 1, axis=0))
        x_next = jnp.where(row == CHUNK - 1, next_row, pltpu.roll(xc, CHUNK - 1, axis=0))
        y = x_prev * cw_ref[0:1, :] + xc * cw_ref[1:2, :]
        y = y + x_next * cw_ref[2:3, :]
        y = _silu(y)
        for part in range(2):
            z = y[:, part * 256:(part + 1) * 256]
            ss = _group_sum(z * z, ones256)
            zn = z * lax.rsqrt(ss + EPS)
            if part == 0:
                zn = zn * (HEAD_DIM ** -0.5)
            qkv_s[pl.ds(r0, CHUNK), part * 256:(part + 1) * 256] = _bf(zn)
        qkv_s[pl.ds(r0, CHUNK), 512:768] = _bf(y[:, 512:768])

        sm = sm_ref[pl.ds(r0, CHUNK), :]
        beta = jax.nn.sigmoid(sm)
        g_all = par_ref[0:1, :] * _softplus(sm + par_ref[1:2, :])
        g = pltpu.roll(g_all, LANES - N_COMBO, axis=1)
        pre = _cumsum_rows(g, reverse=False)
        suf = _cumsum_rows(g, reverse=True)
        b = jnp.where(fwd_lane, pre, suf)
        b_end = jnp.where(fwd_lane, pre[CHUNK - 1:CHUNK], suf[0:1])
        eb = jnp.exp(b)
        zero = jnp.zeros_like(b)
        q4 = (jnp.where(lane8, beta, zero)
              + pltpu.roll(jnp.where(lane8, beta * eb, zero), N_COMBO, axis=1)
              + pltpu.roll(jnp.where(lane8, eb, zero), 2 * N_COMBO, axis=1)
              + pltpu.roll(jnp.where(lane8, jnp.exp(b_end - b), zero), 3 * N_COMBO, axis=1))
        hi, mid, lo3 = _split3(q4)
        nq = N_QTY * N_COMBO
        sm3_s[pl.ds(r0, CHUNK), :] = _bf(hi + pltpu.roll(mid, nq, axis=1) + pltpu.roll(lo3, 2 * nq, axis=1))
        bh, bm, bl = _split3(jnp.where(lane8, b, zero))
        pb = bh + pltpu.roll(bm, N_COMBO, axis=1) + pltpu.roll(bl, 2 * N_COMBO, axis=1)
        ones_lo = (lane < 3 * N_COMBO).astype(F32)
        ones_hi = jnp.logical_and(lane >= 3 * N_COMBO, lane < 6 * N_COMBO).astype(F32)
        xs_s[pl.ds(r0, CHUNK), :] = _bf(pb + ones_hi)
        ys_s[pl.ds(r0, CHUNK), :] = _bf(ones_lo - pltpu.roll(pb, 3 * N_COMBO, axis=1))
        return carry

    lax.fori_loop(0, nchunk, prep, 0)

    o_ref[...] = jnp.zeros_like(o_ref)
    st_s[...] = jnp.zeros_like(st_s)
    bd_mask = bd_ref[...] > 0.5

    def one_chunk(chunk, d, grp):
        r0 = pl.multiple_of(chunk * CHUNK, CHUNK)
        rs = pl.ds(r0, CHUNK)
        q = qkv_s[rs, grp * w:(grp + 1) * w]
        k = qkv_s[rs, 256 + grp * w:256 + (grp + 1) * w]
        v = qkv_s[rs, 512 + grp * w:512 + (grp + 1) * w].astype(F32)
        kf = k.astype(F32)
        ex = jnp.dot(sm3_s[rs, :], exp_ref[d * GDN_NGRP + grp], preferred_element_type=F32)
        betax, bebx, ebx, edx = (ex[:, t * w:(t + 1) * w] for t in range(N_QTY))
        ys = ys_s[rs, :]
        ystack = jnp.concatenate(
            [ys * _bf(cm_ref[d * N_HEADS + grp * GDN_GROUP + hh:d * N_HEADS + grp * GDN_GROUP + hh + 1, :])
             for hh in range(GDN_GROUP)], axis=0)
        diff = lax.dot_general(xs_s[rs, :], ystack, (((1,), (1,)), ((), ())), preferred_element_type=F32)
        tri = tri_ref[d] > 0.5
        stri = tri_ref[2 + d] > 0.5
        eye = tri_ref[4]
        decay = jnp.where(tri, jnp.exp(jnp.where(tri, diff, 0.0)), 0.0)

        head_masks = [(lax.broadcasted_iota(jnp.int32, (1, w), 1) // HEAD_DIM) == hh for hh in range(GDN_GROUP)]
        kbd = jnp.concatenate([jnp.where(hm, k, jnp.zeros_like(k)) for hm in head_masks], axis=0)
        kq = lax.dot_general(jnp.concatenate([k, q], axis=0), kbd, (((1,), (1,)), ((), ())),
                             preferred_element_type=F32)
        kk, qk = kq[:CHUNK], kq[CHUNK:]
        low = jnp.where(stri, betax * kk * decay, 0.0)
        a_mat = qk * decay

        p = _mm(low, _block_diag(low, bd_mask, GDN_GROUP))
        r = eye - low
        for lvl in range(5):
            pbd = _block_diag(p, bd_mask, GDN_GROUP)
            if lvl < 4:
                rp = _mm(jnp.concatenate([r, p], axis=0), pbd)
                r = r + rp[:CHUNK]
                p = rp[CHUNK:]
            else:
                r = r + _mm(r, pbd)
        u = _mm(r, _block_diag(betax * v, bd_mask, GDN_GROUP))
        wm = _mm(r, _block_diag(bebx * kf, bd_mask, GDN_GROUP))
        q_dec = q.astype(F32) * ebx
        k_dec = kf * edx
        cd_row = ebx[CHUNK - 1:CHUNK] if d == 0 else ebx[0:1]

        state = st_s[d * GDN_NGRP + grp]
        ws = _mm(jnp.concatenate([wm, q_dec], axis=0), state)
        v_new = u - ws[:CHUNK]
        o = ws[CHUNK:] + _mm(a_mat, _block_diag(v_new, bd_mask, GDN_GROUP))
        st_s[d * GDN_NGRP + grp] = cd_row * state + jnp.where(bd_mask, _mm_tn(k_dec, v_new), 0.0)
        o_ref[rs, grp * w:(grp + 1) * w] += o

    def step(i, carry):
        for d in range(N_DIRS):
            chunk = i if d == 0 else nchunk - 1 - i
            for grp in range(GDN_NGRP):
                one_chunk(chunk, d, grp)
        return carry

    lax.fori_loop(0, nchunk, step, 0)


def _gdn(gdn_in, small, conv_w, par, consts, batch, seq):
    T = gdn_in.shape[0]
    tri_c, bd, expand, cm, ones256 = consts
    whole = lambda b: (b, 0)
    c2 = lambda b: (0, 0)
    c3 = lambda b: (0, 0, 0)
    return pl.pallas_call(
        _gdn_kernel,
        grid=(batch,),
        in_specs=[
            pl.BlockSpec((seq, 768), whole, pipeline_mode=pl.Buffered(1)),
            pl.BlockSpec((seq, LANES), whole),
            pl.BlockSpec((8, 768), c2),
            pl.BlockSpec((8, LANES), c2),
            pl.BlockSpec((256, 256), c2),
            pl.BlockSpec(tri_c.shape, c3),
            pl.BlockSpec(bd.shape, c2),
            pl.BlockSpec(expand.shape, c3),
            pl.BlockSpec(cm.shape, c2),
        ],
        out_specs=pl.BlockSpec((seq, 256), whole),
        out_shape=jax.ShapeDtypeStruct((T, 256), F32),
        scratch_shapes=[
            pltpu.VMEM((seq, 768), BF16),
            pltpu.VMEM((seq, LANES), BF16),
            pltpu.VMEM((seq, LANES), BF16),
            pltpu.VMEM((seq, LANES), BF16),
            pltpu.VMEM((N_DIRS * GDN_NGRP, GDN_W, GDN_W), F32),
        ],
        compiler_params=pltpu.CompilerParams(
            dimension_semantics=("arbitrary",), vmem_limit_bytes=VMEM_LIMIT),
        name="gdn",
    )(gdn_in, small, conv_w, par, ones256, tri_c, bd, expand, cm)


GLA_QW = N_HEADS * GLA_DK
GLA_VW = N_HEADS * HEAD_DIM


def _gla_constants():
    i = np.arange(CHUNK)[:, None]
    lane = np.arange(GLA_VW)[None, :]
    j = lane % CHUNK
    tri = np.stack([i >= j, i <= j]).astype(np.float32)
    r = np.arange(GLA_VW)[:, None]
    bd_v = ((r // CHUNK) == (lane // HEAD_DIM)).astype(np.float32)
    rs = np.arange(GLA_QW)[:, None]
    st_mask = ((rs // GLA_DK) == (lane // HEAD_DIM)).astype(np.float32)
    kq_mask = ((r // CHUNK) == (np.arange(GLA_QW)[None, :] // GLA_DK)).astype(np.float32)
    return tri, bd_v, st_mask, kq_mask


def _gla_kernel(x_ref, sm_ref, wup_ref, bg_ref, tri_ref, bdv_ref, stm_ref, kqm_ref, o_ref, st_s):
    seq = x_ref.shape[0]
    nchunk = seq // CHUNK
    o_ref[...] = jnp.zeros_like(o_ref)
    st_s[...] = jnp.zeros_like(st_s)
    bdv_mask = bdv_ref[...] > 0.5
    st_mask = stm_ref[...] > 0.5
    kq_mask = kqm_ref[...] > 0.5

    def one_chunk(chunk, d):
        r0 = pl.multiple_of(chunk * CHUNK, CHUNK)
        rs = pl.ds(r0, CHUNK)
        q = x_ref[rs, 0:GLA_QW] * (GLA_DK ** -0.5)
        k = x_ref[rs, GLA_QW:2 * GLA_QW]
        v = x_ref[rs, 2 * GLA_QW:2 * GLA_QW + GLA_VW]
        logit = jnp.dot(_bf(sm_ref[rs, :]), wup_ref[d], preferred_element_type=F32) + bg_ref[d:d + 1, :]
        gk = (jnp.minimum(logit, 0.0) - jnp.log1p(jnp.exp(-jnp.abs(logit)))) * (1.0 / GLA_NORMALIZER)
        b = _cumsum_rows(gk, reverse=(d == 1))
        b_end = b[CHUNK - 1:CHUNK] if d == 0 else b[0:1]
        q_e = q * jnp.exp(b)
        k_e = k * jnp.exp(-b)
        k_dec = k * jnp.exp(b_end - b)
        keb = _bf(k_e)
        kebd = jnp.where(kq_mask, jnp.concatenate([keb] * N_HEADS, axis=0), jnp.zeros((GLA_VW, GLA_QW), BF16))
        attn = jnp.where(tri_ref[d] > 0.5, _mm_nt(q_e, kebd), 0.0)
        o = _mm(attn, _block_diag(v, bdv_mask, N_HEADS))
        state = st_s[d]
        o = o + _mm(q_e, state)
        cd_col = jnp.transpose(jnp.broadcast_to(jnp.exp(b_end), (GLA_QW, GLA_QW)))
        cd_full = jnp.concatenate([cd_col, cd_col], axis=1)
        st_s[d] = cd_full * state + jnp.where(st_mask, _mm_tn(k_dec, v), 0.0)
        o_ref[rs, :] += o

    def step(i, carry):
        one_chunk(i, 0)
        one_chunk(nchunk - 1 - i, 1)
        return carry

    lax.fori_loop(0, nchunk, step, 0)


def _gla(gla_in, small, wup, bg, consts, batch, seq):
    T = gla_in.shape[0]
    tri, bd_v, st_mask, kq_mask = consts
    whole = lambda b: (b, 0)
    c2 = lambda b: (0, 0)
    c3 = lambda b: (0, 0, 0)
    return pl.pallas_call(
        _gla_kernel,
        grid=(batch,),
        in_specs=[
            pl.BlockSpec((seq, 512), whole),
            pl.BlockSpec((seq, LANES), whole),
            pl.BlockSpec(wup.shape, c3),
            pl.BlockSpec(bg.shape, c2),
            pl.BlockSpec(tri.shape, c3),
            pl.BlockSpec(bd_v.shape, c2),
            pl.BlockSpec(st_mask.shape, c2),
            pl.BlockSpec(kq_mask.shape, c2),
        ],
        out_specs=pl.BlockSpec((seq, 256), whole),
        out_shape=jax.ShapeDtypeStruct((T, 256), F32),
        scratch_shapes=[pltpu.VMEM((N_DIRS, GLA_QW, GLA_VW), F32)],
        compiler_params=pltpu.CompilerParams(
            dimension_semantics=("arbitrary",), vmem_limit_bytes=VMEM_LIMIT),
        name="gla",
    )(gla_in, small, wup, bg, tri, bd_v, st_mask, kq_mask)


def _outproj_kernel(ya_ref, ob_ref, oc_ref, ym_ref, gates_ref, gb_ref, gc_ref, ones_ref, w_ref, h_ref,
                    fg_ref, o_ref, *, final):
    ones256 = ones_ref[...]

    def head_norm(x, g):
        ms = _group_sum(x * x, ones256) * (1.0 / HEAD_DIM)
        return x * lax.rsqrt(ms + EPS) * g

    yb = head_norm(ob_ref[...], gb_ref[...]) * gates_ref[:, 0:256]
    yc = head_norm(oc_ref[...], gc_ref[...]) * gates_ref[:, 256:512]
    y = jnp.dot(_bf(ya_ref[...]), w_ref[0:256, :], preferred_element_type=F32)
    y = y + jnp.dot(_bf(yb), w_ref[256:512, :], preferred_element_type=F32)
    y = y + jnp.dot(_bf(yc), w_ref[512:768, :], preferred_element_type=F32)
    y = y + jnp.dot(_bf(ym_ref[...]), w_ref[768:1024, :], preferred_element_type=F32)
    hn = h_ref[...] + y
    if final:
        hn = hn * lax.rsqrt(jnp.mean(hn * hn, axis=-1, keepdims=True) + EPS) * fg_ref[...]
    o_ref[...] = hn


def _outproj(ya, ob, oc, ym, gates, gb, gc, ones256, w, h, fg, tm, final):
    T, D = h.shape
    row = lambda i: (i, 0)
    const = lambda i: (0, 0)
    return pl.pallas_call(
        functools.partial(_outproj_kernel, final=final),
        grid=(T // tm,),
        in_specs=[
            pl.BlockSpec((tm, 256), row), pl.BlockSpec((tm, 256), row),
            pl.BlockSpec((tm, 256), row), pl.BlockSpec((tm, 256), row),
            pl.BlockSpec((tm, 512), lambda i: (i, 1)),
            pl.BlockSpec((1, 256), const), pl.BlockSpec((1, 256), const),
            pl.BlockSpec((256, 256), const),
            pl.BlockSpec((D, D), const),
            pl.BlockSpec((tm, D), row),
            pl.BlockSpec((1, D), const),
        ],
        out_specs=pl.BlockSpec((tm, D), row),
        out_shape=jax.ShapeDtypeStruct((T, D), F32),
        compiler_params=pltpu.CompilerParams(
            dimension_semantics=("arbitrary",), vmem_limit_bytes=VMEM_LIMIT),
        name="outproj_final" if final else "outproj",
    )(ya, ob, oc, ym, gates, gb, gc, ones256, w, h, fg)


def _block_ones(n, group):
    i = np.arange(n)
    return jnp.asarray((i[:, None] // group) == (i[None, :] // group), dtype=BF16)


def _head_perm_cols():
    return np.concatenate([np.arange(h * HEAD_DIM, (h + 1) * HEAD_DIM) for h in ATT_HEAD_PERM])


def _prep_w_in(w_in):
    o = np.cumsum([0, 256, 128, 128, 256, 768, 8, 8, 256, 128, 128, 256, 32, 256, 256, 256])
    (a_q, a_k, a_v, a_g, d_qkv, d_beta, d_alpha, d_gate, l_q, l_k, l_v, l_low, l_gate, m_q, m_gate) = [
        w_in[:, :, o[i]:o[i + 1]] for i in range(15)]
    perm = _head_perm_cols()
    pad = jnp.zeros(w_in.shape[:2] + (N_COLS - C_SM - 48,), w_in.dtype)
    cols = [a_q[:, :, perm], a_k, a_v, a_g[:, :, perm], d_qkv, d_gate, l_q, l_k, l_v, l_gate, m_q, m_gate,
            d_beta, d_alpha, l_low, pad]
    return jnp.concatenate(cols, axis=-1).astype(BF16)


def _rope_tables(seq):
    rows = seq // GRID_W
    row_pos = jnp.repeat(jnp.arange(rows, dtype=F32), GRID_W)
    col_pos = jnp.tile(jnp.arange(GRID_W, dtype=F32), rows)
    half = HEAD_DIM // 4
    inv_freq = ROPE_THETA ** (-jnp.arange(0, 2 * half, 2, dtype=F32) / (2 * half))
    ang = jnp.stack([row_pos, col_pos], axis=-1)[..., None] * inv_freq
    cos, sin = jnp.cos(ang), jnp.sin(ang)
    cos_h = jnp.concatenate([cos, cos], axis=-1).reshape(seq, HEAD_DIM)
    sin_h = jnp.concatenate([-sin, sin], axis=-1).reshape(seq, HEAD_DIM)
    return jnp.tile(cos_h, (1, 2)), jnp.tile(sin_h, (1, 2))


def _trunk(x, mem, p, consts):
    batch, seq, d_model = x.shape
    T = batch * seq
    depth = p["w_in"].shape[0]
    tm = 512 if seq % 512 == 0 else seq
    tq = 256 if seq % 256 == 0 else seq
    kvc = 512 if seq % 512 == 0 else seq
    cos_t, sin_t = _rope_tables(seq)
    mem2d = mem.reshape(batch * mem.shape[1], d_model)
    memkv = _memkv(mem2d, p["mem_norm_g"], p["w_mem_kv"], min(512, mem2d.shape[0]))
    h = x.reshape(T, d_model)
    for d in range(depth):
        qkv, gates, gdn_in, gla_in, mq, small = _inproj(
            h, p["norm_g"][d], p["w_in"][d], cos_t, sin_t, p["qkg"][d], consts["ones384"], seq, tm)
        ya, ym = _attention(qkv, mq, memkv[d], gates, batch, seq, tq, kvc)
        ob = _gdn(gdn_in, small, p["conv_w"][d], p["gdn_par"][d], consts["gdn"], batch, seq)
        oc = _gla(gla_in, small, p["wup"][d], p["bg"][d], consts["gla"], batch, seq)
        h = _outproj(ya, ob, oc, ym, gates, p["gb"][d], p["gc"][d],
                     consts["ones256"], p["w_out"][d], h, p["final_g"], tm, final=(d == depth - 1))
    return h.reshape(batch, seq, d_model)


def kernel(x_prompt, x_sample, mem_prompt, mem_sample, norm_g, w_in, att_q_norm_g, att_k_norm_g, gdn_conv_w,
           gdn_a_log, gdn_dt_bias, gdn_out_norm_g, gla_w_gate_up, gla_b_gate, gla_out_norm_g, mem_norm_g,
           w_mem_kv, w_out, final_norm_g):
    depth, d_model = norm_g.shape
    perm = _head_perm_cols()
    w_out_p = jnp.concatenate([w_out[:, :256][:, perm], w_out[:, 256:]], axis=1).astype(BF16)
    qkg = jnp.concatenate([jnp.tile(att_q_norm_g, (1, N_HEADS)), jnp.tile(att_k_norm_g, (1, KV_HEADS))], axis=1)
    zeros8 = jnp.zeros((depth, N_COMBO), F32)
    par_a = jnp.concatenate([zeros8, -jnp.exp(gdn_a_log.reshape(depth, N_COMBO)),
                             jnp.zeros((depth, LANES - 2 * N_COMBO), F32)], axis=1)
    par_b = jnp.concatenate([zeros8, gdn_dt_bias.reshape(depth, N_COMBO),
                             jnp.zeros((depth, LANES - 2 * N_COMBO), F32)], axis=1)
    gdn_par = jnp.concatenate([par_a[:, None], par_b[:, None], jnp.zeros((depth, 6, LANES), F32)], axis=1)
    conv_w = jnp.concatenate([gdn_conv_w, jnp.zeros((depth, 5, gdn_conv_w.shape[2]), F32)], axis=1)
    wup = jnp.zeros((depth, N_DIRS, LANES, GLA_QW), F32)
    for d in range(N_DIRS):
        lo = 2 * N_COMBO + d * GLA_RANK
        wup = wup.at[:, d, lo:lo + GLA_RANK, :].set(gla_w_gate_up[:, d])
    p = {
        "norm_g": norm_g.reshape(depth, 1, d_model),
        "w_in": _prep_w_in(w_in),
        "qkg": qkg.reshape(depth, 1, 384),
        "conv_w": conv_w,
        "gdn_par": gdn_par,
        "wup": wup.astype(BF16),
        "bg": jnp.concatenate([gla_b_gate, jnp.zeros((depth, 6, GLA_QW), F32)], axis=1),
        "gb": jnp.tile(gdn_out_norm_g, (1, N_HEADS)).reshape(depth, 1, 256),
        "gc": jnp.tile(gla_out_norm_g, (1, N_HEADS)).reshape(depth, 1, 256),
        "mem_norm_g": mem_norm_g.reshape(depth, 1, d_model),
        "w_mem_kv": w_mem_kv.astype(BF16),
        "w_out": w_out_p,
        "final_g": final_norm_g.reshape(1, d_model),
    }
    tri_c, bd, expand, cm = _gdn_constants()
    consts = {
        "ones384": _block_ones(384, HEAD_DIM),
        "ones256": _block_ones(256, HEAD_DIM),
        "gdn": (jnp.asarray(tri_c), jnp.asarray(bd), jnp.asarray(expand, dtype=BF16), jnp.asarray(cm),
                _block_ones(256, HEAD_DIM)),
        "gla": tuple(jnp.asarray(a) for a in _gla_constants()),
    }
    y_prompt = _trunk(x_prompt, mem_prompt, p, consts)
    y_sample = _trunk(x_sample, mem_sample, p, consts)
    return (y_prompt, y_sample)
```

```python
import functools

import numpy as np
import jax
import jax.numpy as jnp
from jax import lax
from jax.experimental import pallas as pl
from jax.experimental.pallas import tpu as pltpu

F32 = jnp.float32
BF16 = jnp.bfloat16

N_HEADS = 4
HEAD_DIM = 64
KV_HEADS = 2
GRID_W = 64
ROPE_THETA = 10000.0
CHUNK = 64
GLA_DK = 32
GLA_RANK = 16
GLA_NORMALIZER = 16.0
EPS = 1e-6
N_DIRS = 2
LANES = 128
GDN_GROUP = 2
GDN_WINDOW = 4
VMEM_LIMIT = 56 * 1024 * 1024

C_AQ, C_AK, C_AV, C_AG = 0, 256, 384, 512
C_DQKV, C_DG = 768, 1536
C_LQ, C_LK, C_LV, C_LG = 1792, 1920, 2048, 2304
C_MQ, C_MG = 2560, 2816
C_SM = 3072
N_COLS = 3200
ATT_HEAD_PERM = (0, 2, 1, 3)


def _bf(x):
    return x.astype(BF16)


def _mm(a, b):
    return jnp.dot(_bf(a), _bf(b), preferred_element_type=F32)


def _mm_nt(a, b):
    return lax.dot_general(_bf(a), _bf(b), (((1,), (1,)), ((), ())), preferred_element_type=F32)


def _mm_tn(a, b):
    return lax.dot_general(_bf(a), _bf(b), (((0,), (0,)), ((), ())), preferred_element_type=F32)


def _split2(x):
    hi = _bf(x)
    lo = _bf(x - hi.astype(F32))
    return hi, lo


def _split3(x):
    hi = _bf(x).astype(F32)
    r = x - hi
    mid = _bf(r).astype(F32)
    lo = _bf(r - mid).astype(F32)
    return hi, mid, lo


def _group_sum(x, ones_blocks):
    hi, lo = _split2(x)
    return (jnp.dot(hi, ones_blocks, preferred_element_type=F32)
            + jnp.dot(lo, ones_blocks, preferred_element_type=F32))


def _silu(x):
    return x * jax.nn.sigmoid(x)


def _softplus(x):
    return jnp.maximum(x, 0.0) + jnp.log1p(jnp.exp(-jnp.abs(x)))


def _cumsum_rows(x, reverse):
    n = x.shape[0]
    row = lax.broadcasted_iota(jnp.int32, x.shape, 0)
    s = 1
    while s < n:
        if reverse:
            x = x + jnp.where(row < n - s, pltpu.roll(x, n - s, axis=0), 0.0)
        else:
            x = x + jnp.where(row >= s, pltpu.roll(x, s, axis=0), 0.0)
        s *= 2
    return x


def _block_diag(x, mask, reps):
    xb = _bf(x)
    t = jnp.concatenate([xb] * reps, axis=0)
    return jnp.where(mask, t, jnp.zeros_like(t))


def _inproj_kernel(h_ref, g_ref, w_ref, cos_ref, sin_ref, qkg_ref, ones_ref,
                   qkv_ref, gates_ref, gdn_ref, gla_ref, mq_ref, small_ref):
    x = h_ref[...]
    xn = x * lax.rsqrt(jnp.mean(x * x, axis=-1, keepdims=True) + EPS) * g_ref[...]
    proj = jnp.dot(_bf(xn), w_ref[...], preferred_element_type=F32)

    qk = proj[:, C_AQ:C_AV]
    ms = _group_sum(qk * qk, ones_ref[...]) * (1.0 / HEAD_DIM)
    qkn = qk * lax.rsqrt(ms + EPS) * qkg_ref[...]
    cos = cos_ref[...]
    sin = sin_ref[...]
    lane = lax.broadcasted_iota(jnp.int32, (1, LANES), 1)
    first_half = (lane % 32) < 16
    outs = []
    for c in range(3):
        xc = qkn[:, c * LANES:(c + 1) * LANES]
        partner = jnp.where(first_half, pltpu.roll(xc, LANES - 16, axis=1), pltpu.roll(xc, 16, axis=1))
        outs.append(xc * cos + partner * sin)
    qkv_ref[:, 0:128] = _bf(outs[0] * (HEAD_DIM ** -0.5))
    qkv_ref[:, 128:256] = _bf(outs[1] * (HEAD_DIM ** -0.5))
    qkv_ref[:, 256:384] = _bf(outs[2])
    qkv_ref[:, 384:512] = _bf(proj[:, C_AV:C_AG])

    gates_ref[:, 0:256] = _silu(proj[:, C_AG:C_AG + 256])
    gates_ref[:, 256:512] = _silu(proj[:, C_MG:C_MG + 256])
    gates_ref[:, 512:768] = _silu(proj[:, C_DG:C_DG + 256])
    gates_ref[:, 768:1024] = _silu(proj[:, C_LG:C_LG + 256])
    gdn_ref[...] = proj[:, C_DQKV:C_DG]
    gla_ref[...] = proj[:, C_LQ:C_LG]
    mq_ref[...] = _bf(proj[:, C_MQ:C_MG] * (HEAD_DIM ** -0.5))
    small_ref[...] = proj[:, C_SM:N_COLS]


def _inproj(h, g, w, cos_t, sin_t, qkg, ones384, seq, tm):
    T, D = h.shape
    nseq = seq // tm
    row = lambda i: (i, 0)
    const = lambda i: (0, 0)
    return pl.pallas_call(
        _inproj_kernel,
        grid=(T // tm,),
        in_specs=[
            pl.BlockSpec((tm, D), row),
            pl.BlockSpec((1, D), const),
            pl.BlockSpec((D, N_COLS), const),
            pl.BlockSpec((tm, LANES), lambda i: (i % nseq, 0)),
            pl.BlockSpec((tm, LANES), lambda i: (i % nseq, 0)),
            pl.BlockSpec((1, 384), const),
            pl.BlockSpec((384, 384), const),
        ],
        out_specs=[
            pl.BlockSpec((tm, 512), row),
            pl.BlockSpec((tm, 1024), row),
            pl.BlockSpec((tm, 768), row),
            pl.BlockSpec((tm, 512), row),
            pl.BlockSpec((tm, 256), row),
            pl.BlockSpec((tm, LANES), row),
        ],
        out_shape=[
            jax.ShapeDtypeStruct((T, 512), BF16),
            jax.ShapeDtypeStruct((T, 1024), F32),
            jax.ShapeDtypeStruct((T, 768), F32),
            jax.ShapeDtypeStruct((T, 512), F32),
            jax.ShapeDtypeStruct((T, 256), BF16),
            jax.ShapeDtypeStruct((T, LANES), F32),
        ],
        compiler_params=pltpu.CompilerParams(
            dimension_semantics=("arbitrary",), vmem_limit_bytes=VMEM_LIMIT),
        name="inproj",
    )(h, g, w, cos_t, sin_t, qkg, ones384)


def _memkv_kernel(m_ref, g_ref, w_ref, o_ref):
    x = m_ref[...]
    xn = x * lax.rsqrt(jnp.mean(x * x, axis=-1, keepdims=True) + EPS) * g_ref[0]
    o_ref[0] = _bf(jnp.dot(_bf(xn), w_ref[0], preferred_element_type=F32))


def _memkv(mem2d, g, w, tr):
    R, D = mem2d.shape
    depth = w.shape[0]
    return pl.pallas_call(
        _memkv_kernel,
        grid=(depth, R // tr),
        in_specs=[
            pl.BlockSpec((tr, D), lambda d, i: (i, 0)),
            pl.BlockSpec((1, 1, D), lambda d, i: (d, 0, 0)),
            pl.BlockSpec((1, D, 512), lambda d, i: (d, 0, 0)),
        ],
        out_specs=pl.BlockSpec((1, tr, 512), lambda d, i: (d, i, 0)),
        out_shape=jax.ShapeDtypeStruct((depth, R, 512), BF16),
        compiler_params=pltpu.CompilerParams(
            dimension_semantics=("arbitrary", "arbitrary"), vmem_limit_bytes=VMEM_LIMIT),
        name="memkv",
    )(mem2d, g, w)


def _attn_kernel(q_ref, k_ref, v_ref, mq_ref, km_ref, vm_ref, ga_ref, gm_ref, ya_ref, ym_ref, *, kvc):
    tq = q_ref.shape[0]
    seq = k_ref.shape[0]
    lane = lax.broadcasted_iota(jnp.int32, (1, LANES), 1)
    lo = lane < HEAD_DIM

    def stack_heads(x):
        z = jnp.zeros_like(x)
        return [jnp.where(lo, x, z), jnp.where(lo, z, x)]

    q = q_ref[...]
    qs = jnp.concatenate(stack_heads(q[:, :LANES]) + stack_heads(q[:, LANES:]), axis=0)
    rows = qs.shape[0]

    def body(j, carry):
        m, l, acc = carry
        kj = k_ref[pl.ds(pl.multiple_of(j * kvc, kvc), kvc), :]
        vj = v_ref[pl.ds(pl.multiple_of(j * kvc, kvc), kvc), :]
        s = lax.dot_general(qs, kj, (((1,), (1,)), ((), ())), preferred_element_type=F32)
        m_new = jnp.maximum(m, jnp.max(s, axis=-1, keepdims=True))
        alpha = jnp.exp(m - m_new)
        p = jnp.exp(s - m_new)
        l = alpha * l + jnp.sum(p, axis=-1, keepdims=True)
        acc = alpha * acc + jnp.dot(_bf(p), vj, preferred_element_type=F32)
        return m_new, l, acc

    init = (jnp.full((rows, 1), -jnp.inf, F32), jnp.zeros((rows, 1), F32), jnp.zeros((rows, LANES), F32))
    _, l, acc = lax.fori_loop(0, seq // kvc, body, init)
    o = acc / l
    ya = jnp.concatenate([jnp.where(lo, o[0:tq], o[tq:2 * tq]),
                          jnp.where(lo, o[2 * tq:3 * tq], o[3 * tq:4 * tq])], axis=1)
    ya_ref[...] = ya * ga_ref[...]

    mq = mq_ref[...]
    outs = []
    for c in range(2):
        qc = jnp.concatenate(stack_heads(mq[:, c * LANES:(c + 1) * LANES]), axis=0)
        kc = km_ref[:, c * LANES:(c + 1) * LANES]
        vc = vm_ref[:, c * LANES:(c + 1) * LANES]
        s = lax.dot_general(qc, kc, (((1,), (1,)), ((), ())), preferred_element_type=F32)
        p = jnp.exp(s - jnp.max(s, axis=-1, keepdims=True))
        om = jnp.dot(_bf(p), vc, preferred_element_type=F32) / jnp.sum(p, axis=-1, keepdims=True)
        outs.append(jnp.where(lo, om[0:tq], om[tq:2 * tq]))
    ym_ref[...] = jnp.concatenate(outs, axis=1) * gm_ref[...]


def _attention(qkv, mq, memkv, gates, batch, seq, tq, kvc):
    T = qkv.shape[0]
    nq = seq // tq
    mem_len = memkv.shape[0] // batch
    qrow = lambda b, i: (b * nq + i, 0)
    return pl.pallas_call(
        functools.partial(_attn_kernel, kvc=kvc),
        grid=(batch, nq),
        in_specs=[
            pl.BlockSpec((tq, 256), qrow),
            pl.BlockSpec((seq, LANES), lambda b, i: (b, 2)),
            pl.BlockSpec((seq, LANES), lambda b, i: (b, 3)),
            pl.BlockSpec((tq, 256), qrow),
            pl.BlockSpec((mem_len, 256), lambda b, i: (b, 0)),
            pl.BlockSpec((mem_len, 256), lambda b, i: (b, 1)),
            pl.BlockSpec((tq, 256), lambda b, i: (b * nq + i, 0)),
            pl.BlockSpec((tq, 256), lambda b, i: (b * nq + i, 1)),
        ],
        out_specs=[pl.BlockSpec((tq, 256), qrow), pl.BlockSpec((tq, 256), qrow)],
        out_shape=[jax.ShapeDtypeStruct((T, 256), F32), jax.ShapeDtypeStruct((T, 256), F32)],
        compiler_params=pltpu.CompilerParams(
            dimension_semantics=("arbitrary", "arbitrary"), vmem_limit_bytes=VMEM_LIMIT),
        name="attention",
    )(qkv, qkv, qkv, mq, memkv, memkv, gates, gates)


GDN_W = GDN_GROUP * HEAD_DIM
GDN_NGRP = N_HEADS // GDN_GROUP
N_COMBO = N_DIRS * N_HEADS
N_QTY = 4


def _gdn_constants():
    w = GDN_W
    i = np.arange(CHUNK)[:, None]
    lane = np.arange(w)[None, :]
    j = lane % CHUNK
    tri = np.stack([i >= j, i <= j]).astype(np.float32)
    stri = np.stack([i > j, i < j]).astype(np.float32)
    eye = (i == j).astype(np.float32)[None]
    tri_c = np.concatenate([tri, stri, eye], axis=0)
    r = np.arange(w)[:, None]
    bd = ((r // CHUNK) == (lane // CHUNK)).astype(np.float32)
    expand = np.zeros((N_DIRS * GDN_NGRP, LANES, N_QTY * w), np.float32)
    for d in range(N_DIRS):
        for g in range(GDN_NGRP):
            for qty in range(N_QTY):
                for hh in range(GDN_GROUP):
                    combo = d * N_HEADS + g * GDN_GROUP + hh
                    for piece in range(3):
                        src = piece * (N_QTY * N_COMBO) + qty * N_COMBO + combo
                        expand[d * GDN_NGRP + g, src, qty * w + hh * CHUNK: qty * w + (hh + 1) * CHUNK] = 1.0
    cm = np.zeros((N_COMBO, LANES), np.float32)
    for c in range(N_COMBO):
        for t in range(6):
            cm[c, t * N_COMBO + c] = 1.0
    return tri_c, bd, expand, cm


def _gdn_kernel(x_ref, sm_ref, cw_ref, par_ref, ones_ref, tri_ref, bd_ref, exp_ref, cm_ref,
                o_ref, qkv_s, sm3_s, xs_s, ys_s, st_s, pa_s, pb_s, cd_s):
    seq = x_ref.shape[0]
    nchunk = seq // CHUNK
    w = GDN_W
    lane = lax.broadcasted_iota(jnp.int32, (1, LANES), 1)
    row = lax.broadcasted_iota(jnp.int32, (CHUNK, 1), 0)
    lane8 = lane < N_COMBO
    fwd_lane = lane < N_HEADS
    ones256 = ones_ref[...]

    def prep(c, carry):
        r0 = pl.multiple_of(c * CHUNK, CHUNK)
        xc = x_ref[pl.ds(r0, CHUNK), :]
        pstart = pl.multiple_of(jnp.maximum(r0 - 8, 0), 8)
        nstart = pl.multiple_of(jnp.minimum(r0 + CHUNK, seq - 8), 8)
        prev_row = x_ref[pl.ds(pstart, 8), :][7:8] * (c > 0).astype(F32)
        next_row = x_ref[pl.ds(nstart, 8), :][0:1] * (c < nchunk - 1).astype(F32)
        x_prev = jnp.where(row == 0, prev_row, pltpu.roll(xc, 1, axis=0))
        x_next = jnp.where(row == CHUNK - 1, next_row, pltpu.roll(xc, CHUNK - 1, axis=0))
        y = x_prev * cw_ref[0:1, :] + xc * cw_ref[1:2, :]
        y = y + x_next * cw_ref[2:3, :]
        y = _silu(y)
        for part in range(2):
            z = y[:, part * 256:(part + 1) * 256]
            ss = _group_sum(z * z, ones256)
            zn = z * lax.rsqrt(ss + EPS)
            if part == 0:
                zn = zn * (HEAD_DIM ** -0.5)
            qkv_s[pl.ds(r0, CHUNK), part * 256:(part + 1) * 256] = _bf(zn)
        qkv_s[pl.ds(r0, CHUNK), 512:768] = _bf(y[:, 512:768])

        sm = sm_ref[pl.ds(r0, CHUNK), :]
        beta = jax.nn.sigmoid(sm)
        g_all = par_ref[0:1, :] * _softplus(sm + par_ref[1:2, :])
        g = pltpu.roll(g_all, LANES - N_COMBO, axis=1)
        pre = _cumsum_rows(g, reverse=False)
        suf = _cumsum_rows(g, reverse=True)
        b = jnp.where(fwd_lane, pre, suf)
        b_end = jnp.where(fwd_lane, pre[CHUNK - 1:CHUNK], suf[0:1])
        eb = jnp.exp(b)
        zero = jnp.zeros_like(b)
        q4 = (jnp.where(lane8, beta, zero)
              + pltpu.roll(jnp.where(lane8, beta * eb, zero), N_COMBO, axis=1)
              + pltpu.roll(jnp.where(lane8, eb, zero), 2 * N_COMBO, axis=1)
              + pltpu.roll(jnp.where(lane8, jnp.exp(b_end - b), zero), 3 * N_COMBO, axis=1))
        hi, mid, lo3 = _split3(q4)
        nq = N_QTY * N_COMBO
        sm3_s[pl.ds(r0, CHUNK), :] = _bf(hi + pltpu.roll(mid, nq, axis=1) + pltpu.roll(lo3, 2 * nq, axis=1))
        bh, bm, bl = _split3(jnp.where(lane8, b, zero))
        pb = bh + pltpu.roll(bm, N_COMBO, axis=1) + pltpu.roll(bl, 2 * N_COMBO, axis=1)
        ones_lo = (lane < 3 * N_COMBO).astype(F32)
        ones_hi = jnp.logical_and(lane >= 3 * N_COMBO, lane < 6 * N_COMBO).astype(F32)
        xs_s[pl.ds(r0, CHUNK), :] = _bf(pb + ones_hi)
        ys_s[pl.ds(r0, CHUNK), :] = _bf(ones_lo - pltpu.roll(pb, 3 * N_COMBO, axis=1))
        return carry

    lax.fori_loop(0, nchunk, prep, 0)

    o_ref[...] = jnp.zeros_like(o_ref)
    st_s[...] = jnp.zeros_like(st_s)
    bd_mask = bd_ref[...] > 0.5

    combos = [(d, grp) for d in range(N_DIRS) for grp in range(GDN_NGRP)]

    def load_chunk(chunk, d, grp):
        rs = pl.ds(pl.multiple_of(chunk * CHUNK, CHUNK), CHUNK)
        return (qkv_s[rs, grp * w:(grp + 1) * w], qkv_s[rs, 256 + grp * w:256 + (grp + 1) * w],
                qkv_s[rs, 512 + grp * w:512 + (grp + 1) * w], sm3_s[rs, :], xs_s[rs, :], ys_s[rs, :])

    nc = len(combos)
    head_masks = [(lax.broadcasted_iota(jnp.int32, (1, w), 1) // HEAD_DIM) == hh for hh in range(GDN_GROUP)]

    items = [(d, grp) for _ in range(GDN_WINDOW) for d, grp in combos]
    ni = len(items)

    def transform_stages(raws, outs):
        exs, diffs, kqs = [], [], []
        a_mats, ps, rs_ = [], [], []

        def stage_products():
            for n, (d, grp) in enumerate(items):
                q, k, vb, sm3, xs, ys = raws[n]
                exs.append(jnp.dot(sm3, exp_ref[d * GDN_NGRP + grp], preferred_element_type=F32))
                ystack = jnp.concatenate(
                    [ys * _bf(cm_ref[d * N_HEADS + grp * GDN_GROUP + hh:d * N_HEADS + grp * GDN_GROUP + hh + 1, :])
                     for hh in range(GDN_GROUP)], axis=0)
                diffs.append(lax.dot_general(xs, ystack, (((1,), (1,)), ((), ())), preferred_element_type=F32))
                kbd = jnp.concatenate([jnp.where(hm, k, jnp.zeros_like(k)) for hm in head_masks], axis=0)
                kqs.append(lax.dot_general(jnp.concatenate([k, q], axis=0), kbd, (((1,), (1,)), ((), ())),
                                           preferred_element_type=F32))

        def stage_square():
            for n, (d, grp) in enumerate(items):
                tri = tri_ref[d] > 0.5
                stri = tri_ref[2 + d] > 0.5
                decay = jnp.where(tri, jnp.exp(jnp.where(tri, diffs[n], 0.0)), 0.0)
                low = jnp.where(stri, exs[n][:, 0:w] * kqs[n][:CHUNK] * decay, 0.0)
                a_mats.append(kqs[n][CHUNK:] * decay)
                ps.append(_mm(low, _block_diag(low, bd_mask, GDN_GROUP)))
                rs_.append(tri_ref[4] - low)

        def stage_level(last):
            for n in range(ni):
                pbd = _block_diag(ps[n], bd_mask, GDN_GROUP)
                if last:
                    rs_[n] = rs_[n] + _mm(rs_[n], pbd)
                else:
                    rp = _mm(jnp.concatenate([rs_[n], ps[n]], axis=0), pbd)
                    rs_[n] = rs_[n] + rp[:CHUNK]
                    ps[n] = rp[CHUNK:]

        def stage_apply():
            for n, (d, grp) in enumerate(items):
                q, k, vb, sm3, xs, ys = raws[n]
                betax, bebx, ebx, edx = (exs[n][:, t * w:(t + 1) * w] for t in range(N_QTY))
                kf = k.astype(F32)
                u = _mm(rs_[n], _block_diag(betax * vb.astype(F32), bd_mask, GDN_GROUP))
                wm = _mm(rs_[n], _block_diag(bebx * kf, bd_mask, GDN_GROUP))
                cd_row = ebx[CHUNK - 1:CHUNK] if d == 0 else ebx[0:1]
                outs.append((a_mats[n], u, _bf(wm), _bf(q.astype(F32) * ebx), _bf(kf * edx), cd_row))

        return ([stage_products, stage_square] + [functools.partial(stage_level, False)] * 4
                + [functools.partial(stage_level, True), stage_apply])

    def store_transformed(c, t):
        a_mat, u, wm, q_dec, k_dec, cd_row = t
        pa_s[2 * c] = a_mat
        pa_s[2 * c + 1] = u
        pb_s[3 * c] = wm
        pb_s[3 * c + 1] = q_dec
        pb_s[3 * c + 2] = k_dec
        cd_s[c] = jnp.broadcast_to(cd_row, (8, w))

    def load_transformed(c):
        return pa_s[2 * c], pa_s[2 * c + 1], pb_s[3 * c], pb_s[3 * c + 1], pb_s[3 * c + 2], cd_s[c][0:1]

    def scan_stages(states, cur, os_):
        tmp = {}

        def stage_ws(t):
            tmp["ws"] = [_mm(jnp.concatenate([cur[t * nc + c][2], cur[t * nc + c][3]], axis=0), states[c])
                         for c in range(nc)]

        def stage_update(t):
            for c in range(nc):
                a_mat, u, _, _, k_dec, cd_row = cur[t * nc + c]
                v_new = u - tmp["ws"][c][:CHUNK]
                os_.append(tmp["ws"][c][CHUNK:] + _mm(a_mat, _block_diag(v_new, bd_mask, GDN_GROUP)))
                states[c] = cd_row * states[c] + jnp.where(bd_mask, _mm_tn(k_dec, v_new), 0.0)

        stages = []
        for t in range(GDN_WINDOW):
            stages += [functools.partial(stage_ws, t), functools.partial(stage_update, t)]
        return stages

    def chunk_of(j, t, d):
        i = j * GDN_WINDOW + t
        return i if d == 0 else nchunk - 1 - i

    def load_window(j):
        return [load_chunk(chunk_of(j, t, d), d, grp) for t in range(GDN_WINDOW) for d, grp in combos]

    first = []
    for stage in transform_stages(load_window(0), first):
        stage()
    for n in range(ni):
        store_transformed(n, first[n])

    nwin = nchunk // GDN_WINDOW

    def step(j, carry):
        cur = [load_transformed(n) for n in range(ni)]
        states = [st_s[c] for c in range(nc)]
        raws = load_window(jnp.minimum(j + 1, nwin - 1))
        os_, nexts = [], []
        t_stages = transform_stages(raws, nexts)
        s_stages = scan_stages(states, cur, os_)
        for k in range(max(len(t_stages), len(s_stages))):
            if k < len(t_stages):
                t_stages[k]()
            if k < len(s_stages):
                s_stages[k]()
        for t in range(GDN_WINDOW):
            for c, (d, grp) in enumerate(combos):
                rs = pl.ds(pl.multiple_of(chunk_of(j, t, d) * CHUNK, CHUNK), CHUNK)
                o_ref[rs, grp * w:(grp + 1) * w] += os_[t * nc + c]
        for c in range(nc):
            st_s[c] = states[c]
        for n in range(ni):
            store_transformed(n, nexts[n])
        return carry

    lax.fori_loop(0, nwin, step, 0)


def _gdn(gdn_in, small, conv_w, par, consts, batch, seq):
    T = gdn_in.shape[0]
    assert seq % (CHUNK * GDN_WINDOW) == 0
    tri_c, bd, expand, cm, ones256 = consts
    whole = lambda b: (b, 0)
    c2 = lambda b: (0, 0)
    c3 = lambda b: (0, 0, 0)
    return pl.pallas_call(
        _gdn_kernel,
        grid=(batch,),
        in_specs=[
            pl.BlockSpec((seq, 768), whole, pipeline_mode=pl.Buffered(1)),
            pl.BlockSpec((seq, LANES), whole),
            pl.BlockSpec((8, 768), c2),
            pl.BlockSpec((8, LANES), c2),
            pl.BlockSpec((256, 256), c2),
            pl.BlockSpec(tri_c.shape, c3),
            pl.BlockSpec(bd.shape, c2),
            pl.BlockSpec(expand.shape, c3),
            pl.BlockSpec(cm.shape, c2),
        ],
        out_specs=pl.BlockSpec((seq, 256), whole),
        out_shape=jax.ShapeDtypeStruct((T, 256), F32),
        scratch_shapes=[
            pltpu.VMEM((seq, 768), BF16),
            pltpu.VMEM((seq, LANES), BF16),
            pltpu.VMEM((seq, LANES), BF16),
            pltpu.VMEM((seq, LANES), BF16),
            pltpu.VMEM((N_DIRS * GDN_NGRP, GDN_W, GDN_W), F32),
            pltpu.VMEM((2 * GDN_WINDOW * N_DIRS * GDN_NGRP, CHUNK, GDN_W), F32),
            pltpu.VMEM((3 * GDN_WINDOW * N_DIRS * GDN_NGRP, CHUNK, GDN_W), BF16),
            pltpu.VMEM((GDN_WINDOW * N_DIRS * GDN_NGRP, 8, GDN_W), F32),
        ],
        compiler_params=pltpu.CompilerParams(
            dimension_semantics=("arbitrary",), vmem_limit_bytes=VMEM_LIMIT),
        name="gdn",
    )(gdn_in, small, conv_w, par, ones256, tri_c, bd, expand, cm)


GLA_QW = N_HEADS * GLA_DK
GLA_VW = N_HEADS * HEAD_DIM


def _gla_constants():
    i = np.arange(CHUNK)[:, None]
    lane = np.arange(GLA_VW)[None, :]
    j = lane % CHUNK
    tri = np.stack([i >= j, i <= j]).astype(np.float32)
    r = np.arange(GLA_VW)[:, None]
    bd_v = ((r // CHUNK) == (lane // HEAD_DIM)).astype(np.float32)
    rs = np.arange(GLA_QW)[:, None]
    st_mask = ((rs // GLA_DK) == (lane // HEAD_DIM)).astype(np.float32)
    kq_mask = ((r // CHUNK) == (np.arange(GLA_QW)[None, :] // GLA_DK)).astype(np.float32)
    return tri, bd_v, st_mask, kq_mask


def _gla_kernel(x_ref, sm_ref, wup_ref, bg_ref, tri_ref, bdv_ref, stm_ref, kqm_ref, o_ref, st_s):
    seq = x_ref.shape[0]
    nchunk = seq // CHUNK
    o_ref[...] = jnp.zeros_like(o_ref)
    st_s[...] = jnp.zeros_like(st_s)
    bdv_mask = bdv_ref[...] > 0.5
    st_mask = stm_ref[...] > 0.5
    kq_mask = kqm_ref[...] > 0.5

    def rows_of(chunk):
        return pl.ds(pl.multiple_of(chunk * CHUNK, CHUNK), CHUNK)

    def load_chunk(chunk):
        rs = rows_of(chunk)
        return (x_ref[rs, 0:GLA_QW], x_ref[rs, GLA_QW:2 * GLA_QW], x_ref[rs, 2 * GLA_QW:2 * GLA_QW + GLA_VW],
                sm_ref[rs, :])

    def one_chunk(raw, state, d):
        q, k, v, sm = raw
        q = q * (GLA_DK ** -0.5)
        logit = jnp.dot(_bf(sm), wup_ref[d], preferred_element_type=F32) + bg_ref[d:d + 1, :]
        gk = (jnp.minimum(logit, 0.0) - jnp.log1p(jnp.exp(-jnp.abs(logit)))) * (1.0 / GLA_NORMALIZER)
        b = _cumsum_rows(gk, reverse=(d == 1))
        b_end = b[CHUNK - 1:CHUNK] if d == 0 else b[0:1]
        q_e = q * jnp.exp(b)
        k_e = k * jnp.exp(-b)
        k_dec = k * jnp.exp(b_end - b)
        keb = _bf(k_e)
        kebd = jnp.where(kq_mask, jnp.concatenate([keb] * N_HEADS, axis=0), jnp.zeros((GLA_VW, GLA_QW), BF16))
        attn = jnp.where(tri_ref[d] > 0.5, _mm_nt(q_e, kebd), 0.0)
        o = _mm(attn, _block_diag(v, bdv_mask, N_HEADS))
        o = o + _mm(q_e, state)
        cd_col = jnp.transpose(jnp.broadcast_to(jnp.exp(b_end), (GLA_QW, GLA_QW)))
        cd_full = jnp.concatenate([cd_col, cd_col], axis=1)
        return o, cd_full * state + jnp.where(st_mask, _mm_tn(k_dec, v), 0.0)

    def step(i, carry):
        chunks = (i, nchunk - 1 - i)
        raws = [load_chunk(c) for c in chunks]
        states = [st_s[d] for d in range(N_DIRS)]
        outs = [one_chunk(raws[d], states[d], d) for d in range(N_DIRS)]
        for d in range(N_DIRS):
            o_ref[rows_of(chunks[d]), :] += outs[d][0]
            st_s[d] = outs[d][1]
        return carry

    lax.fori_loop(0, nchunk, step, 0)


def _gla(gla_in, small, wup, bg, consts, batch, seq):
    T = gla_in.shape[0]
    tri, bd_v, st_mask, kq_mask = consts
    whole = lambda b: (b, 0)
    c2 = lambda b: (0, 0)
    c3 = lambda b: (0, 0, 0)
    return pl.pallas_call(
        _gla_kernel,
        grid=(batch,),
        in_specs=[
            pl.BlockSpec((seq, 512), whole),
            pl.BlockSpec((seq, LANES), whole),
            pl.BlockSpec(wup.shape, c3),
            pl.BlockSpec(bg.shape, c2),
            pl.BlockSpec(tri.shape, c3),
            pl.BlockSpec(bd_v.shape, c2),
            pl.BlockSpec(st_mask.shape, c2),
            pl.BlockSpec(kq_mask.shape, c2),
        ],
        out_specs=pl.BlockSpec((seq, 256), whole),
        out_shape=jax.ShapeDtypeStruct((T, 256), F32),
        scratch_shapes=[pltpu.VMEM((N_DIRS, GLA_QW, GLA_VW), F32)],
        compiler_params=pltpu.CompilerParams(
            dimension_semantics=("arbitrary",), vmem_limit_bytes=VMEM_LIMIT),
        name="gla",
    )(gla_in, small, wup, bg, tri, bd_v, st_mask, kq_mask)


def _outproj_kernel(ya_ref, ob_ref, oc_ref, ym_ref, gates_ref, gb_ref, gc_ref, ones_ref, w_ref, h_ref,
                    fg_ref, o_ref, *, final):
    ones256 = ones_ref[...]

    def head_norm(x, g):
        ms = _group_sum(x * x, ones256) * (1.0 / HEAD_DIM)
        return x * lax.rsqrt(ms + EPS) * g

    yb = head_norm(ob_ref[...], gb_ref[...]) * gates_ref[:, 0:256]
    yc = head_norm(oc_ref[...], gc_ref[...]) * gates_ref[:, 256:512]
    y = jnp.dot(_bf(ya_ref[...]), w_ref[0:256, :], preferred_element_type=F32)
    y = y + jnp.dot(_bf(yb), w_ref[256:512, :], preferred_element_type=F32)
    y = y + jnp.dot(_bf(yc), w_ref[512:768, :], preferred_element_type=F32)
    y = y + jnp.dot(_bf(ym_ref[...]), w_ref[768:1024, :], preferred_element_type=F32)
    hn = h_ref[...] + y
    if final:
        hn = hn * lax.rsqrt(jnp.mean(hn * hn, axis=-1, keepdims=True) + EPS) * fg_ref[...]
    o_ref[...] = hn


def _outproj(ya, ob, oc, ym, gates, gb, gc, ones256, w, h, fg, tm, final):
    T, D = h.shape
    row = lambda i: (i, 0)
    const = lambda i: (0, 0)
    return pl.pallas_call(
        functools.partial(_outproj_kernel, final=final),
        grid=(T // tm,),
        in_specs=[
            pl.BlockSpec((tm, 256), row), pl.BlockSpec((tm, 256), row),
            pl.BlockSpec((tm, 256), row), pl.BlockSpec((tm, 256), row),
            pl.BlockSpec((tm, 512), lambda i: (i, 1)),
            pl.BlockSpec((1, 256), const), pl.BlockSpec((1, 256), const),
            pl.BlockSpec((256, 256), const),
            pl.BlockSpec((D, D), const),
            pl.BlockSpec((tm, D), row),
            pl.BlockSpec((1, D), const),
        ],
        out_specs=pl.BlockSpec((tm, D), row),
        out_shape=jax.ShapeDtypeStruct((T, D), F32),
        compiler_params=pltpu.CompilerParams(
            dimension_semantics=("arbitrary",), vmem_limit_bytes=VMEM_LIMIT),
        name="outproj_final" if final else "outproj",
    )(ya, ob, oc, ym, gates, gb, gc, ones256, w, h, fg)


def _block_ones(n, group):
    i = np.arange(n)
    return jnp.asarray((i[:, None] // group) == (i[None, :] // group), dtype=BF16)


def _head_perm_cols():
    return np.concatenate([np.arange(h * HEAD_DIM, (h + 1) * HEAD_DIM) for h in ATT_HEAD_PERM])


def _prep_w_in(w_in):
    o = np.cumsum([0, 256, 128, 128, 256, 768, 8, 8, 256, 128, 128, 256, 32, 256, 256, 256])
    (a_q, a_k, a_v, a_g, d_qkv, d_beta, d_alpha, d_gate, l_q, l_k, l_v, l_low, l_gate, m_q, m_gate) = [
        w_in[:, :, o[i]:o[i + 1]] for i in range(15)]
    perm = _head_perm_cols()
    pad = jnp.zeros(w_in.shape[:2] + (N_COLS - C_SM - 48,), w_in.dtype)
    cols = [a_q[:, :, perm], a_k, a_v, a_g[:, :, perm], d_qkv, d_gate, l_q, l_k, l_v, l_gate, m_q, m_gate,
            d_beta, d_alpha, l_low, pad]
    return jnp.concatenate(cols, axis=-1).astype(BF16)


def _rope_tables(seq):
    rows = seq // GRID_W
    row_pos = jnp.repeat(jnp.arange(rows, dtype=F32), GRID_W)
    col_pos = jnp.tile(jnp.arange(GRID_W, dtype=F32), rows)
    half = HEAD_DIM // 4
    inv_freq = ROPE_THETA ** (-jnp.arange(0, 2 * half, 2, dtype=F32) / (2 * half))
    ang = jnp.stack([row_pos, col_pos], axis=-1)[..., None] * inv_freq
    cos, sin = jnp.cos(ang), jnp.sin(ang)
    cos_h = jnp.concatenate([cos, cos], axis=-1).reshape(seq, HEAD_DIM)
    sin_h = jnp.concatenate([-sin, sin], axis=-1).reshape(seq, HEAD_DIM)
    return jnp.tile(cos_h, (1, 2)), jnp.tile(sin_h, (1, 2))


def _trunk(x, mem, p, consts):
    batch, seq, d_model = x.shape
    T = batch * seq
    depth = p["w_in"].shape[0]
    tm = 512 if seq % 512 == 0 else seq
    tq = 256 if seq % 256 == 0 else seq
    kvc = 512 if seq % 512 == 0 else seq
    cos_t, sin_t = _rope_tables(seq)
    mem2d = mem.reshape(batch * mem.shape[1], d_model)
    memkv = _memkv(mem2d, p["mem_norm_g"], p["w_mem_kv"], min(512, mem2d.shape[0]))
    h = x.reshape(T, d_model)
    for d in range(depth):
        qkv, gates, gdn_in, gla_in, mq, small = _inproj(
            h, p["norm_g"][d], p["w_in"][d], cos_t, sin_t, p["qkg"][d], consts["ones384"], seq, tm)
        ya, ym = _attention(qkv, mq, memkv[d], gates, batch, seq, tq, kvc)
        ob = _gdn(gdn_in, small, p["conv_w"][d], p["gdn_par"][d], consts["gdn"], batch, seq)
        oc = _gla(gla_in, small, p["wup"][d], p["bg"][d], consts["gla"], batch, seq)
        h = _outproj(ya, ob, oc, ym, gates, p["gb"][d], p["gc"][d],
                     consts["ones256"], p["w_out"][d], h, p["final_g"], tm, final=(d == depth - 1))
    return h.reshape(batch, seq, d_model)


def kernel(x_prompt, x_sample, mem_prompt, mem_sample, norm_g, w_in, att_q_norm_g, att_k_norm_g, gdn_conv_w,
           gdn_a_log, gdn_dt_bias, gdn_out_norm_g, gla_w_gate_up, gla_b_gate, gla_out_norm_g, mem_norm_g,
           w_mem_kv, w_out, final_norm_g):
    depth, d_model = norm_g.shape
    perm = _head_perm_cols()
    w_out_p = jnp.concatenate([w_out[:, :256][:, perm], w_out[:, 256:]], axis=1).astype(BF16)
    qkg = jnp.concatenate([jnp.tile(att_q_norm_g, (1, N_HEADS)), jnp.tile(att_k_norm_g, (1, KV_HEADS))], axis=1)
    zeros8 = jnp.zeros((depth, N_COMBO), F32)
    par_a = jnp.concatenate([zeros8, -jnp.exp(gdn_a_log.reshape(depth, N_COMBO)),
                             jnp.zeros((depth, LANES - 2 * N_COMBO), F32)], axis=1)
    par_b = jnp.concatenate([zeros8, gdn_dt_bias.reshape(depth, N_COMBO),
                             jnp.zeros((depth, LANES - 2 * N_COMBO), F32)], axis=1)
    gdn_par = jnp.concatenate([par_a[:, None], par_b[:, None], jnp.zeros((depth, 6, LANES), F32)], axis=1)
    conv_w = jnp.concatenate([gdn_conv_w, jnp.zeros((depth, 5, gdn_conv_w.shape[2]), F32)], axis=1)
    wup = jnp.zeros((depth, N_DIRS, LANES, GLA_QW), F32)
    for d in range(N_DIRS):
        lo = 2 * N_COMBO + d * GLA_RANK
        wup = wup.at[:, d, lo:lo + GLA_RANK, :].set(gla_w_gate_up[:, d])
    p = {
        "norm_g": norm_g.reshape(depth, 1, d_model),
        "w_in": _prep_w_in(w_in),
        "qkg": qkg.reshape(depth, 1, 384),
        "conv_w": conv_w,
        "gdn_par": gdn_par,
        "wup": wup.astype(BF16),
        "bg": jnp.concatenate([gla_b_gate, jnp.zeros((depth, 6, GLA_QW), F32)], axis=1),
        "gb": jnp.tile(gdn_out_norm_g, (1, N_HEADS)).reshape(depth, 1, 256),
        "gc": jnp.tile(gla_out_norm_g, (1, N_HEADS)).reshape(depth, 1, 256),
        "mem_norm_g": mem_norm_g.reshape(depth, 1, d_model),
        "w_mem_kv": w_mem_kv.astype(BF16),
        "w_out": w_out_p,
        "final_g": final_norm_g.reshape(1, d_model),
    }
    tri_c, bd, expand, cm = _gdn_constants()
    consts = {
        "ones384": _block_ones(384, HEAD_DIM),
        "ones256": _block_ones(256, HEAD_DIM),
        "gdn": (jnp.asarray(tri_c), jnp.asarray(bd), jnp.asarray(expand, dtype=BF16), jnp.asarray(cm),
                _block_ones(256, HEAD_DIM)),
        "gla": tuple(jnp.asarray(a) for a in _gla_constants()),
    }
    y_prompt = _trunk(x_prompt, mem_prompt, p, consts)
    y_sample = _trunk(x_sample, mem_sample, p, consts)
    return (y_prompt, y_sample)
```

```python
import functools

import numpy as np
import jax
import jax.numpy as jnp
from jax import lax
from jax.experimental import pallas as pl
from jax.experimental.pallas import tpu as pltpu

F32 = jnp.float32
BF16 = jnp.bfloat16

N_HEADS = 4
HEAD_DIM = 64
KV_HEADS = 2
GRID_W = 64
ROPE_THETA = 10000.0
CHUNK = 64
GLA_DK = 32
GLA_RANK = 16
GLA_NORMALIZER = 16.0
EPS = 1e-6
N_DIRS = 2
LANES = 128
GDN_GROUP = 2
GDN_WINDOW = 4
VMEM_LIMIT = 56 * 1024 * 1024

C_AQ, C_AK, C_AV, C_AG = 0, 256, 384, 512
C_DQKV, C_DG = 768, 1536
C_LQ, C_LK, C_LV, C_LG = 1792, 1920, 2048, 2304
C_MQ, C_MG = 2560, 2816
C_SM = 3072
N_COLS = 3200
ATT_HEAD_PERM = (0, 2, 1, 3)


def _bf(x):
    return x.astype(BF16)


def _mm(a, b):
    return jnp.dot(_bf(a), _bf(b), preferred_element_type=F32)


def _mm_nt(a, b):
    return lax.dot_general(_bf(a), _bf(b), (((1,), (1,)), ((), ())), preferred_element_type=F32)


def _mm_tn(a, b):
    return lax.dot_general(_bf(a), _bf(b), (((0,), (0,)), ((), ())), preferred_element_type=F32)


def _split2(x):
    hi = _bf(x)
    lo = _bf(x - hi.astype(F32))
    return hi, lo


def _split3(x):
    hi = _bf(x).astype(F32)
    r = x - hi
    mid = _bf(r).astype(F32)
    lo = _bf(r - mid).astype(F32)
    return hi, mid, lo


def _group_sum(x, ones_blocks):
    hi, lo = _split2(x)
    return (jnp.dot(hi, ones_blocks, preferred_element_type=F32)
            + jnp.dot(lo, ones_blocks, preferred_element_type=F32))


def _silu(x):
    return x * jax.nn.sigmoid(x)


def _softplus(x):
    return jnp.maximum(x, 0.0) + jnp.log1p(jnp.exp(-jnp.abs(x)))


def _cumsum_rows(x, reverse):
    n = x.shape[0]
    row = lax.broadcasted_iota(jnp.int32, x.shape, 0)
    s = 1
    while s < n:
        if reverse:
            x = x + jnp.where(row < n - s, pltpu.roll(x, n - s, axis=0), 0.0)
        else:
            x = x + jnp.where(row >= s, pltpu.roll(x, s, axis=0), 0.0)
        s *= 2
    return x


def _block_diag(x, mask, reps):
    xb = _bf(x)
    t = jnp.concatenate([xb] * reps, axis=0)
    return jnp.where(mask, t, jnp.zeros_like(t))


def _inproj_kernel(h_ref, g_ref, w_ref, cos_ref, sin_ref, qkg_ref, ones_ref,
                   qkv_ref, gates_ref, gdn_ref, gla_ref, mq_ref, small_ref):
    x = h_ref[...]
    xn = x * lax.rsqrt(jnp.mean(x * x, axis=-1, keepdims=True) + EPS) * g_ref[...]
    proj = jnp.dot(_bf(xn), w_ref[...], preferred_element_type=F32)

    qk = proj[:, C_AQ:C_AV]
    ms = _group_sum(qk * qk, ones_ref[...]) * (1.0 / HEAD_DIM)
    qkn = qk * lax.rsqrt(ms + EPS) * qkg_ref[...]
    cos = cos_ref[...]
    sin = sin_ref[...]
    lane = lax.broadcasted_iota(jnp.int32, (1, LANES), 1)
    first_half = (lane % 32) < 16
    outs = []
    for c in range(3):
        xc = qkn[:, c * LANES:(c + 1) * LANES]
        partner = jnp.where(first_half, pltpu.roll(xc, LANES - 16, axis=1), pltpu.roll(xc, 16, axis=1))
        outs.append(xc * cos + partner * sin)
    qkv_ref[:, 0:128] = _bf(outs[0] * (HEAD_DIM ** -0.5))
    qkv_ref[:, 128:256] = _bf(outs[1] * (HEAD_DIM ** -0.5))
    qkv_ref[:, 256:384] = _bf(outs[2])
    qkv_ref[:, 384:512] = _bf(proj[:, C_AV:C_AG])

    gates_ref[:, 0:256] = _silu(proj[:, C_AG:C_AG + 256])
    gates_ref[:, 256:512] = _silu(proj[:, C_MG:C_MG + 256])
    gates_ref[:, 512:768] = _silu(proj[:, C_DG:C_DG + 256])
    gates_ref[:, 768:1024] = _silu(proj[:, C_LG:C_LG + 256])
    gdn_ref[...] = proj[:, C_DQKV:C_DG]
    gla_ref[...] = proj[:, C_LQ:C_LG]
    mq_ref[...] = _bf(proj[:, C_MQ:C_MG] * (HEAD_DIM ** -0.5))
    small_ref[...] = proj[:, C_SM:N_COLS]


def _inproj(h, g, w, cos_t, sin_t, qkg, ones384, seq, tm):
    T, D = h.shape
    nseq = seq // tm
    row = lambda i: (i, 0)
    const = lambda i: (0, 0)
    return pl.pallas_call(
        _inproj_kernel,
        grid=(T // tm,),
        in_specs=[
            pl.BlockSpec((tm, D), row),
            pl.BlockSpec((1, D), const),
            pl.BlockSpec((D, N_COLS), const),
            pl.BlockSpec((tm, LANES), lambda i: (i % nseq, 0)),
            pl.BlockSpec((tm, LANES), lambda i: (i % nseq, 0)),
            pl.BlockSpec((1, 384), const),
            pl.BlockSpec((384, 384), const),
        ],
        out_specs=[
            pl.BlockSpec((tm, 512), row),
            pl.BlockSpec((tm, 1024), row),
            pl.BlockSpec((tm, 768), row),
            pl.BlockSpec((tm, 512), row),
            pl.BlockSpec((tm, 256), row),
            pl.BlockSpec((tm, LANES), row),
        ],
        out_shape=[
            jax.ShapeDtypeStruct((T, 512), BF16),
            jax.ShapeDtypeStruct((T, 1024), F32),
            jax.ShapeDtypeStruct((T, 768), F32),
            jax.ShapeDtypeStruct((T, 512), F32),
            jax.ShapeDtypeStruct((T, 256), BF16),
            jax.ShapeDtypeStruct((T, LANES), F32),
        ],
        compiler_params=pltpu.CompilerParams(
            dimension_semantics=("arbitrary",), vmem_limit_bytes=VMEM_LIMIT),
        name="inproj",
    )(h, g, w, cos_t, sin_t, qkg, ones384)


def _memkv_kernel(m_ref, g_ref, w_ref, o_ref):
    x = m_ref[...]
    xn = x * lax.rsqrt(jnp.mean(x * x, axis=-1, keepdims=True) + EPS) * g_ref[0]
    o_ref[0] = _bf(jnp.dot(_bf(xn), w_ref[0], preferred_element_type=F32))


def _memkv(mem2d, g, w, tr):
    R, D = mem2d.shape
    depth = w.shape[0]
    return pl.pallas_call(
        _memkv_kernel,
        grid=(depth, R // tr),
        in_specs=[
            pl.BlockSpec((tr, D), lambda d, i: (i, 0)),
            pl.BlockSpec((1, 1, D), lambda d, i: (d, 0, 0)),
            pl.BlockSpec((1, D, 512), lambda d, i: (d, 0, 0)),
        ],
        out_specs=pl.BlockSpec((1, tr, 512), lambda d, i: (d, i, 0)),
        out_shape=jax.ShapeDtypeStruct((depth, R, 512), BF16),
        compiler_params=pltpu.CompilerParams(
            dimension_semantics=("arbitrary", "arbitrary"), vmem_limit_bytes=VMEM_LIMIT),
        name="memkv",
    )(mem2d, g, w)


def _attn_kernel(q_ref, k_ref, v_ref, mq_ref, km_ref, vm_ref, ga_ref, gm_ref, ya_ref, ym_ref, vt_s, *, kvc):
    tq = q_ref.shape[0]
    seq = k_ref.shape[0]
    lane = lax.broadcasted_iota(jnp.int32, (1, LANES), 1)
    lo = lane < HEAD_DIM

    def stack_heads(x):
        z = jnp.zeros_like(x)
        return [jnp.where(lo, x, z), jnp.where(lo, z, x)]

    @pl.when(pl.program_id(1) == 0)
    def _():
        sub = lax.broadcasted_iota(jnp.int32, (LANES, 1), 0) < HEAD_DIM
        for c in range(seq // kvc):
            vt = jnp.transpose(v_ref[c * kvc:(c + 1) * kvc, :].astype(F32))
            vt_s[0, c] = _bf(jnp.where(sub, vt, 1.0))
            vt_s[1, c] = _bf(jnp.where(sub, 1.0, vt))

    q = q_ref[...]
    qm = stack_heads(q[:, :LANES]) + stack_heads(q[:, LANES:])
    n_heads = len(qm)

    def scores(c):
        kc = k_ref[c * kvc:(c + 1) * kvc, :]
        return [lax.dot_general(kc, qm[h], (((1,), (1,)), ((), ())), preferred_element_type=F32)
                for h in range(n_heads)]

    ms = [jnp.full((1, tq), -jnp.inf, F32) for _ in range(n_heads)]
    accs = [jnp.zeros((LANES, tq), F32) for _ in range(n_heads)]
    nsub = seq // kvc
    sts = scores(0)
    for c in range(nsub):
        nxt = scores(c + 1) if c + 1 < nsub else None
        new_ms = [jnp.maximum(ms[h], jnp.max(sts[h], axis=0, keepdims=True)) for h in range(n_heads)]
        ps = [_bf(jnp.exp(sts[h] - new_ms[h])) for h in range(n_heads)]
        accs = [jnp.exp(ms[h] - new_ms[h]) * accs[h]
                + jnp.dot(vt_s[h % 2, c], ps[h], preferred_element_type=F32) for h in range(n_heads)]
        ms, sts = new_ms, nxt
    o_t = []
    for h in range(n_heads):
        if h % 2 == 0:
            o_t.append(accs[h][0:HEAD_DIM] / accs[h][HEAD_DIM:HEAD_DIM + 1])
        else:
            o_t.append(accs[h][HEAD_DIM:LANES] / accs[h][0:1])
    ya = jnp.transpose(jnp.concatenate(o_t, axis=0))
    ya_ref[...] = ya * ga_ref[...]

    mq = mq_ref[...]
    outs = []
    for c in range(2):
        qc = jnp.concatenate(stack_heads(mq[:, c * LANES:(c + 1) * LANES]), axis=0)
        kc = km_ref[:, c * LANES:(c + 1) * LANES]
        vc = vm_ref[:, c * LANES:(c + 1) * LANES]
        s = lax.dot_general(qc, kc, (((1,), (1,)), ((), ())), preferred_element_type=F32)
        p = jnp.exp(s - jnp.max(s, axis=-1, keepdims=True))
        om = jnp.dot(_bf(p), vc, preferred_element_type=F32) / jnp.sum(p, axis=-1, keepdims=True)
        outs.append(jnp.where(lo, om[0:tq], om[tq:2 * tq]))
    ym_ref[...] = jnp.concatenate(outs, axis=1) * gm_ref[...]


def _attention(qkv, mq, memkv, gates, batch, seq, tq, kvc):
    T = qkv.shape[0]
    nq = seq // tq
    mem_len = memkv.shape[0] // batch
    qrow = lambda b, i: (b * nq + i, 0)
    return pl.pallas_call(
        functools.partial(_attn_kernel, kvc=kvc),
        grid=(batch, nq),
        in_specs=[
            pl.BlockSpec((tq, 256), qrow),
            pl.BlockSpec((seq, LANES), lambda b, i: (b, 2)),
            pl.BlockSpec((seq, LANES), lambda b, i: (b, 3)),
            pl.BlockSpec((tq, 256), qrow),
            pl.BlockSpec((mem_len, 256), lambda b, i: (b, 0)),
            pl.BlockSpec((mem_len, 256), lambda b, i: (b, 1)),
            pl.BlockSpec((tq, 256), lambda b, i: (b * nq + i, 0)),
            pl.BlockSpec((tq, 256), lambda b, i: (b * nq + i, 1)),
        ],
        out_specs=[pl.BlockSpec((tq, 256), qrow), pl.BlockSpec((tq, 256), qrow)],
        out_shape=[jax.ShapeDtypeStruct((T, 256), F32), jax.ShapeDtypeStruct((T, 256), F32)],
        scratch_shapes=[pltpu.VMEM((KV_HEADS, seq // kvc, LANES, kvc), BF16)],
        compiler_params=pltpu.CompilerParams(
            dimension_semantics=("arbitrary", "arbitrary"), vmem_limit_bytes=VMEM_LIMIT),
        name="attention",
    )(qkv, qkv, qkv, mq, memkv, memkv, gates, gates)


GDN_W = GDN_GROUP * HEAD_DIM
GDN_NGRP = N_HEADS // GDN_GROUP
N_COMBO = N_DIRS * N_HEADS
N_QTY = 4


def _gdn_constants():
    w = GDN_W
    i = np.arange(CHUNK)[:, None]
    lane = np.arange(w)[None, :]
    j = lane % CHUNK
    tri = np.stack([i >= j, i <= j]).astype(np.float32)
    stri = np.stack([i > j, i < j]).astype(np.float32)
    eye = (i == j).astype(np.float32)[None]
    tri_c = np.concatenate([tri, stri, eye], axis=0)
    r = np.arange(w)[:, None]
    bd = ((r // CHUNK) == (lane // CHUNK)).astype(np.float32)
    expand = np.zeros((N_DIRS * GDN_NGRP, LANES, N_QTY * w), np.float32)
    for d in range(N_DIRS):
        for g in range(GDN_NGRP):
            for qty in range(N_QTY):
                for hh in range(GDN_GROUP):
                    combo = d * N_HEADS + g * GDN_GROUP + hh
                    for piece in range(3):
                        src = piece * (N_QTY * N_COMBO) + qty * N_COMBO + combo
                        expand[d * GDN_NGRP + g, src, qty * w + hh * CHUNK: qty * w + (hh + 1) * CHUNK] = 1.0
    cm = np.zeros((N_COMBO, LANES), np.float32)
    for c in range(N_COMBO):
        for t in range(6):
            cm[c, t * N_COMBO + c] = 1.0
    return tri_c, bd, expand, cm


def _gdn_kernel(x_ref, sm_ref, cw_ref, par_ref, ones_ref, tri_ref, bd_ref, exp_ref, cm_ref,
                o_ref, qkv_s, sm3_s, xs_s, ys_s, st_s, pa_s, pb_s, cd_s):
    seq = x_ref.shape[0]
    nchunk = seq // CHUNK
    w = GDN_W
    lane = lax.broadcasted_iota(jnp.int32, (1, LANES), 1)
    row = lax.broadcasted_iota(jnp.int32, (CHUNK, 1), 0)
    lane8 = lane < N_COMBO
    fwd_lane = lane < N_HEADS
    ones256 = ones_ref[...]

    def prep(c, carry):
        r0 = pl.multiple_of(c * CHUNK, CHUNK)
        xc = x_ref[pl.ds(r0, CHUNK), :]
        pstart = pl.multiple_of(jnp.maximum(r0 - 8, 0), 8)
        nstart = pl.multiple_of(jnp.minimum(r0 + CHUNK, seq - 8), 8)
        prev_row = x_ref[pl.ds(pstart, 8), :][7:8] * jnp.where(c > 0, 1.0, 0.0)
        next_row = x_ref[pl.ds(nstart, 8), :][0:1] * jnp.where(c < nchunk - 1, 1.0, 0.0)
        x_prev = jnp.where(row == 0, prev_row, pltpu.roll(xc, 1, axis=0))
        x_next = jnp.where(row == CHUNK - 1, next_row, pltpu.roll(xc, CHUNK - 1, axis=0))
        y = x_prev * cw_ref[0:1, :] + xc * cw_ref[1:2, :]
        y = y + x_next * cw_ref[2:3, :]
        y = _silu(y)
        for part in range(2):
            z = y[:, part * 256:(part + 1) * 256]
            ss = _group_sum(z * z, ones256)
            zn = z * lax.rsqrt(ss + EPS)
            if part == 0:
                zn = zn * (HEAD_DIM ** -0.5)
            qkv_s[pl.ds(r0, CHUNK), part * 256:(part + 1) * 256] = _bf(zn)
        qkv_s[pl.ds(r0, CHUNK), 512:768] = _bf(y[:, 512:768])

        sm = sm_ref[pl.ds(r0, CHUNK), :]
        beta = jax.nn.sigmoid(sm)
        g_all = par_ref[0:1, :] * _softplus(sm + par_ref[1:2, :])
        g = pltpu.roll(g_all, LANES - N_COMBO, axis=1)
        pre = _cumsum_rows(g, reverse=False)
        suf = _cumsum_rows(g, reverse=True)
        b = jnp.where(fwd_lane, pre, suf)
        b_end = jnp.where(fwd_lane, pre[CHUNK - 1:CHUNK], suf[0:1])
        eb = jnp.exp(b)
        zero = jnp.zeros_like(b)
        q4 = (jnp.where(lane8, beta, zero)
              + pltpu.roll(jnp.where(lane8, beta * eb, zero), N_COMBO, axis=1)
              + pltpu.roll(jnp.where(lane8, eb, zero), 2 * N_COMBO, axis=1)
              + pltpu.roll(jnp.where(lane8, jnp.exp(b_end - b), zero), 3 * N_COMBO, axis=1))
        hi, mid, lo3 = _split3(q4)
        nq = N_QTY * N_COMBO
        sm3_s[pl.ds(r0, CHUNK), :] = _bf(hi + pltpu.roll(mid, nq, axis=1) + pltpu.roll(lo3, 2 * nq, axis=1))
        bh, bm, bl = _split3(jnp.where(lane8, b, zero))
        pb = bh + pltpu.roll(bm, N_COMBO, axis=1) + pltpu.roll(bl, 2 * N_COMBO, axis=1)
        ones_lo = (lane < 3 * N_COMBO).astype(F32)
        ones_hi = jnp.logical_and(lane >= 3 * N_COMBO, lane < 6 * N_COMBO).astype(F32)
        xs_s[pl.ds(r0, CHUNK), :] = _bf(pb + ones_hi)
        ys_s[pl.ds(r0, CHUNK), :] = _bf(ones_lo - pltpu.roll(pb, 3 * N_COMBO, axis=1))
        return carry

    lax.fori_loop(0, nchunk, prep, 0)

    o_ref[...] = jnp.zeros_like(o_ref)
    st_s[...] = jnp.zeros_like(st_s)
    bd_mask = bd_ref[...] > 0.5

    combos = [(d, grp) for d in range(N_DIRS) for grp in range(GDN_NGRP)]

    def load_chunk(chunk, d, grp):
        rs = pl.ds(pl.multiple_of(chunk * CHUNK, CHUNK), CHUNK)
        return (qkv_s[rs, grp * w:(grp + 1) * w], qkv_s[rs, 256 + grp * w:256 + (grp + 1) * w],
                qkv_s[rs, 512 + grp * w:512 + (grp + 1) * w], sm3_s[rs, :], xs_s[rs, :], ys_s[rs, :])

    nc = len(combos)
    head_masks = [(lax.broadcasted_iota(jnp.int32, (1, w), 1) // HEAD_DIM) == hh for hh in range(GDN_GROUP)]

    items = [(d, grp) for _ in range(GDN_WINDOW) for d, grp in combos]
    ni = len(items)

    def transform_stages(raws, outs):
        exs, diffs, kqs = [], [], []
        a_mats, ps, rs_ = [], [], []

        def stage_products():
            for n, (d, grp) in enumerate(items):
                q, k, vb, sm3, xs, ys = raws[n]
                exs.append(jnp.dot(sm3, exp_ref[d * GDN_NGRP + grp], preferred_element_type=F32))
                ystack = jnp.concatenate(
                    [ys * _bf(cm_ref[d * N_HEADS + grp * GDN_GROUP + hh:d * N_HEADS + grp * GDN_GROUP + hh + 1, :])
                     for hh in range(GDN_GROUP)], axis=0)
                diffs.append(lax.dot_general(xs, ystack, (((1,), (1,)), ((), ())), preferred_element_type=F32))
                kbd = jnp.concatenate([jnp.where(hm, k, jnp.zeros_like(k)) for hm in head_masks], axis=0)
                kqs.append(lax.dot_general(jnp.concatenate([k, q], axis=0), kbd, (((1,), (1,)), ((), ())),
                                           preferred_element_type=F32))

        def stage_square():
            for n, (d, grp) in enumerate(items):
                tri = tri_ref[d] > 0.5
                stri = tri_ref[2 + d] > 0.5
                decay = jnp.where(tri, jnp.exp(jnp.where(tri, diffs[n], 0.0)), 0.0)
                low = jnp.where(stri, exs[n][:, 0:w] * kqs[n][:CHUNK] * decay, 0.0)
                a_mats.append(kqs[n][CHUNK:] * decay)
                ps.append(_mm(low, _block_diag(low, bd_mask, GDN_GROUP)))
                rs_.append(tri_ref[4] - low)

        def stage_level(last):
            for n in range(ni):
                pbd = _block_diag(ps[n], bd_mask, GDN_GROUP)
                if last:
                    rs_[n] = rs_[n] + _mm(rs_[n], pbd)
                else:
                    rp = _mm(jnp.concatenate([rs_[n], ps[n]], axis=0), pbd)
                    rs_[n] = rs_[n] + rp[:CHUNK]
                    ps[n] = rp[CHUNK:]

        def stage_apply():
            for n, (d, grp) in enumerate(items):
                q, k, vb, sm3, xs, ys = raws[n]
                betax, bebx, ebx, edx = (exs[n][:, t * w:(t + 1) * w] for t in range(N_QTY))
                kf = k.astype(F32)
                u = _mm(rs_[n], _block_diag(betax * vb.astype(F32), bd_mask, GDN_GROUP))
                wm = _mm(rs_[n], _block_diag(bebx * kf, bd_mask, GDN_GROUP))
                cd_row = ebx[CHUNK - 1:CHUNK] if d == 0 else ebx[0:1]
                outs.append((a_mats[n], u, _bf(wm), _bf(q.astype(F32) * ebx), _bf(kf * edx), cd_row))

        return ([stage_products, stage_square] + [functools.partial(stage_level, False)] * 4
                + [functools.partial(stage_level, True), stage_apply])

    def store_transformed(c, t):
        a_mat, u, wm, q_dec, k_dec, cd_row = t
        pa_s[2 * c] = a_mat
        pa_s[2 * c + 1] = u
        pb_s[3 * c] = wm
        pb_s[3 * c + 1] = q_dec
        pb_s[3 * c + 2] = k_dec
        cd_s[c] = jnp.broadcast_to(cd_row, (8, w))

    def load_transformed(c):
        return pa_s[2 * c], pa_s[2 * c + 1], pb_s[3 * c], pb_s[3 * c + 1], pb_s[3 * c + 2], cd_s[c][0:1]

    def scan_stages(states, cur, os_):
        tmp = {}

        def stage_ws(t):
            tmp["ws"] = [_mm(jnp.concatenate([cur[t * nc + c][2], cur[t * nc + c][3]], axis=0), states[c])
                         for c in range(nc)]

        def stage_update(t):
            for c in range(nc):
                a_mat, u, _, _, k_dec, cd_row = cur[t * nc + c]
                v_new = u - tmp["ws"][c][:CHUNK]
                os_.append(tmp["ws"][c][CHUNK:] + _mm(a_mat, _block_diag(v_new, bd_mask, GDN_GROUP)))
                states[c] = cd_row * states[c] + jnp.where(bd_mask, _mm_tn(k_dec, v_new), 0.0)

        stages = []
        for t in range(GDN_WINDOW):
            stages += [functools.partial(stage_ws, t), functools.partial(stage_update, t)]
        return stages

    def chunk_of(j, t, d):
        i = j * GDN_WINDOW + t
        return i if d == 0 else nchunk - 1 - i

    def load_window(j):
        return [load_chunk(chunk_of(j, t, d), d, grp) for t in range(GDN_WINDOW) for d, grp in combos]

    first = []
    for stage in transform_stages(load_window(0), first):
        stage()
    for n in range(ni):
        store_transformed(n, first[n])

    nwin = nchunk // GDN_WINDOW

    def step(j, carry):
        cur = [load_transformed(n) for n in range(ni)]
        states = [st_s[c] for c in range(nc)]
        raws = load_window(jnp.minimum(j + 1, nwin - 1))
        os_, nexts = [], []
        t_stages = transform_stages(raws, nexts)
        s_stages = scan_stages(states, cur, os_)
        for k in range(max(len(t_stages), len(s_stages))):
            if k < len(t_stages):
                t_stages[k]()
            if k < len(s_stages):
                s_stages[k]()
        for t in range(GDN_WINDOW):
            for c, (d, grp) in enumerate(combos):
                rs = pl.ds(pl.multiple_of(chunk_of(j, t, d) * CHUNK, CHUNK), CHUNK)
                o_ref[rs, grp * w:(grp + 1) * w] += os_[t * nc + c]
        for c in range(nc):
            st_s[c] = states[c]
        for n in range(ni):
            store_transformed(n, nexts[n])
        return carry

    lax.fori_loop(0, nwin, step, 0)


def _gdn(gdn_in, small, conv_w, par, consts, batch, seq):
    T = gdn_in.shape[0]
    assert seq % (CHUNK * GDN_WINDOW) == 0
    tri_c, bd, expand, cm, ones256 = consts
    whole = lambda b: (b, 0)
    c2 = lambda b: (0, 0)
    c3 = lambda b: (0, 0, 0)
    return pl.pallas_call(
        _gdn_kernel,
        grid=(batch,),
        in_specs=[
            pl.BlockSpec((seq, 768), whole, pipeline_mode=pl.Buffered(1)),
            pl.BlockSpec((seq, LANES), whole),
            pl.BlockSpec((8, 768), c2),
            pl.BlockSpec((8, LANES), c2),
            pl.BlockSpec((256, 256), c2),
            pl.BlockSpec(tri_c.shape, c3),
            pl.BlockSpec(bd.shape, c2),
            pl.BlockSpec(expand.shape, c3),
            pl.BlockSpec(cm.shape, c2),
        ],
        out_specs=pl.BlockSpec((seq, 256), whole),
        out_shape=jax.ShapeDtypeStruct((T, 256), F32),
        scratch_shapes=[
            pltpu.VMEM((seq, 768), BF16),
            pltpu.VMEM((seq, LANES), BF16),
            pltpu.VMEM((seq, LANES), BF16),
            pltpu.VMEM((seq, LANES), BF16),
            pltpu.VMEM((N_DIRS * GDN_NGRP, GDN_W, GDN_W), F32),
            pltpu.VMEM((2 * GDN_WINDOW * N_DIRS * GDN_NGRP, CHUNK, GDN_W), F32),
            pltpu.VMEM((3 * GDN_WINDOW * N_DIRS * GDN_NGRP, CHUNK, GDN_W), BF16),
            pltpu.VMEM((GDN_WINDOW * N_DIRS * GDN_NGRP, 8, GDN_W), F32),
        ],
        compiler_params=pltpu.CompilerParams(
            dimension_semantics=("arbitrary",), vmem_limit_bytes=VMEM_LIMIT),
        name="gdn",
    )(gdn_in, small, conv_w, par, ones256, tri_c, bd, expand, cm)


GLA_QW = N_HEADS * GLA_DK
GLA_VW = N_HEADS * HEAD_DIM
GLA_WINDOW = 4


def _gla_constants():
    i = np.arange(CHUNK)[:, None]
    lane = np.arange(GLA_VW)[None, :]
    j = lane % CHUNK
    tri = np.stack([i >= j, i <= j]).astype(np.float32)
    r = np.arange(GLA_VW)[:, None]
    bd_v = ((r // CHUNK) == (lane // HEAD_DIM)).astype(np.float32)
    rs = np.arange(GLA_QW)[:, None]
    st_mask = ((rs // GLA_DK) == (lane // HEAD_DIM)).astype(np.float32)
    kq_mask = ((r // CHUNK) == (np.arange(GLA_QW)[None, :] // GLA_DK)).astype(np.float32)
    return tri, bd_v, st_mask, kq_mask


def _gla_kernel(x_ref, sm_ref, wup_ref, bg_ref, tri_ref, bdv_ref, stm_ref, kqm_ref, o_ref, st_s):
    seq = x_ref.shape[0]
    nchunk = seq // CHUNK
    o_ref[...] = jnp.zeros_like(o_ref)
    st_s[...] = jnp.zeros_like(st_s)
    bdv_mask = bdv_ref[...] > 0.5
    st_mask = stm_ref[...] > 0.5
    kq_mask = kqm_ref[...] > 0.5

    def rows_of(chunk):
        return pl.ds(pl.multiple_of(chunk * CHUNK, CHUNK), CHUNK)

    def load_chunk(chunk):
        rs = rows_of(chunk)
        return (x_ref[rs, 0:GLA_QW], x_ref[rs, GLA_QW:2 * GLA_QW], x_ref[rs, 2 * GLA_QW:2 * GLA_QW + GLA_VW],
                sm_ref[rs, :])

    items = [(t, d) for t in range(GLA_WINDOW) for d in range(N_DIRS)]
    ni = len(items)

    def chunk_of(jw, t, d):
        i = jw * GLA_WINDOW + t
        return i if d == 0 else nchunk - 1 - i

    def step(jw, carry):
        raws = [load_chunk(chunk_of(jw, t, d)) for t, d in items]
        states = [st_s[d] for d in range(N_DIRS)]
        logits = [jnp.dot(_bf(raws[n][3]), wup_ref[d], preferred_element_type=F32) + bg_ref[d:d + 1, :]
                  for n, (t, d) in enumerate(items)]
        q_es, attns, dsts, cds = [], [], [], []
        for n, (t, d) in enumerate(items):
            q, k, v, _ = raws[n]
            gk = (jnp.minimum(logits[n], 0.0) - jnp.log1p(jnp.exp(-jnp.abs(logits[n])))) * (1.0 / GLA_NORMALIZER)
            b = _cumsum_rows(gk, reverse=(d == 1))
            b_end = b[CHUNK - 1:CHUNK] if d == 0 else b[0:1]
            q_e = q * (GLA_DK ** -0.5) * jnp.exp(b)
            keb = _bf(k * jnp.exp(-b))
            kebd = jnp.where(kq_mask, jnp.concatenate([keb] * N_HEADS, axis=0), jnp.zeros((GLA_VW, GLA_QW), BF16))
            q_es.append(_bf(q_e))
            attns.append(jnp.where(tri_ref[d] > 0.5, _mm_nt(q_es[n], kebd), 0.0))
            dsts.append(jnp.where(st_mask, _mm_tn(k * jnp.exp(b_end - b), v), 0.0))
            cd_col = jnp.transpose(jnp.broadcast_to(jnp.exp(b_end), (GLA_QW, GLA_QW)))
            cds.append(jnp.concatenate([cd_col, cd_col], axis=1))
        outs = [_mm(attns[n], _block_diag(raws[n][2], bdv_mask, N_HEADS)) for n in range(ni)]
        for n, (t, d) in enumerate(items):
            outs[n] = outs[n] + _mm(q_es[n], states[d])
            states[d] = cds[n] * states[d] + dsts[n]
        for n, (t, d) in enumerate(items):
            o_ref[rows_of(chunk_of(jw, t, d)), :] += outs[n]
        for d in range(N_DIRS):
            st_s[d] = states[d]
        return carry

    lax.fori_loop(0, nchunk // GLA_WINDOW, step, 0)


def _gla(gla_in, small, wup, bg, consts, batch, seq):
    T = gla_in.shape[0]
    tri, bd_v, st_mask, kq_mask = consts
    whole = lambda b: (b, 0)
    c2 = lambda b: (0, 0)
    c3 = lambda b: (0, 0, 0)
    return pl.pallas_call(
        _gla_kernel,
        grid=(batch,),
        in_specs=[
            pl.BlockSpec((seq, 512), whole),
            pl.BlockSpec((seq, LANES), whole),
            pl.BlockSpec(wup.shape, c3),
            pl.BlockSpec(bg.shape, c2),
            pl.BlockSpec(tri.shape, c3),
            pl.BlockSpec(bd_v.shape, c2),
            pl.BlockSpec(st_mask.shape, c2),
            pl.BlockSpec(kq_mask.shape, c2),
        ],
        out_specs=pl.BlockSpec((seq, 256), whole),
        out_shape=jax.ShapeDtypeStruct((T, 256), F32),
        scratch_shapes=[pltpu.VMEM((N_DIRS, GLA_QW, GLA_VW), F32)],
        compiler_params=pltpu.CompilerParams(
            dimension_semantics=("arbitrary",), vmem_limit_bytes=VMEM_LIMIT),
        name="gla",
    )(gla_in, small, wup, bg, tri, bd_v, st_mask, kq_mask)


def _outproj_kernel(ya_ref, ob_ref, oc_ref, ym_ref, gates_ref, gb_ref, gc_ref, ones_ref, w_ref, h_ref,
                    fg_ref, o_ref, *, final):
    ones256 = ones_ref[...]

    def head_norm(x, g):
        ms = _group_sum(x * x, ones256) * (1.0 / HEAD_DIM)
        return x * lax.rsqrt(ms + EPS) * g

    yb = head_norm(ob_ref[...], gb_ref[...]) * gates_ref[:, 0:256]
    yc = head_norm(oc_ref[...], gc_ref[...]) * gates_ref[:, 256:512]
    y = jnp.dot(_bf(ya_ref[...]), w_ref[0:256, :], preferred_element_type=F32)
    y = y + jnp.dot(_bf(yb), w_ref[256:512, :], preferred_element_type=F32)
    y = y + jnp.dot(_bf(yc), w_ref[512:768, :], preferred_element_type=F32)
    y = y + jnp.dot(_bf(ym_ref[...]), w_ref[768:1024, :], preferred_element_type=F32)
    hn = h_ref[...] + y
    if final:
        hn = hn * lax.rsqrt(jnp.mean(hn * hn, axis=-1, keepdims=True) + EPS) * fg_ref[...]
    o_ref[...] = hn


def _outproj(ya, ob, oc, ym, gates, gb, gc, ones256, w, h, fg, tm, final):
    T, D = h.shape
    row = lambda i: (i, 0)
    const = lambda i: (0, 0)
    return pl.pallas_call(
        functools.partial(_outproj_kernel, final=final),
        grid=(T // tm,),
        in_specs=[
            pl.BlockSpec((tm, 256), row), pl.BlockSpec((tm, 256), row),
            pl.BlockSpec((tm, 256), row), pl.BlockSpec((tm, 256), row),
            pl.BlockSpec((tm, 512), lambda i: (i, 1)),
            pl.BlockSpec((1, 256), const), pl.BlockSpec((1, 256), const),
            pl.BlockSpec((256, 256), const),
            pl.BlockSpec((D, D), const),
            pl.BlockSpec((tm, D), row),
            pl.BlockSpec((1, D), const),
        ],
        out_specs=pl.BlockSpec((tm, D), row),
        out_shape=jax.ShapeDtypeStruct((T, D), F32),
        compiler_params=pltpu.CompilerParams(
            dimension_semantics=("arbitrary",), vmem_limit_bytes=VMEM_LIMIT),
        name="outproj_final" if final else "outproj",
    )(ya, ob, oc, ym, gates, gb, gc, ones256, w, h, fg)


def _block_ones(n, group):
    i = np.arange(n)
    return jnp.asarray((i[:, None] // group) == (i[None, :] // group), dtype=BF16)


def _head_perm_cols():
    return np.concatenate([np.arange(h * HEAD_DIM, (h + 1) * HEAD_DIM) for h in ATT_HEAD_PERM])


def _prep_w_in(w_in):
    o = np.cumsum([0, 256, 128, 128, 256, 768, 8, 8, 256, 128, 128, 256, 32, 256, 256, 256])
    (a_q, a_k, a_v, a_g, d_qkv, d_beta, d_alpha, d_gate, l_q, l_k, l_v, l_low, l_gate, m_q, m_gate) = [
        w_in[:, :, o[i]:o[i + 1]] for i in range(15)]
    perm = _head_perm_cols()
    pad = jnp.zeros(w_in.shape[:2] + (N_COLS - C_SM - 48,), w_in.dtype)
    cols = [a_q[:, :, perm], a_k, a_v, a_g[:, :, perm], d_qkv, d_gate, l_q, l_k, l_v, l_gate, m_q, m_gate,
            d_beta, d_alpha, l_low, pad]
    return jnp.concatenate(cols, axis=-1).astype(BF16)


def _rope_tables(seq):
    rows = seq // GRID_W
    row_pos = jnp.repeat(jnp.arange(rows, dtype=F32), GRID_W)
    col_pos = jnp.tile(jnp.arange(GRID_W, dtype=F32), rows)
    half = HEAD_DIM // 4
    inv_freq = ROPE_THETA ** (-jnp.arange(0, 2 * half, 2, dtype=F32) / (2 * half))
    ang = jnp.stack([row_pos, col_pos], axis=-1)[..., None] * inv_freq
    cos, sin = jnp.cos(ang), jnp.sin(ang)
    cos_h = jnp.concatenate([cos, cos], axis=-1).reshape(seq, HEAD_DIM)
    sin_h = jnp.concatenate([-sin, sin], axis=-1).reshape(seq, HEAD_DIM)
    return jnp.tile(cos_h, (1, 2)), jnp.tile(sin_h, (1, 2))


def _trunk(x, mem, p, consts):
    batch, seq, d_model = x.shape
    T = batch * seq
    depth = p["w_in"].shape[0]
    tm = 512 if seq % 512 == 0 else seq
    tq = 256 if seq % 256 == 0 else seq
    kvc = 512 if seq % 512 == 0 else seq
    cos_t, sin_t = _rope_tables(seq)
    mem2d = mem.reshape(batch * mem.shape[1], d_model)
    memkv = _memkv(mem2d, p["mem_norm_g"], p["w_mem_kv"], min(512, mem2d.shape[0]))
    h = x.reshape(T, d_model)
    for d in range(depth):
        qkv, gates, gdn_in, gla_in, mq, small = _inproj(
            h, p["norm_g"][d], p["w_in"][d], cos_t, sin_t, p["qkg"][d], consts["ones384"], seq, tm)
        ya, ym = _attention(qkv, mq, memkv[d], gates, batch, seq, tq, kvc)
        ob = _gdn(gdn_in, small, p["conv_w"][d], p["gdn_par"][d], consts["gdn"], batch, seq)
        oc = _gla(gla_in, small, p["wup"][d], p["bg"][d], consts["gla"], batch, seq)
        h = _outproj(ya, ob, oc, ym, gates, p["gb"][d], p["gc"][d],
                     consts["ones256"], p["w_out"][d], h, p["final_g"], tm, final=(d == depth - 1))
    return h.reshape(batch, seq, d_model)


def kernel(x_prompt, x_sample, mem_prompt, mem_sample, norm_g, w_in, att_q_norm_g, att_k_norm_g, gdn_conv_w,
           gdn_a_log, gdn_dt_bias, gdn_out_norm_g, gla_w_gate_up, gla_b_gate, gla_out_norm_g, mem_norm_g,
           w_mem_kv, w_out, final_norm_g):
    depth, d_model = norm_g.shape
    perm = _head_perm_cols()
    w_out_p = jnp.concatenate([w_out[:, :256][:, perm], w_out[:, 256:]], axis=1).astype(BF16)
    qkg = jnp.concatenate([jnp.tile(att_q_norm_g, (1, N_HEADS)), jnp.tile(att_k_norm_g, (1, KV_HEADS))], axis=1)
    zeros8 = jnp.zeros((depth, N_COMBO), F32)
    par_a = jnp.concatenate([zeros8, -jnp.exp(gdn_a_log.reshape(depth, N_COMBO)),
                             jnp.zeros((depth, LANES - 2 * N_COMBO), F32)], axis=1)
    par_b = jnp.concatenate([zeros8, gdn_dt_bias.reshape(depth, N_COMBO),
                             jnp.zeros((depth, LANES - 2 * N_COMBO), F32)], axis=1)
    gdn_par = jnp.concatenate([par_a[:, None], par_b[:, None], jnp.zeros((depth, 6, LANES), F32)], axis=1)
    conv_w = jnp.concatenate([gdn_conv_w, jnp.zeros((depth, 5, gdn_conv_w.shape[2]), F32)], axis=1)
    wup = jnp.zeros((depth, N_DIRS, LANES, GLA_QW), F32)
    for d in range(N_DIRS):
        lo = 2 * N_COMBO + d * GLA_RANK
        wup = wup.at[:, d, lo:lo + GLA_RANK, :].set(gla_w_gate_up[:, d])
    p = {
        "norm_g": norm_g.reshape(depth, 1, d_model),
        "w_in": _prep_w_in(w_in),
        "qkg": qkg.reshape(depth, 1, 384),
        "conv_w": conv_w,
        "gdn_par": gdn_par,
        "wup": wup.astype(BF16),
        "bg": jnp.concatenate([gla_b_gate, jnp.zeros((depth, 6, GLA_QW), F32)], axis=1),
        "gb": jnp.tile(gdn_out_norm_g, (1, N_HEADS)).reshape(depth, 1, 256),
        "gc": jnp.tile(gla_out_norm_g, (1, N_HEADS)).reshape(depth, 1, 256),
        "mem_norm_g": mem_norm_g.reshape(depth, 1, d_model),
        "w_mem_kv": w_mem_kv.astype(BF16),
        "w_out": w_out_p,
        "final_g": final_norm_g.reshape(1, d_model),
    }
    tri_c, bd, expand, cm = _gdn_constants()
    consts = {
        "ones384": _block_ones(384, HEAD_DIM),
        "ones256": _block_ones(256, HEAD_DIM),
        "gdn": (jnp.asarray(tri_c), jnp.asarray(bd), jnp.asarray(expand, dtype=BF16), jnp.asarray(cm),
                _block_ones(256, HEAD_DIM)),
        "gla": tuple(jnp.asarray(a) for a in _gla_constants()),
    }
    y_prompt = _trunk(x_prompt, mem_prompt, p, consts)
    y_sample = _trunk(x_sample, mem_sample, p, consts)
    return (y_prompt, y_sample)
```

```python
import functools

import numpy as np
import jax
import jax.numpy as jnp
from jax import lax
from jax.experimental import pallas as pl
from jax.experimental.pallas import tpu as pltpu

F32 = jnp.float32
BF16 = jnp.bfloat16

N_HEADS = 4
HEAD_DIM = 64
KV_HEADS = 2
GRID_W = 64
ROPE_THETA = 10000.0
CHUNK = 64
GLA_DK = 32
GLA_RANK = 16
GLA_NORMALIZER = 16.0
EPS = 1e-6
N_DIRS = 2
LANES = 128
GDN_GROUP = 2
GDN_WINDOW = 4
VMEM_LIMIT = 56 * 1024 * 1024

C_AQ, C_AK, C_AV, C_AG = 0, 256, 384, 512
C_DQKV, C_DG = 768, 1536
C_LQ, C_LK, C_LV, C_LG = 1792, 1920, 2048, 2304
C_MQ, C_MG = 2560, 2816
C_SM = 3072
N_COLS = 3200
ATT_HEAD_PERM = (0, 2, 1, 3)


def _bf(x):
    return x.astype(BF16)


def _mm(a, b):
    return jnp.dot(_bf(a), _bf(b), preferred_element_type=F32)


def _mm_nt(a, b):
    return lax.dot_general(_bf(a), _bf(b), (((1,), (1,)), ((), ())), preferred_element_type=F32)


def _mm_tn(a, b):
    return lax.dot_general(_bf(a), _bf(b), (((0,), (0,)), ((), ())), preferred_element_type=F32)


def _split2(x):
    hi = _bf(x)
    lo = _bf(x - hi.astype(F32))
    return hi, lo


def _split3(x):
    hi = _bf(x).astype(F32)
    r = x - hi
    mid = _bf(r).astype(F32)
    lo = _bf(r - mid).astype(F32)
    return hi, mid, lo


def _group_sum(x, ones_blocks):
    hi, lo = _split2(x)
    return (jnp.dot(hi, ones_blocks, preferred_element_type=F32)
            + jnp.dot(lo, ones_blocks, preferred_element_type=F32))


def _silu(x):
    return x * (0.5 * jnp.tanh(0.5 * x) + 0.5)


def _softplus(x):
    return jnp.maximum(x, 0.0) + jnp.log1p(jnp.exp(-jnp.abs(x)))


def _cumsum_rows(x, reverse):
    n = x.shape[0]
    row = lax.broadcasted_iota(jnp.int32, x.shape, 0)
    s = 1
    while s < n:
        if reverse:
            x = x + jnp.where(row < n - s, pltpu.roll(x, n - s, axis=0), 0.0)
        else:
            x = x + jnp.where(row >= s, pltpu.roll(x, s, axis=0), 0.0)
        s *= 2
    return x


def _block_diag(x, mask, reps):
    xb = _bf(x)
    t = jnp.concatenate([xb] * reps, axis=0)
    return jnp.where(mask, t, jnp.zeros_like(t))


def _inproj_kernel(h_ref, g_ref, w_ref, cos_ref, sin_ref, qkg_ref, ones_ref,
                   qkv_ref, gates_ref, gdn_ref, gla_ref, mq_ref, small_ref):
    x = h_ref[...]
    xn = x * lax.rsqrt(jnp.mean(x * x, axis=-1, keepdims=True) + EPS) * g_ref[...]
    proj = jnp.dot(_bf(xn), w_ref[...], preferred_element_type=F32)

    qk = proj[:, C_AQ:C_AV]
    ms = _group_sum(qk * qk, ones_ref[...]) * (1.0 / HEAD_DIM)
    qkn = qk * lax.rsqrt(ms + EPS) * qkg_ref[...]
    cos = cos_ref[...]
    sin = sin_ref[...]
    lane = lax.broadcasted_iota(jnp.int32, (1, LANES), 1)
    first_half = (lane % 32) < 16
    outs = []
    for c in range(3):
        xc = qkn[:, c * LANES:(c + 1) * LANES]
        partner = jnp.where(first_half, pltpu.roll(xc, LANES - 16, axis=1), pltpu.roll(xc, 16, axis=1))
        outs.append(xc * cos + partner * sin)
    qkv_ref[:, 0:128] = _bf(outs[0] * (HEAD_DIM ** -0.5))
    qkv_ref[:, 128:256] = _bf(outs[1] * (HEAD_DIM ** -0.5))
    qkv_ref[:, 256:384] = _bf(outs[2])
    qkv_ref[:, 384:512] = _bf(proj[:, C_AV:C_AG])

    gates_ref[:, 0:256] = _bf(_silu(proj[:, C_AG:C_AG + 256]))
    gates_ref[:, 256:512] = _bf(_silu(proj[:, C_MG:C_MG + 256]))
    gates_ref[:, 512:768] = _bf(_silu(proj[:, C_DG:C_DG + 256]))
    gates_ref[:, 768:1024] = _bf(_silu(proj[:, C_LG:C_LG + 256]))
    gdn_ref[...] = _bf(proj[:, C_DQKV:C_DG])
    gla_ref[...] = _bf(proj[:, C_LQ:C_LG])
    mq_ref[...] = _bf(proj[:, C_MQ:C_MG] * (HEAD_DIM ** -0.5))
    small_ref[...] = proj[:, C_SM:N_COLS]


def _inproj(h, g, w, cos_t, sin_t, qkg, ones384, seq, tm):
    T, D = h.shape
    nseq = seq // tm
    row = lambda i: (i, 0)
    const = lambda i: (0, 0)
    return pl.pallas_call(
        _inproj_kernel,
        grid=(T // tm,),
        in_specs=[
            pl.BlockSpec((tm, D), row),
            pl.BlockSpec((1, D), const),
            pl.BlockSpec((D, N_COLS), const),
            pl.BlockSpec((tm, LANES), lambda i: (i % nseq, 0)),
            pl.BlockSpec((tm, LANES), lambda i: (i % nseq, 0)),
            pl.BlockSpec((1, 384), const),
            pl.BlockSpec((384, 384), const),
        ],
        out_specs=[
            pl.BlockSpec((tm, 512), row),
            pl.BlockSpec((tm, 1024), row),
            pl.BlockSpec((tm, 768), row),
            pl.BlockSpec((tm, 512), row),
            pl.BlockSpec((tm, 256), row),
            pl.BlockSpec((tm, LANES), row),
        ],
        out_shape=[
            jax.ShapeDtypeStruct((T, 512), BF16),
            jax.ShapeDtypeStruct((T, 1024), BF16),
            jax.ShapeDtypeStruct((T, 768), BF16),
            jax.ShapeDtypeStruct((T, 512), BF16),
            jax.ShapeDtypeStruct((T, 256), BF16),
            jax.ShapeDtypeStruct((T, LANES), F32),
        ],
        compiler_params=pltpu.CompilerParams(
            dimension_semantics=("arbitrary",), vmem_limit_bytes=VMEM_LIMIT),
        name="inproj",
    )(h, g, w, cos_t, sin_t, qkg, ones384)


def _memkv_kernel(m_ref, g_ref, w_ref, o_ref):
    x = m_ref[...]
    xn = x * lax.rsqrt(jnp.mean(x * x, axis=-1, keepdims=True) + EPS) * g_ref[0]
    o_ref[0] = _bf(jnp.dot(_bf(xn), w_ref[0], preferred_element_type=F32))


def _memkv(mem2d, g, w, tr):
    R, D = mem2d.shape
    depth = w.shape[0]
    return pl.pallas_call(
        _memkv_kernel,
        grid=(depth, R // tr),
        in_specs=[
            pl.BlockSpec((tr, D), lambda d, i: (i, 0)),
            pl.BlockSpec((1, 1, D), lambda d, i: (d, 0, 0)),
            pl.BlockSpec((1, D, 512), lambda d, i: (d, 0, 0)),
        ],
        out_specs=pl.BlockSpec((1, tr, 512), lambda d, i: (d, i, 0)),
        out_shape=jax.ShapeDtypeStruct((depth, R, 512), BF16),
        compiler_params=pltpu.CompilerParams(
            dimension_semantics=("arbitrary", "arbitrary"), vmem_limit_bytes=VMEM_LIMIT),
        name="memkv",
    )(mem2d, g, w)


def _attn_kernel(q_ref, k_ref, v_ref, mq_ref, km_ref, vm_ref, ga_ref, gm_ref, ya_ref, ym_ref, vt_s, *, kvc):
    tq = q_ref.shape[0]
    seq = k_ref.shape[0]
    lane = lax.broadcasted_iota(jnp.int32, (1, LANES), 1)
    lo = lane < HEAD_DIM

    def stack_heads(x):
        z = jnp.zeros_like(x)
        return [jnp.where(lo, x, z), jnp.where(lo, z, x)]

    @pl.when(pl.program_id(1) == 0)
    def _():
        sub = lax.broadcasted_iota(jnp.int32, (LANES, 1), 0) < HEAD_DIM
        for c in range(seq // kvc):
            vt = jnp.transpose(v_ref[c * kvc:(c + 1) * kvc, :].astype(F32))
            vt_s[0, c] = _bf(jnp.where(sub, vt, 1.0))
            vt_s[1, c] = _bf(jnp.where(sub, 1.0, vt))

    q = q_ref[...]
    qm = stack_heads(q[:, :LANES]) + stack_heads(q[:, LANES:])
    n_heads = len(qm)

    def scores(c):
        kc = k_ref[c * kvc:(c + 1) * kvc, :]
        return [lax.dot_general(kc, qm[h], (((1,), (1,)), ((), ())), preferred_element_type=F32)
                for h in range(n_heads)]

    ms = [jnp.full((1, tq), -jnp.inf, F32) for _ in range(n_heads)]
    accs = [jnp.zeros((LANES, tq), F32) for _ in range(n_heads)]
    nsub = seq // kvc
    sts = scores(0)
    for c in range(nsub):
        nxt = scores(c + 1) if c + 1 < nsub else None
        new_ms = [jnp.maximum(ms[h], jnp.max(sts[h], axis=0, keepdims=True)) for h in range(n_heads)]
        ps = [_bf(jnp.exp(sts[h] - new_ms[h])) for h in range(n_heads)]
        accs = [jnp.exp(ms[h] - new_ms[h]) * accs[h]
                + jnp.dot(vt_s[h % 2, c], ps[h], preferred_element_type=F32) for h in range(n_heads)]
        ms, sts = new_ms, nxt
    o_t = []
    for h in range(n_heads):
        if h % 2 == 0:
            o_t.append(accs[h][0:HEAD_DIM] / accs[h][HEAD_DIM:HEAD_DIM + 1])
        else:
            o_t.append(accs[h][HEAD_DIM:LANES] / accs[h][0:1])
    ya = jnp.transpose(jnp.concatenate(o_t, axis=0))
    ya_ref[...] = _bf(ya * ga_ref[...].astype(F32))

    mq = mq_ref[...]
    outs = []
    for c in range(2):
        qc = jnp.concatenate(stack_heads(mq[:, c * LANES:(c + 1) * LANES]), axis=0)
        kc = km_ref[:, c * LANES:(c + 1) * LANES]
        vc = vm_ref[:, c * LANES:(c + 1) * LANES]
        s = lax.dot_general(qc, kc, (((1,), (1,)), ((), ())), preferred_element_type=F32)
        p = jnp.exp(s - jnp.max(s, axis=-1, keepdims=True))
        om = jnp.dot(_bf(p), vc, preferred_element_type=F32) / jnp.sum(p, axis=-1, keepdims=True)
        outs.append(jnp.where(lo, om[0:tq], om[tq:2 * tq]))
    ym_ref[...] = _bf(jnp.concatenate(outs, axis=1) * gm_ref[...].astype(F32))


def _attention(qkv, mq, memkv, gates, batch, seq, tq, kvc):
    T = qkv.shape[0]
    nq = seq // tq
    mem_len = memkv.shape[0] // batch
    qrow = lambda b, i: (b * nq + i, 0)
    return pl.pallas_call(
        functools.partial(_attn_kernel, kvc=kvc),
        grid=(batch, nq),
        in_specs=[
            pl.BlockSpec((tq, 256), qrow),
            pl.BlockSpec((seq, LANES), lambda b, i: (b, 2)),
            pl.BlockSpec((seq, LANES), lambda b, i: (b, 3)),
            pl.BlockSpec((tq, 256), qrow),
            pl.BlockSpec((mem_len, 256), lambda b, i: (b, 0)),
            pl.BlockSpec((mem_len, 256), lambda b, i: (b, 1)),
            pl.BlockSpec((tq, 256), lambda b, i: (b * nq + i, 0)),
            pl.BlockSpec((tq, 256), lambda b, i: (b * nq + i, 1)),
        ],
        out_specs=[pl.BlockSpec((tq, 256), qrow), pl.BlockSpec((tq, 256), qrow)],
        out_shape=[jax.ShapeDtypeStruct((T, 256), BF16), jax.ShapeDtypeStruct((T, 256), BF16)],
        scratch_shapes=[pltpu.VMEM((KV_HEADS, seq // kvc, LANES, kvc), BF16)],
        compiler_params=pltpu.CompilerParams(
            dimension_semantics=("arbitrary", "arbitrary"), vmem_limit_bytes=VMEM_LIMIT),
        name="attention",
    )(qkv, qkv, qkv, mq, memkv, memkv, gates, gates)


GDN_W = GDN_GROUP * HEAD_DIM
GDN_NGRP = N_HEADS // GDN_GROUP
N_COMBO = N_DIRS * N_HEADS
N_QTY = 4


def _gdn_constants():
    w = GDN_W
    i = np.arange(CHUNK)[:, None]
    lane = np.arange(w)[None, :]
    j = lane % CHUNK
    tri = np.stack([i >= j, i <= j]).astype(np.float32)
    stri = np.stack([i > j, i < j]).astype(np.float32)
    eye = (i == j).astype(np.float32)[None]
    tri_c = np.concatenate([tri, stri, eye], axis=0)
    r = np.arange(w)[:, None]
    bd = ((r // CHUNK) == (lane // CHUNK)).astype(np.float32)
    expand = np.zeros((N_DIRS * GDN_NGRP, LANES, N_QTY * w), np.float32)
    for d in range(N_DIRS):
        for g in range(GDN_NGRP):
            for qty in range(N_QTY):
                for hh in range(GDN_GROUP):
                    combo = d * N_HEADS + g * GDN_GROUP + hh
                    for piece in range(3):
                        src = piece * (N_QTY * N_COMBO) + qty * N_COMBO + combo
                        expand[d * GDN_NGRP + g, src, qty * w + hh * CHUNK: qty * w + (hh + 1) * CHUNK] = 1.0
    cm = np.zeros((N_COMBO, LANES), np.float32)
    for c in range(N_COMBO):
        for t in range(6):
            cm[c, t * N_COMBO + c] = 1.0
    return tri_c, bd, expand, cm


def _gdn_kernel(x_ref, sm_ref, cw_ref, par_ref, ones_ref, tri_ref, bd_ref, exp_ref, cm_ref,
                o_ref, qkv_s, sm3_s, xs_s, ys_s, st_s, pa_s, pb_s, cd_s, acc_s):
    seq = x_ref.shape[0]
    nchunk = seq // CHUNK
    w = GDN_W
    lane = lax.broadcasted_iota(jnp.int32, (1, LANES), 1)
    row = lax.broadcasted_iota(jnp.int32, (CHUNK, 1), 0)
    lane8 = lane < N_COMBO
    fwd_lane = lane < N_HEADS
    ones256 = ones_ref[...]

    def prep(c, carry):
        r0 = pl.multiple_of(c * CHUNK, CHUNK)
        xc = x_ref[pl.ds(r0, CHUNK), :].astype(F32)
        pstart = pl.multiple_of(jnp.maximum(r0 - 16, 0), 16)
        nstart = pl.multiple_of(jnp.minimum(r0 + CHUNK, seq - 16), 16)
        prev_row = x_ref[pl.ds(pstart, 16), :].astype(F32)[15:16] * jnp.where(c > 0, 1.0, 0.0)
        next_row = x_ref[pl.ds(nstart, 16), :].astype(F32)[0:1] * jnp.where(c < nchunk - 1, 1.0, 0.0)
        x_prev = jnp.where(row == 0, prev_row, pltpu.roll(xc, 1, axis=0))
        x_next = jnp.where(row == CHUNK - 1, next_row, pltpu.roll(xc, CHUNK - 1, axis=0))
        y = x_prev * cw_ref[0:1, :] + xc * cw_ref[1:2, :]
        y = y + x_next * cw_ref[2:3, :]
        y = _silu(y)
        for part in range(2):
            z = y[:, part * 256:(part + 1) * 256]
            ss = _group_sum(z * z, ones256)
            zn = z * lax.rsqrt(ss + EPS)
            if part == 0:
                zn = zn * (HEAD_DIM ** -0.5)
            qkv_s[pl.ds(r0, CHUNK), part * 256:(part + 1) * 256] = _bf(zn)
        qkv_s[pl.ds(r0, CHUNK), 512:768] = _bf(y[:, 512:768])

        sm = sm_ref[pl.ds(r0, CHUNK), :]
        beta = jax.nn.sigmoid(sm)
        g_all = par_ref[0:1, :] * _softplus(sm + par_ref[1:2, :])
        g = pltpu.roll(g_all, LANES - N_COMBO, axis=1)
        pre = _cumsum_rows(g, reverse=False)
        suf = _cumsum_rows(g, reverse=True)
        b = jnp.where(fwd_lane, pre, suf)
        b_end = jnp.where(fwd_lane, pre[CHUNK - 1:CHUNK], suf[0:1])
        eb = jnp.exp(b)
        zero = jnp.zeros_like(b)
        q4 = (jnp.where(lane8, beta, zero)
              + pltpu.roll(jnp.where(lane8, beta * eb, zero), N_COMBO, axis=1)
              + pltpu.roll(jnp.where(lane8, eb, zero), 2 * N_COMBO, axis=1)
              + pltpu.roll(jnp.where(lane8, jnp.exp(b_end - b), zero), 3 * N_COMBO, axis=1))
        hi, mid, lo3 = _split3(q4)
        nq = N_QTY * N_COMBO
        sm3_s[pl.ds(r0, CHUNK), :] = _bf(hi + pltpu.roll(mid, nq, axis=1) + pltpu.roll(lo3, 2 * nq, axis=1))
        bh, bm, bl = _split3(jnp.where(lane8, b, zero))
        pb = bh + pltpu.roll(bm, N_COMBO, axis=1) + pltpu.roll(bl, 2 * N_COMBO, axis=1)
        ones_lo = (lane < 3 * N_COMBO).astype(F32)
        ones_hi = jnp.logical_and(lane >= 3 * N_COMBO, lane < 6 * N_COMBO).astype(F32)
        xs_s[pl.ds(r0, CHUNK), :] = _bf(pb + ones_hi)
        ys_s[pl.ds(r0, CHUNK), :] = _bf(ones_lo - pltpu.roll(pb, 3 * N_COMBO, axis=1))
        return carry

    lax.fori_loop(0, nchunk, prep, 0)

    acc_s[...] = jnp.zeros_like(acc_s)
    st_s[...] = jnp.zeros_like(st_s)
    bd_mask = bd_ref[...] > 0.5

    combos = [(d, grp) for d in range(N_DIRS) for grp in range(GDN_NGRP)]

    def load_chunk(chunk, d, grp):
        rs = pl.ds(pl.multiple_of(chunk * CHUNK, CHUNK), CHUNK)
        return (qkv_s[rs, grp * w:(grp + 1) * w], qkv_s[rs, 256 + grp * w:256 + (grp + 1) * w],
                qkv_s[rs, 512 + grp * w:512 + (grp + 1) * w], sm3_s[rs, :], xs_s[rs, :], ys_s[rs, :])

    nc = len(combos)
    head_masks = [(lax.broadcasted_iota(jnp.int32, (1, w), 1) // HEAD_DIM) == hh for hh in range(GDN_GROUP)]

    items = [(d, grp) for _ in range(GDN_WINDOW) for d, grp in combos]
    ni = len(items)

    def transform_stages(raws, outs):
        exs, diffs, kqs = [], [], []
        a_mats, ps, rs_ = [], [], []

        def stage_products():
            for n, (d, grp) in enumerate(items):
                q, k, vb, sm3, xs, ys = raws[n]
                ex = jnp.dot(sm3, exp_ref[d * GDN_NGRP + grp], preferred_element_type=F32)
                exs.append([ex[:, t * w:(t + 1) * w] for t in range(N_QTY)])
                ystack = jnp.concatenate(
                    [ys * _bf(cm_ref[d * N_HEADS + grp * GDN_GROUP + hh:d * N_HEADS + grp * GDN_GROUP + hh + 1, :])
                     for hh in range(GDN_GROUP)], axis=0)
                diffs.append(lax.dot_general(xs, ystack, (((1,), (1,)), ((), ())), preferred_element_type=F32))
                kbd = jnp.concatenate([jnp.where(hm, k, jnp.zeros_like(k)) for hm in head_masks], axis=0)
                kqs.append(lax.dot_general(jnp.concatenate([k, q], axis=0), kbd, (((1,), (1,)), ((), ())),
                                           preferred_element_type=F32))

        def stage_square():
            for n, (d, grp) in enumerate(items):
                tri = tri_ref[d] > 0.5
                stri = tri_ref[2 + d] > 0.5
                decay = jnp.where(tri, jnp.exp(jnp.where(tri, diffs[n], 0.0)), 0.0)
                low = jnp.where(stri, exs[n][0] * kqs[n][:CHUNK] * decay, 0.0)
                a_mats.append(kqs[n][CHUNK:] * decay)
                ps.append(_mm(low, _block_diag(low, bd_mask, GDN_GROUP)))
                rs_.append(tri_ref[4] - low)

        def stage_level(last):
            for n in range(ni):
                pbd = _block_diag(ps[n], bd_mask, GDN_GROUP)
                if last:
                    rs_[n] = rs_[n] + _mm(rs_[n], pbd)
                else:
                    rp = _mm(jnp.concatenate([rs_[n], ps[n]], axis=0), pbd)
                    rs_[n] = rs_[n] + rp[:CHUNK]
                    ps[n] = rp[CHUNK:]

        def stage_apply():
            for n, (d, grp) in enumerate(items):
                q, k, vb = raws[n][:3]
                betax, bebx, ebx, edx = exs[n]
                kf = k.astype(F32)
                u = _mm(rs_[n], _block_diag(betax * vb.astype(F32), bd_mask, GDN_GROUP))
                wm = _mm(rs_[n], _block_diag(bebx * kf, bd_mask, GDN_GROUP))
                cd_row = ebx[CHUNK - 1:CHUNK] if d == 0 else ebx[0:1]
                outs.append((a_mats[n], u, _bf(wm), _bf(q.astype(F32) * ebx), _bf(kf * edx), cd_row))

        return ([stage_products, stage_square] + [functools.partial(stage_level, False)] * 4
                + [functools.partial(stage_level, True), stage_apply])

    def store_transformed(c, t):
        a_mat, u, wm, q_dec, k_dec, cd_row = t
        pa_s[2 * c] = a_mat
        pa_s[2 * c + 1] = u
        pb_s[3 * c] = wm
        pb_s[3 * c + 1] = q_dec
        pb_s[3 * c + 2] = k_dec
        cd_s[c] = jnp.broadcast_to(cd_row, (8, w))

    def load_transformed(c):
        return pa_s[2 * c], pa_s[2 * c + 1], pb_s[3 * c], pb_s[3 * c + 1], pb_s[3 * c + 2], cd_s[c][0:1]

    def scan_stages(states, cur, os_):
        tmp = {}

        def stage_ws(t):
            tmp["ws"] = [_mm(jnp.concatenate([cur[t * nc + c][2], cur[t * nc + c][3]], axis=0), states[c])
                         for c in range(nc)]

        def stage_update(t):
            for c in range(nc):
                a_mat, u, _, _, k_dec, cd_row = cur[t * nc + c]
                v_new = u - tmp["ws"][c][:CHUNK]
                os_.append(tmp["ws"][c][CHUNK:] + _mm(a_mat, _block_diag(v_new, bd_mask, GDN_GROUP)))
                states[c] = cd_row * states[c] + jnp.where(bd_mask, _mm_tn(k_dec, v_new), 0.0)

        stages = []
        for t in range(GDN_WINDOW):
            stages += [functools.partial(stage_ws, t), functools.partial(stage_update, t)]
        return stages

    def chunk_of(j, t, d):
        i = j * GDN_WINDOW + t
        return i if d == 0 else nchunk - 1 - i

    def load_window(j):
        return [load_chunk(chunk_of(j, t, d), d, grp) for t in range(GDN_WINDOW) for d, grp in combos]

    first = []
    for stage in transform_stages(load_window(0), first):
        stage()
    for n in range(ni):
        store_transformed(n, first[n])

    nwin = nchunk // GDN_WINDOW

    def step(j, carry):
        cur = [load_transformed(n) for n in range(ni)]
        states = [st_s[c] for c in range(nc)]
        raws = load_window(jnp.minimum(j + 1, nwin - 1))
        os_, nexts = [], []
        t_stages = transform_stages(raws, nexts)
        s_stages = scan_stages(states, cur, os_)
        for k in range(max(len(t_stages), len(s_stages))):
            if k < len(t_stages):
                t_stages[k]()
            if k < len(s_stages):
                s_stages[k]()
        for t in range(GDN_WINDOW):
            for c, (d, grp) in enumerate(combos):
                rs = pl.ds(pl.multiple_of(chunk_of(j, t, d) * CHUNK, CHUNK), CHUNK)
                acc_s[rs, grp * w:(grp + 1) * w] += os_[t * nc + c]
        for c in range(nc):
            st_s[c] = states[c]
        for n in range(ni):
            store_transformed(n, nexts[n])
        return carry

    lax.fori_loop(0, nwin, step, 0)
    o_ref[...] = _bf(acc_s[...])


def _gdn(gdn_in, small, conv_w, par, consts, batch, seq):
    T = gdn_in.shape[0]
    assert seq % (CHUNK * GDN_WINDOW) == 0
    tri_c, bd, expand, cm, ones256 = consts
    whole = lambda b: (b, 0)
    c2 = lambda b: (0, 0)
    c3 = lambda b: (0, 0, 0)
    return pl.pallas_call(
        _gdn_kernel,
        grid=(batch,),
        in_specs=[
            pl.BlockSpec((seq, 768), whole),
            pl.BlockSpec((seq, LANES), whole),
            pl.BlockSpec((8, 768), c2),
            pl.BlockSpec((8, LANES), c2),
            pl.BlockSpec((256, 256), c2),
            pl.BlockSpec(tri_c.shape, c3),
            pl.BlockSpec(bd.shape, c2),
            pl.BlockSpec(expand.shape, c3),
            pl.BlockSpec(cm.shape, c2),
        ],
        out_specs=pl.BlockSpec((seq, 256), whole),
        out_shape=jax.ShapeDtypeStruct((T, 256), BF16),
        scratch_shapes=[
            pltpu.VMEM((seq, 768), BF16),
            pltpu.VMEM((seq, LANES), BF16),
            pltpu.VMEM((seq, LANES), BF16),
            pltpu.VMEM((seq, LANES), BF16),
            pltpu.VMEM((N_DIRS * GDN_NGRP, GDN_W, GDN_W), F32),
            pltpu.VMEM((2 * GDN_WINDOW * N_DIRS * GDN_NGRP, CHUNK, GDN_W), F32),
            pltpu.VMEM((3 * GDN_WINDOW * N_DIRS * GDN_NGRP, CHUNK, GDN_W), BF16),
            pltpu.VMEM((GDN_WINDOW * N_DIRS * GDN_NGRP, 8, GDN_W), F32),
            pltpu.VMEM((seq, 256), F32),
        ],
        compiler_params=pltpu.CompilerParams(
            dimension_semantics=("arbitrary",), vmem_limit_bytes=VMEM_LIMIT),
        name="gdn",
    )(gdn_in, small, conv_w, par, ones256, tri_c, bd, expand, cm)


GLA_QW = N_HEADS * GLA_DK
GLA_VW = N_HEADS * HEAD_DIM
GLA_WINDOW = 4


def _gla_constants():
    i = np.arange(CHUNK)[:, None]
    lane = np.arange(GLA_VW)[None, :]
    j = lane % CHUNK
    tri = np.stack([i >= j, i <= j]).astype(np.float32)
    r = np.arange(GLA_VW)[:, None]
    bd_v = ((r // CHUNK) == (lane // HEAD_DIM)).astype(np.float32)
    rs = np.arange(GLA_QW)[:, None]
    st_mask = ((rs // GLA_DK) == (lane // HEAD_DIM)).astype(np.float32)
    kq_mask = ((r // CHUNK) == (np.arange(GLA_QW)[None, :] // GLA_DK)).astype(np.float32)
    return tri, bd_v, st_mask, kq_mask


def _gla_kernel(x_ref, sm_ref, wup_ref, bg_ref, tri_ref, bdv_ref, stm_ref, kqm_ref, o_ref, st_s, acc_s):
    seq = x_ref.shape[0]
    nchunk = seq // CHUNK
    acc_s[...] = jnp.zeros_like(acc_s)
    st_s[...] = jnp.zeros_like(st_s)
    bdv_mask = bdv_ref[...] > 0.5
    st_mask = stm_ref[...] > 0.5
    kq_mask = kqm_ref[...] > 0.5

    def rows_of(chunk):
        return pl.ds(pl.multiple_of(chunk * CHUNK, CHUNK), CHUNK)

    def load_chunk(chunk):
        rs = rows_of(chunk)
        return (x_ref[rs, 0:GLA_QW].astype(F32), x_ref[rs, GLA_QW:2 * GLA_QW].astype(F32),
                x_ref[rs, 2 * GLA_QW:2 * GLA_QW + GLA_VW], sm_ref[rs, :])

    items = [(t, d) for t in range(GLA_WINDOW) for d in range(N_DIRS)]
    ni = len(items)

    def chunk_of(jw, t, d):
        i = jw * GLA_WINDOW + t
        return i if d == 0 else nchunk - 1 - i

    def step(jw, carry):
        raws = [load_chunk(chunk_of(jw, t, d)) for t, d in items]
        states = [st_s[d] for d in range(N_DIRS)]
        logits = [jnp.dot(_bf(raws[n][3]), wup_ref[d], preferred_element_type=F32) + bg_ref[d:d + 1, :]
                  for n, (t, d) in enumerate(items)]
        q_es, attns, dsts, cds = [], [], [], []
        for n, (t, d) in enumerate(items):
            q, k, v, _ = raws[n]
            gk = (jnp.minimum(logits[n], 0.0) - jnp.log1p(jnp.exp(-jnp.abs(logits[n])))) * (1.0 / GLA_NORMALIZER)
            b = _cumsum_rows(gk, reverse=(d == 1))
            b_end = b[CHUNK - 1:CHUNK] if d == 0 else b[0:1]
            q_e = q * (GLA_DK ** -0.5) * jnp.exp(b)
            keb = _bf(k * jnp.exp(-b))
            kebd = jnp.where(kq_mask, jnp.concatenate([keb] * N_HEADS, axis=0), jnp.zeros((GLA_VW, GLA_QW), BF16))
            q_es.append(_bf(q_e))
            attns.append(jnp.where(tri_ref[d] > 0.5, _mm_nt(q_es[n], kebd), 0.0))
            dsts.append(jnp.where(st_mask, _mm_tn(k * jnp.exp(b_end - b), v), 0.0))
            cd_col = jnp.transpose(jnp.broadcast_to(jnp.exp(b_end), (GLA_QW, GLA_QW)))
            cds.append(jnp.concatenate([cd_col, cd_col], axis=1))
        outs = [_mm(attns[n], _block_diag(raws[n][2], bdv_mask, N_HEADS)) for n in range(ni)]
        for n, (t, d) in enumerate(items):
            outs[n] = outs[n] + _mm(q_es[n], states[d])
            states[d] = cds[n] * states[d] + dsts[n]
        for n, (t, d) in enumerate(items):
            acc_s[rows_of(chunk_of(jw, t, d)), :] += outs[n]
        for d in range(N_DIRS):
            st_s[d] = states[d]
        return carry

    lax.fori_loop(0, nchunk // GLA_WINDOW, step, 0)
    o_ref[...] = _bf(acc_s[...])


def _gla(gla_in, small, wup, bg, consts, batch, seq):
    T = gla_in.shape[0]
    tri, bd_v, st_mask, kq_mask = consts
    whole = lambda b: (b, 0)
    c2 = lambda b: (0, 0)
    c3 = lambda b: (0, 0, 0)
    return pl.pallas_call(
        _gla_kernel,
        grid=(batch,),
        in_specs=[
            pl.BlockSpec((seq, 512), whole),
            pl.BlockSpec((seq, LANES), whole),
            pl.BlockSpec(wup.shape, c3),
            pl.BlockSpec(bg.shape, c2),
            pl.BlockSpec(tri.shape, c3),
            pl.BlockSpec(bd_v.shape, c2),
            pl.BlockSpec(st_mask.shape, c2),
            pl.BlockSpec(kq_mask.shape, c2),
        ],
        out_specs=pl.BlockSpec((seq, 256), whole),
        out_shape=jax.ShapeDtypeStruct((T, 256), BF16),
        scratch_shapes=[pltpu.VMEM((N_DIRS, GLA_QW, GLA_VW), F32), pltpu.VMEM((seq, 256), F32)],
        compiler_params=pltpu.CompilerParams(
            dimension_semantics=("arbitrary",), vmem_limit_bytes=VMEM_LIMIT),
        name="gla",
    )(gla_in, small, wup, bg, tri, bd_v, st_mask, kq_mask)


def _outproj_kernel(ya_ref, ob_ref, oc_ref, ym_ref, gates_ref, gb_ref, gc_ref, ones_ref, w_ref, h_ref,
                    fg_ref, o_ref, *, final):
    ones256 = ones_ref[...]

    def head_norm(x, g):
        ms = _group_sum(x * x, ones256) * (1.0 / HEAD_DIM)
        return x * lax.rsqrt(ms + EPS) * g

    yb = head_norm(ob_ref[...].astype(F32), gb_ref[...]) * gates_ref[:, 0:256].astype(F32)
    yc = head_norm(oc_ref[...].astype(F32), gc_ref[...]) * gates_ref[:, 256:512].astype(F32)
    y = jnp.dot(ya_ref[...], w_ref[0:256, :], preferred_element_type=F32)
    y = y + jnp.dot(_bf(yb), w_ref[256:512, :], preferred_element_type=F32)
    y = y + jnp.dot(_bf(yc), w_ref[512:768, :], preferred_element_type=F32)
    y = y + jnp.dot(ym_ref[...], w_ref[768:1024, :], preferred_element_type=F32)
    hn = h_ref[...] + y
    if final:
        hn = hn * lax.rsqrt(jnp.mean(hn * hn, axis=-1, keepdims=True) + EPS) * fg_ref[...]
    o_ref[...] = hn


def _outproj(ya, ob, oc, ym, gates, gb, gc, ones256, w, h, fg, tm, final):
    T, D = h.shape
    row = lambda i: (i, 0)
    const = lambda i: (0, 0)
    return pl.pallas_call(
        functools.partial(_outproj_kernel, final=final),
        grid=(T // tm,),
        in_specs=[
            pl.BlockSpec((tm, 256), row), pl.BlockSpec((tm, 256), row),
            pl.BlockSpec((tm, 256), row), pl.BlockSpec((tm, 256), row),
            pl.BlockSpec((tm, 512), lambda i: (i, 1)),
            pl.BlockSpec((1, 256), const), pl.BlockSpec((1, 256), const),
            pl.BlockSpec((256, 256), const),
            pl.BlockSpec((D, D), const),
            pl.BlockSpec((tm, D), row),
            pl.BlockSpec((1, D), const),
        ],
        out_specs=pl.BlockSpec((tm, D), row),
        out_shape=jax.ShapeDtypeStruct((T, D), F32),
        compiler_params=pltpu.CompilerParams(
            dimension_semantics=("arbitrary",), vmem_limit_bytes=VMEM_LIMIT),
        name="outproj_final" if final else "outproj",
    )(ya, ob, oc, ym, gates, gb, gc, ones256, w, h, fg)


def _block_ones(n, group):
    i = np.arange(n)
    return jnp.asarray((i[:, None] // group) == (i[None, :] // group), dtype=BF16)


def _head_perm_cols():
    return np.concatenate([np.arange(h * HEAD_DIM, (h + 1) * HEAD_DIM) for h in ATT_HEAD_PERM])


def _prep_w_in(w_in):
    o = np.cumsum([0, 256, 128, 128, 256, 768, 8, 8, 256, 128, 128, 256, 32, 256, 256, 256])
    (a_q, a_k, a_v, a_g, d_qkv, d_beta, d_alpha, d_gate, l_q, l_k, l_v, l_low, l_gate, m_q, m_gate) = [
        w_in[:, :, o[i]:o[i + 1]] for i in range(15)]
    perm = _head_perm_cols()
    pad = jnp.zeros(w_in.shape[:2] + (N_COLS - C_SM - 48,), w_in.dtype)
    cols = [a_q[:, :, perm], a_k, a_v, a_g[:, :, perm], d_qkv, d_gate, l_q, l_k, l_v, l_gate, m_q, m_gate,
            d_beta, d_alpha, l_low, pad]
    return jnp.concatenate(cols, axis=-1).astype(BF16)


def _rope_tables(seq):
    rows = seq // GRID_W
    row_pos = jnp.repeat(jnp.arange(rows, dtype=F32), GRID_W)
    col_pos = jnp.tile(jnp.arange(GRID_W, dtype=F32), rows)
    half = HEAD_DIM // 4
    inv_freq = ROPE_THETA ** (-jnp.arange(0, 2 * half, 2, dtype=F32) / (2 * half))
    ang = jnp.stack([row_pos, col_pos], axis=-1)[..., None] * inv_freq
    cos, sin = jnp.cos(ang), jnp.sin(ang)
    cos_h = jnp.concatenate([cos, cos], axis=-1).reshape(seq, HEAD_DIM)
    sin_h = jnp.concatenate([-sin, sin], axis=-1).reshape(seq, HEAD_DIM)
    return jnp.tile(cos_h, (1, 2)), jnp.tile(sin_h, (1, 2))


def _trunk(x, mem, p, consts):
    batch, seq, d_model = x.shape
    T = batch * seq
    depth = p["w_in"].shape[0]
    tm = 512 if seq % 512 == 0 else seq
    tq = 256 if seq % 256 == 0 else seq
    kvc = 512 if seq % 512 == 0 else seq
    cos_t, sin_t = _rope_tables(seq)
    mem2d = mem.reshape(batch * mem.shape[1], d_model)
    memkv = _memkv(mem2d, p["mem_norm_g"], p["w_mem_kv"], min(512, mem2d.shape[0]))
    h = x.reshape(T, d_model)
    for d in range(depth):
        qkv, gates, gdn_in, gla_in, mq, small = _inproj(
            h, p["norm_g"][d], p["w_in"][d], cos_t, sin_t, p["qkg"][d], consts["ones384"], seq, tm)
        ya, ym = _attention(qkv, mq, memkv[d], gates, batch, seq, tq, kvc)
        ob = _gdn(gdn_in, small, p["conv_w"][d], p["gdn_par"][d], consts["gdn"], batch, seq)
        oc = _gla(gla_in, small, p["wup"][d], p["bg"][d], consts["gla"], batch, seq)
        h = _outproj(ya, ob, oc, ym, gates, p["gb"][d], p["gc"][d],
                     consts["ones256"], p["w_out"][d], h, p["final_g"], tm, final=(d == depth - 1))
    return h.reshape(batch, seq, d_model)


def kernel(x_prompt, x_sample, mem_prompt, mem_sample, norm_g, w_in, att_q_norm_g, att_k_norm_g, gdn_conv_w,
           gdn_a_log, gdn_dt_bias, gdn_out_norm_g, gla_w_gate_up, gla_b_gate, gla_out_norm_g, mem_norm_g,
           w_mem_kv, w_out, final_norm_g):
    depth, d_model = norm_g.shape
    perm = _head_perm_cols()
    w_out_p = jnp.concatenate([w_out[:, :256][:, perm], w_out[:, 256:]], axis=1).astype(BF16)
    qkg = jnp.concatenate([jnp.tile(att_q_norm_g, (1, N_HEADS)), jnp.tile(att_k_norm_g, (1, KV_HEADS))], axis=1)
    zeros8 = jnp.zeros((depth, N_COMBO), F32)
    par_a = jnp.concatenate([zeros8, -jnp.exp(gdn_a_log.reshape(depth, N_COMBO)),
                             jnp.zeros((depth, LANES - 2 * N_COMBO), F32)], axis=1)
    par_b = jnp.concatenate([zeros8, gdn_dt_bias.reshape(depth, N_COMBO),
                             jnp.zeros((depth, LANES - 2 * N_COMBO), F32)], axis=1)
    gdn_par = jnp.concatenate([par_a[:, None], par_b[:, None], jnp.zeros((depth, 6, LANES), F32)], axis=1)
    conv_w = jnp.concatenate([gdn_conv_w, jnp.zeros((depth, 5, gdn_conv_w.shape[2]), F32)], axis=1)
    wup = jnp.zeros((depth, N_DIRS, LANES, GLA_QW), F32)
    for d in range(N_DIRS):
        lo = 2 * N_COMBO + d * GLA_RANK
        wup = wup.at[:, d, lo:lo + GLA_RANK, :].set(gla_w_gate_up[:, d])
    p = {
        "norm_g": norm_g.reshape(depth, 1, d_model),
        "w_in": _prep_w_in(w_in),
        "qkg": qkg.reshape(depth, 1, 384),
        "conv_w": conv_w,
        "gdn_par": gdn_par,
        "wup": wup.astype(BF16),
        "bg": jnp.concatenate([gla_b_gate, jnp.zeros((depth, 6, GLA_QW), F32)], axis=1),
        "gb": jnp.tile(gdn_out_norm_g, (1, N_HEADS)).reshape(depth, 1, 256),
        "gc": jnp.tile(gla_out_norm_g, (1, N_HEADS)).reshape(depth, 1, 256),
        "mem_norm_g": mem_norm_g.reshape(depth, 1, d_model),
        "w_mem_kv": w_mem_kv.astype(BF16),
        "w_out": w_out_p,
        "final_g": final_norm_g.reshape(1, d_model),
    }
    tri_c, bd, expand, cm = _gdn_constants()
    consts = {
        "ones384": _block_ones(384, HEAD_DIM),
        "ones256": _block_ones(256, HEAD_DIM),
        "gdn": (jnp.asarray(tri_c), jnp.asarray(bd), jnp.asarray(expand, dtype=BF16), jnp.asarray(cm),
                _block_ones(256, HEAD_DIM)),
        "gla": tuple(jnp.asarray(a) for a in _gla_constants()),
    }
    y_prompt = _trunk(x_prompt, mem_prompt, p, consts)
    y_sample = _trunk(x_sample, mem_sample, p, consts)
    return (y_prompt, y_sample)
```

```python
import functools

import numpy as np
import jax
import jax.numpy as jnp
from jax import lax
from jax.experimental import pallas as pl
from jax.experimental.pallas import tpu as pltpu

F32 = jnp.float32
BF16 = jnp.bfloat16

N_HEADS = 4
HEAD_DIM = 64
KV_HEADS = 2
GRID_W = 64
ROPE_THETA = 10000.0
CHUNK = 64
GLA_DK = 32
GLA_RANK = 16
GLA_NORMALIZER = 16.0
EPS = 1e-6
N_DIRS = 2
LANES = 128
GDN_GROUP = 2
GDN_WINDOW = 4
VMEM_LIMIT = 56 * 1024 * 1024

C_AQ, C_AK, C_AV, C_AG = 0, 256, 384, 512
C_DQKV, C_DG = 768, 1536
C_LQ, C_LK, C_LV, C_LG = 1792, 1920, 2048, 2304
C_MQ, C_MG = 2560, 2816
C_SM = 3072
N_COLS = 3200
ATT_HEAD_PERM = (0, 2, 1, 3)


def _bf(x):
    return x.astype(BF16)


def _mm(a, b):
    return jnp.dot(_bf(a), _bf(b), preferred_element_type=F32)


def _mm_nt(a, b):
    return lax.dot_general(_bf(a), _bf(b), (((1,), (1,)), ((), ())), preferred_element_type=F32)


def _mm_tn(a, b):
    return lax.dot_general(_bf(a), _bf(b), (((0,), (0,)), ((), ())), preferred_element_type=F32)


def _split2(x):
    hi = _bf(x)
    lo = _bf(x - hi.astype(F32))
    return hi, lo


def _split3(x):
    hi = _bf(x).astype(F32)
    r = x - hi
    mid = _bf(r).astype(F32)
    lo = _bf(r - mid).astype(F32)
    return hi, mid, lo


def _group_sums(xs, ones_blocks):
    rows = xs[0].shape[0]
    parts = []
    for x in xs:
        parts.extend(_split2(x))
    s = jnp.dot(jnp.concatenate(parts, axis=0), ones_blocks, preferred_element_type=F32)
    return [s[2 * i * rows:(2 * i + 1) * rows] + s[(2 * i + 1) * rows:(2 * i + 2) * rows] for i in range(len(xs))]


def _silu(x):
    return x * (0.5 * jnp.tanh(0.5 * x) + 0.5)


def _softplus(x):
    return jnp.maximum(x, 0.0) + jnp.log1p(jnp.exp(-jnp.abs(x)))


def _cumsum_rows(x, reverse):
    n = x.shape[0]
    row = lax.broadcasted_iota(jnp.int32, x.shape, 0)
    s = 1
    while s < n:
        if reverse:
            x = x + jnp.where(row < n - s, pltpu.roll(x, n - s, axis=0), 0.0)
        else:
            x = x + jnp.where(row >= s, pltpu.roll(x, s, axis=0), 0.0)
        s *= 2
    return x


def _block_diag(x, mask, reps):
    xb = _bf(x)
    t = jnp.concatenate([xb] * reps, axis=0)
    return jnp.where(mask, t, jnp.zeros_like(t))


def _inproj_kernel(h_ref, hprev_ref, hnext_ref, g_ref, w_ref, cos_ref, sin_ref, qkg_ref, ones_ref,
                   cw_ref, par_ref, ones256_ref,
                   qkv_ref, gates_ref, gdn_ref, gla_ref, mq_ref, small_ref, sm3_ref, *, nseq):
    tm = h_ref.shape[0]

    def normed(x):
        return _bf(x * lax.rsqrt(jnp.mean(x * x, axis=-1, keepdims=True) + EPS) * g_ref[...])

    proj = jnp.dot(normed(h_ref[...]), w_ref[...], preferred_element_type=F32)

    halo = jnp.concatenate([hprev_ref[0], hnext_ref[0]], axis=0)
    hp = jnp.dot(normed(halo), w_ref[:, C_DQKV:C_DG], preferred_element_type=F32)
    pos = pl.program_id(0) % nseq
    prev_row = hp[7:8] * jnp.where(pos > 0, 1.0, 0.0)
    next_row = hp[8:9] * jnp.where(pos < nseq - 1, 1.0, 0.0)
    xg = proj[:, C_DQKV:C_DG]
    trow = lax.broadcasted_iota(jnp.int32, (tm, 1), 0)
    x_prev = jnp.where(trow == 0, prev_row, pltpu.roll(xg, 1, axis=0))
    x_next = jnp.where(trow == tm - 1, next_row, pltpu.roll(xg, tm - 1, axis=0))
    y = x_prev * cw_ref[0:1, :] + xg * cw_ref[1:2, :]
    y = _silu(y + x_next * cw_ref[2:3, :])
    sss = _group_sums([y[:, 0:256] * y[:, 0:256], y[:, 256:512] * y[:, 256:512]], ones256_ref[...])
    gdn_ref[:, 0:256] = _bf(y[:, 0:256] * lax.rsqrt(sss[0] + EPS) * (HEAD_DIM ** -0.5))
    gdn_ref[:, 256:512] = _bf(y[:, 256:512] * lax.rsqrt(sss[1] + EPS))
    gdn_ref[:, 512:768] = _bf(y[:, 512:768])

    lane = lax.broadcasted_iota(jnp.int32, (1, LANES), 1)
    lane8 = lane < N_COMBO
    fwd_lane = lane < N_HEADS
    for c in range(tm // CHUNK):
        sm = proj[c * CHUNK:(c + 1) * CHUNK, C_SM:N_COLS]
        beta = 0.5 * jnp.tanh(0.5 * sm) + 0.5
        g_all = par_ref[0:1, :] * _softplus(sm + par_ref[1:2, :])
        g = pltpu.roll(g_all, LANES - N_COMBO, axis=1)
        b = jnp.where(fwd_lane, _cumsum_rows(g, reverse=False), _cumsum_rows(g, reverse=True))
        zero = jnp.zeros_like(b)
        q2 = jnp.where(lane8, beta, zero) + pltpu.roll(jnp.where(lane8, b, zero), N_COMBO, axis=1)
        hi, mid, lo3 = _split3(q2)
        sm3_ref[c * CHUNK:(c + 1) * CHUNK, :] = _bf(
            hi + pltpu.roll(mid, 2 * N_COMBO, axis=1) + pltpu.roll(lo3, 4 * N_COMBO, axis=1))

    qk = proj[:, C_AQ:C_AV]
    ms = _group_sums([qk * qk], ones_ref[...])[0] * (1.0 / HEAD_DIM)
    qkn = qk * lax.rsqrt(ms + EPS) * qkg_ref[...]
    cos = cos_ref[...]
    sin = sin_ref[...]
    lane = lax.broadcasted_iota(jnp.int32, (1, LANES), 1)
    first_half = (lane % 32) < 16
    outs = []
    for c in range(3):
        xc = qkn[:, c * LANES:(c + 1) * LANES]
        partner = jnp.where(first_half, pltpu.roll(xc, LANES - 16, axis=1), pltpu.roll(xc, 16, axis=1))
        outs.append(xc * cos + partner * sin)
    qkv_ref[:, 0:128] = _bf(outs[0] * (HEAD_DIM ** -0.5))
    qkv_ref[:, 128:256] = _bf(outs[1] * (HEAD_DIM ** -0.5))
    qkv_ref[:, 256:384] = _bf(outs[2])
    qkv_ref[:, 384:512] = _bf(proj[:, C_AV:C_AG])

    gates_ref[:, 0:256] = _bf(_silu(proj[:, C_AG:C_AG + 256]))
    gates_ref[:, 256:512] = _bf(_silu(proj[:, C_MG:C_MG + 256]))
    gates_ref[:, 512:768] = _bf(_silu(proj[:, C_DG:C_DG + 256]))
    gates_ref[:, 768:1024] = _bf(_silu(proj[:, C_LG:C_LG + 256]))
    gla_ref[...] = _bf(proj[:, C_LQ:C_LG])
    mq_ref[...] = _bf(proj[:, C_MQ:C_MG] * (HEAD_DIM ** -0.5))
    small_ref[...] = proj[:, C_SM:N_COLS]


def _inproj(h, g, w, cos_t, sin_t, qkg, ones384, conv_w, par, ones256, seq, tm):
    T, D = h.shape
    nseq = seq // tm
    row = lambda i: (i, 0)
    const = lambda i: (0, 0)
    h8 = h.reshape(T // 8, 8, D)
    g8 = tm // 8
    return pl.pallas_call(
        functools.partial(_inproj_kernel, nseq=nseq),
        grid=(T // tm,),
        in_specs=[
            pl.BlockSpec((tm, D), row),
            pl.BlockSpec((1, 8, D), lambda i: (jnp.maximum(i * g8 - 1, 0), 0, 0)),
            pl.BlockSpec((1, 8, D), lambda i: (jnp.minimum((i + 1) * g8, T // 8 - 1), 0, 0)),
            pl.BlockSpec((1, D), const),
            pl.BlockSpec((D, N_COLS), const),
            pl.BlockSpec((tm, LANES), lambda i: (i % nseq, 0)),
            pl.BlockSpec((tm, LANES), lambda i: (i % nseq, 0)),
            pl.BlockSpec((1, 384), const),
            pl.BlockSpec((384, 384), const),
            pl.BlockSpec((8, 768), const),
            pl.BlockSpec((8, LANES), const),
            pl.BlockSpec((256, 256), const),
        ],
        out_specs=[
            pl.BlockSpec((tm, 512), row),
            pl.BlockSpec((tm, 1024), row),
            pl.BlockSpec((tm, 768), row),
            pl.BlockSpec((tm, 512), row),
            pl.BlockSpec((tm, 256), row),
            pl.BlockSpec((tm, LANES), row),
            pl.BlockSpec((tm, LANES), row),
        ],
        out_shape=[
            jax.ShapeDtypeStruct((T, 512), BF16),
            jax.ShapeDtypeStruct((T, 1024), BF16),
            jax.ShapeDtypeStruct((T, 768), BF16),
            jax.ShapeDtypeStruct((T, 512), BF16),
            jax.ShapeDtypeStruct((T, 256), BF16),
            jax.ShapeDtypeStruct((T, LANES), F32),
            jax.ShapeDtypeStruct((T, LANES), BF16),
        ],
        compiler_params=pltpu.CompilerParams(
            dimension_semantics=("arbitrary",), vmem_limit_bytes=VMEM_LIMIT),
        name="inproj",
    )(h, h8, h8, g, w, cos_t, sin_t, qkg, ones384, conv_w, par, ones256)


def _memkv_kernel(m_ref, g_ref, w_ref, o_ref):
    x = m_ref[...]
    xn = x * lax.rsqrt(jnp.mean(x * x, axis=-1, keepdims=True) + EPS) * g_ref[0]
    o_ref[0] = _bf(jnp.dot(_bf(xn), w_ref[0], preferred_element_type=F32))


def _memkv(mem2d, g, w, tr):
    R, D = mem2d.shape
    depth = w.shape[0]
    return pl.pallas_call(
        _memkv_kernel,
        grid=(depth, R // tr),
        in_specs=[
            pl.BlockSpec((tr, D), lambda d, i: (i, 0)),
            pl.BlockSpec((1, 1, D), lambda d, i: (d, 0, 0)),
            pl.BlockSpec((1, D, 512), lambda d, i: (d, 0, 0)),
        ],
        out_specs=pl.BlockSpec((1, tr, 512), lambda d, i: (d, i, 0)),
        out_shape=jax.ShapeDtypeStruct((depth, R, 512), BF16),
        compiler_params=pltpu.CompilerParams(
            dimension_semantics=("arbitrary", "arbitrary"), vmem_limit_bytes=VMEM_LIMIT),
        name="memkv",
    )(mem2d, g, w)


def _attn_kernel(q_ref, k_ref, v_ref, mq_ref, km_ref, vm_ref, ga_ref, gm_ref, ya_ref, ym_ref, vt_s, *, kvc):
    tq = q_ref.shape[0]
    seq = k_ref.shape[0]
    lane = lax.broadcasted_iota(jnp.int32, (1, LANES), 1)
    lo = lane < HEAD_DIM

    def stack_heads(x):
        z = jnp.zeros_like(x)
        return [jnp.where(lo, x, z), jnp.where(lo, z, x)]

    @pl.when(pl.program_id(1) == 0)
    def _():
        sub = lax.broadcasted_iota(jnp.int32, (LANES, 1), 0) < HEAD_DIM
        for c in range(seq // kvc):
            vt = jnp.transpose(v_ref[c * kvc:(c + 1) * kvc, :].astype(F32))
            vt_s[0, c] = _bf(jnp.where(sub, vt, 1.0))
            vt_s[1, c] = _bf(jnp.where(sub, 1.0, vt))

    q = q_ref[...]
    qm = stack_heads(q[:, :LANES]) + stack_heads(q[:, LANES:])
    n_heads = len(qm)

    def scores(c):
        kc = k_ref[c * kvc:(c + 1) * kvc, :]
        return [lax.dot_general(kc, qm[h], (((1,), (1,)), ((), ())), preferred_element_type=F32)
                for h in range(n_heads)]

    ms = [jnp.full((1, tq), -jnp.inf, F32) for _ in range(n_heads)]
    accs = [jnp.zeros((LANES, tq), F32) for _ in range(n_heads)]
    nsub = seq // kvc
    sts = scores(0)
    for c in range(nsub):
        nxt = scores(c + 1) if c + 1 < nsub else None
        new_ms = [jnp.maximum(ms[h], jnp.max(sts[h], axis=0, keepdims=True)) for h in range(n_heads)]
        ps = [_bf(jnp.exp(sts[h] - new_ms[h])) for h in range(n_heads)]
        accs = [jnp.exp(ms[h] - new_ms[h]) * accs[h]
                + jnp.dot(vt_s[h % 2, c], ps[h], preferred_element_type=F32) for h in range(n_heads)]
        ms, sts = new_ms, nxt
    o_t = []
    for h in range(n_heads):
        if h % 2 == 0:
            o_t.append(accs[h][0:HEAD_DIM] / accs[h][HEAD_DIM:HEAD_DIM + 1])
        else:
            o_t.append(accs[h][HEAD_DIM:LANES] / accs[h][0:1])
    ya = jnp.transpose(jnp.concatenate(o_t, axis=0))
    ya_ref[...] = _bf(ya * ga_ref[...].astype(F32))

    mq = mq_ref[...]
    outs = []
    for c in range(2):
        qc = jnp.concatenate(stack_heads(mq[:, c * LANES:(c + 1) * LANES]), axis=0)
        kc = km_ref[:, c * LANES:(c + 1) * LANES]
        vc = vm_ref[:, c * LANES:(c + 1) * LANES]
        s = lax.dot_general(qc, kc, (((1,), (1,)), ((), ())), preferred_element_type=F32)
        p = jnp.exp(s - jnp.max(s, axis=-1, keepdims=True))
        om = jnp.dot(_bf(p), vc, preferred_element_type=F32) / jnp.sum(p, axis=-1, keepdims=True)
        outs.append(jnp.where(lo, om[0:tq], om[tq:2 * tq]))
    ym_ref[...] = _bf(jnp.concatenate(outs, axis=1) * gm_ref[...].astype(F32))


def _attention(qkv, mq, memkv, gates, batch, seq, tq, kvc):
    T = qkv.shape[0]
    nq = seq // tq
    mem_len = memkv.shape[0] // batch
    qrow = lambda b, i: (b * nq + i, 0)
    return pl.pallas_call(
        functools.partial(_attn_kernel, kvc=kvc),
        grid=(batch, nq),
        in_specs=[
            pl.BlockSpec((tq, 256), qrow),
            pl.BlockSpec((seq, LANES), lambda b, i: (b, 2)),
            pl.BlockSpec((seq, LANES), lambda b, i: (b, 3)),
            pl.BlockSpec((tq, 256), qrow),
            pl.BlockSpec((mem_len, 256), lambda b, i: (b, 0)),
            pl.BlockSpec((mem_len, 256), lambda b, i: (b, 1)),
            pl.BlockSpec((tq, 256), lambda b, i: (b * nq + i, 0)),
            pl.BlockSpec((tq, 256), lambda b, i: (b * nq + i, 1)),
        ],
        out_specs=[pl.BlockSpec((tq, 256), qrow), pl.BlockSpec((tq, 256), qrow)],
        out_shape=[jax.ShapeDtypeStruct((T, 256), BF16), jax.ShapeDtypeStruct((T, 256), BF16)],
        scratch_shapes=[pltpu.VMEM((KV_HEADS, seq // kvc, LANES, kvc), BF16)],
        compiler_params=pltpu.CompilerParams(
            dimension_semantics=("arbitrary", "arbitrary"), vmem_limit_bytes=VMEM_LIMIT),
        name="attention",
    )(qkv, qkv, qkv, mq, memkv, memkv, gates, gates)


GDN_W = GDN_GROUP * HEAD_DIM
GDN_NGRP = N_HEADS // GDN_GROUP
N_COMBO = N_DIRS * N_HEADS
N_QTY = 2


def _gdn_constants():
    w = GDN_W
    i = np.arange(CHUNK)[:, None]
    lane = np.arange(w)[None, :]
    j = lane % CHUNK
    tri = np.stack([i >= j, i <= j]).astype(np.float32)
    stri = np.stack([i > j, i < j]).astype(np.float32)
    eye = (i == j).astype(np.float32)[None]
    tri_c = np.concatenate([tri, stri, eye], axis=0)
    r = np.arange(w)[:, None]
    bd = ((r // CHUNK) == (lane // CHUNK)).astype(np.float32)
    expand = np.zeros((N_DIRS * GDN_NGRP, LANES, N_QTY * w), np.float32)
    for d in range(N_DIRS):
        for g in range(GDN_NGRP):
            for qty in range(N_QTY):
                for hh in range(GDN_GROUP):
                    combo = d * N_HEADS + g * GDN_GROUP + hh
                    for piece in range(3):
                        src = piece * (N_QTY * N_COMBO) + qty * N_COMBO + combo
                        expand[d * GDN_NGRP + g, src, qty * w + hh * CHUNK: qty * w + (hh + 1) * CHUNK] = 1.0
    return tri_c, bd, expand


def _gdn_kernel(qkv_ref, sm3_ref, tri_ref, bd_ref, exp_ref, o_ref, st_s, pa_s, pb_s, cd_s, acc_s):
    seq = qkv_ref.shape[0]
    nchunk = seq // CHUNK
    w = GDN_W
    acc_s[...] = jnp.zeros_like(acc_s)
    st_s[...] = jnp.zeros_like(st_s)
    bd_mask = bd_ref[...] > 0.5

    combos = [(d, grp) for d in range(N_DIRS) for grp in range(GDN_NGRP)]

    def load_chunk(chunk, d, grp):
        rs = pl.ds(pl.multiple_of(chunk * CHUNK, CHUNK), CHUNK)
        return (qkv_ref[rs, grp * w:(grp + 1) * w], qkv_ref[rs, 256 + grp * w:256 + (grp + 1) * w],
                qkv_ref[rs, 512 + grp * w:512 + (grp + 1) * w], sm3_ref[rs, :])

    nc = len(combos)
    head_masks = [(lax.broadcasted_iota(jnp.int32, (1, w), 1) // HEAD_DIM) == hh for hh in range(GDN_GROUP)]

    items = [(d, grp) for _ in range(GDN_WINDOW) for d, grp in combos]
    ni = len(items)

    def transform_stages(raws, outs):
        exs, diffs, kqs = [], [], []
        a_mats, ps, rs_ = [], [], []

        def stage_products():
            ex_all = [jnp.dot(jnp.concatenate([raws[t * nc + c][3] for t in range(GDN_WINDOW)], axis=0),
                              exp_ref[c], preferred_element_type=F32) for c in range(nc)]
            for n, (d, grp) in enumerate(items):
                q, k, vb, _ = raws[n]
                ex = ex_all[n % nc][(n // nc) * CHUNK:(n // nc + 1) * CHUNK]
                betax, bx = ex[:, 0:w], ex[:, w:2 * w]
                ebx = jnp.exp(bx)
                b_end = bx[CHUNK - 1:CHUNK] if d == 0 else bx[0:1]
                exs.append([betax, betax * ebx, ebx, jnp.exp(b_end - bx)])
                diffs.append(bx - jnp.sum(bx * tri_ref[4], axis=0, keepdims=True))
                kbd = jnp.concatenate([jnp.where(hm, k, jnp.zeros_like(k)) for hm in head_masks], axis=0)
                kqs.append(lax.dot_general(jnp.concatenate([k, q], axis=0), kbd, (((1,), (1,)), ((), ())),
                                           preferred_element_type=F32))

        def stage_square():
            for n, (d, grp) in enumerate(items):
                tri = tri_ref[d] > 0.5
                stri = tri_ref[2 + d] > 0.5
                decay = jnp.where(tri, jnp.exp(jnp.where(tri, diffs[n], 0.0)), 0.0)
                low = jnp.where(stri, exs[n][0] * kqs[n][:CHUNK] * decay, 0.0)
                a_mats.append(kqs[n][CHUNK:] * decay)
                ps.append(_mm(low, _block_diag(low, bd_mask, GDN_GROUP)))
                rs_.append(tri_ref[4] - low)

        def stage_level(last):
            for n in range(ni):
                pbd = _block_diag(ps[n], bd_mask, GDN_GROUP)
                if last:
                    rs_[n] = rs_[n] + _mm(rs_[n], pbd)
                else:
                    rp = _mm(jnp.concatenate([rs_[n], ps[n]], axis=0), pbd)
                    rs_[n] = rs_[n] + rp[:CHUNK]
                    ps[n] = rp[CHUNK:]

        def stage_apply():
            for n, (d, grp) in enumerate(items):
                q, k, vb = raws[n][:3]
                betax, bebx, ebx, edx = exs[n]
                kf = k.astype(F32)
                u = _mm(rs_[n], _block_diag(betax * vb.astype(F32), bd_mask, GDN_GROUP))
                wm = _mm(rs_[n], _block_diag(bebx * kf, bd_mask, GDN_GROUP))
                cd_row = ebx[CHUNK - 1:CHUNK] if d == 0 else ebx[0:1]
                outs.append((a_mats[n], u, _bf(wm), _bf(q.astype(F32) * ebx), _bf(kf * edx), cd_row))

        return ([stage_products, stage_square] + [functools.partial(stage_level, False)] * 4
                + [functools.partial(stage_level, True), stage_apply])

    def store_transformed(c, t):
        a_mat, u, wm, q_dec, k_dec, cd_row = t
        pa_s[2 * c] = a_mat
        pa_s[2 * c + 1] = u
        pb_s[3 * c] = wm
        pb_s[3 * c + 1] = q_dec
        pb_s[3 * c + 2] = k_dec
        cd_s[c] = jnp.broadcast_to(cd_row, (8, w))

    def load_transformed(c):
        return pa_s[2 * c], pa_s[2 * c + 1], pb_s[3 * c], pb_s[3 * c + 1], pb_s[3 * c + 2], cd_s[c][0:1]

    def scan_stages(states, cur, os_):
        tmp = {}

        def stage_ws(t):
            tmp["ws"] = [_mm(jnp.concatenate([cur[t * nc + c][2], cur[t * nc + c][3]], axis=0), states[c])
                         for c in range(nc)]

        def stage_update(t):
            for c in range(nc):
                a_mat, u, _, _, k_dec, cd_row = cur[t * nc + c]
                v_new = u - tmp["ws"][c][:CHUNK]
                os_.append(tmp["ws"][c][CHUNK:] + _mm(a_mat, _block_diag(v_new, bd_mask, GDN_GROUP)))
                states[c] = cd_row * states[c] + jnp.where(bd_mask, _mm_tn(k_dec, v_new), 0.0)

        stages = []
        for t in range(GDN_WINDOW):
            stages += [functools.partial(stage_ws, t), functools.partial(stage_update, t)]
        return stages

    def chunk_of(j, t, d):
        i = j * GDN_WINDOW + t
        return i if d == 0 else nchunk - 1 - i

    def load_window(j):
        return [load_chunk(chunk_of(j, t, d), d, grp) for t in range(GDN_WINDOW) for d, grp in combos]

    first = []
    for stage in transform_stages(load_window(0), first):
        stage()
    for n in range(ni):
        store_transformed(n, first[n])

    nwin = nchunk // GDN_WINDOW

    def step(j, carry):
        cur = [load_transformed(n) for n in range(ni)]
        states = [st_s[c] for c in range(nc)]
        raws = load_window(jnp.minimum(j + 1, nwin - 1))
        os_, nexts = [], []
        t_stages = transform_stages(raws, nexts)
        s_stages = scan_stages(states, cur, os_)
        for k in range(max(len(t_stages), len(s_stages))):
            if k < len(t_stages):
                t_stages[k]()
            if k < len(s_stages):
                s_stages[k]()
        for t in range(GDN_WINDOW):
            for c, (d, grp) in enumerate(combos):
                rs = pl.ds(pl.multiple_of(chunk_of(j, t, d) * CHUNK, CHUNK), CHUNK)
                acc_s[rs, grp * w:(grp + 1) * w] += os_[t * nc + c]
        for c in range(nc):
            st_s[c] = states[c]
        for n in range(ni):
            store_transformed(n, nexts[n])
        return carry

    lax.fori_loop(0, nwin, step, 0)
    o_ref[...] = _bf(acc_s[...])


def _gdn(gdn_qkv, gdn_sm3, consts, batch, seq):
    T = gdn_qkv.shape[0]
    assert seq % (CHUNK * GDN_WINDOW) == 0
    tri_c, bd, expand = consts
    whole = lambda b: (b, 0)
    c2 = lambda b: (0, 0)
    c3 = lambda b: (0, 0, 0)
    return pl.pallas_call(
        _gdn_kernel,
        grid=(batch,),
        in_specs=[
            pl.BlockSpec((seq, 768), whole),
            pl.BlockSpec((seq, LANES), whole),
            pl.BlockSpec(tri_c.shape, c3),
            pl.BlockSpec(bd.shape, c2),
            pl.BlockSpec(expand.shape, c3),
        ],
        out_specs=pl.BlockSpec((seq, 256), whole),
        out_shape=jax.ShapeDtypeStruct((T, 256), BF16),
        scratch_shapes=[
            pltpu.VMEM((N_DIRS * GDN_NGRP, GDN_W, GDN_W), F32),
            pltpu.VMEM((2 * GDN_WINDOW * N_DIRS * GDN_NGRP, CHUNK, GDN_W), F32),
            pltpu.VMEM((3 * GDN_WINDOW * N_DIRS * GDN_NGRP, CHUNK, GDN_W), BF16),
            pltpu.VMEM((GDN_WINDOW * N_DIRS * GDN_NGRP, 8, GDN_W), F32),
            pltpu.VMEM((seq, 256), F32),
        ],
        compiler_params=pltpu.CompilerParams(
            dimension_semantics=("arbitrary",), vmem_limit_bytes=VMEM_LIMIT),
        name="gdn",
    )(gdn_qkv, gdn_sm3, tri_c, bd, expand)


GLA_QW = N_HEADS * GLA_DK
GLA_VW = N_HEADS * HEAD_DIM
GLA_WINDOW = 4


def _gla_constants():
    i = np.arange(CHUNK)[:, None]
    lane = np.arange(GLA_VW)[None, :]
    j = lane % CHUNK
    tri = np.stack([i >= j, i <= j]).astype(np.float32)
    r = np.arange(GLA_VW)[:, None]
    bd_v = ((r // CHUNK) == (lane // HEAD_DIM)).astype(np.float32)
    rs = np.arange(GLA_QW)[:, None]
    st_mask = ((rs // GLA_DK) == (lane // HEAD_DIM)).astype(np.float32)
    kq_mask = ((r // CHUNK) == (np.arange(GLA_QW)[None, :] // GLA_DK)).astype(np.float32)
    return tri, bd_v, st_mask, kq_mask


def _gla_kernel(x_ref, sm_ref, wup_ref, bg_ref, tri_ref, bdv_ref, stm_ref, kqm_ref, o_ref, st_s, acc_s):
    seq = x_ref.shape[0]
    nchunk = seq // CHUNK
    acc_s[...] = jnp.zeros_like(acc_s)
    st_s[...] = jnp.zeros_like(st_s)
    bdv_mask = bdv_ref[...] > 0.5
    st_mask = stm_ref[...] > 0.5
    kq_mask = kqm_ref[...] > 0.5

    def rows_of(chunk):
        return pl.ds(pl.multiple_of(chunk * CHUNK, CHUNK), CHUNK)

    def load_chunk(chunk):
        rs = rows_of(chunk)
        return (x_ref[rs, 0:GLA_QW].astype(F32), x_ref[rs, GLA_QW:2 * GLA_QW].astype(F32),
                x_ref[rs, 2 * GLA_QW:2 * GLA_QW + GLA_VW], sm_ref[rs, :])

    items = [(t, d) for t in range(GLA_WINDOW) for d in range(N_DIRS)]
    ni = len(items)

    def chunk_of(jw, t, d):
        i = jw * GLA_WINDOW + t
        return i if d == 0 else nchunk - 1 - i

    def step(jw, carry):
        raws = [load_chunk(chunk_of(jw, t, d)) for t, d in items]
        states = [st_s[d] for d in range(N_DIRS)]
        logits = [jnp.dot(_bf(raws[n][3]), wup_ref[d], preferred_element_type=F32) + bg_ref[d:d + 1, :]
                  for n, (t, d) in enumerate(items)]
        q_es, attns, dsts, cds = [], [], [], []
        for n, (t, d) in enumerate(items):
            q, k, v, _ = raws[n]
            gk = (jnp.minimum(logits[n], 0.0) - jnp.log1p(jnp.exp(-jnp.abs(logits[n])))) * (1.0 / GLA_NORMALIZER)
            b = _cumsum_rows(gk, reverse=(d == 1))
            b_end = b[CHUNK - 1:CHUNK] if d == 0 else b[0:1]
            q_e = q * (GLA_DK ** -0.5) * jnp.exp(b)
            keb = _bf(k * jnp.exp(-b))
            kebd = jnp.where(kq_mask, jnp.concatenate([keb] * N_HEADS, axis=0), jnp.zeros((GLA_VW, GLA_QW), BF16))
            q_es.append(_bf(q_e))
            attns.append(jnp.where(tri_ref[d] > 0.5, _mm_nt(q_es[n], kebd), 0.0))
            dsts.append(jnp.where(st_mask, _mm_tn(k * jnp.exp(b_end - b), v), 0.0))
            cd_col = jnp.transpose(jnp.broadcast_to(jnp.exp(b_end), (GLA_QW, GLA_QW)))
            cds.append(jnp.concatenate([cd_col, cd_col], axis=1))
        outs = [_mm(attns[n], _block_diag(raws[n][2], bdv_mask, N_HEADS)) for n in range(ni)]
        for n, (t, d) in enumerate(items):
            outs[n] = outs[n] + _mm(q_es[n], states[d])
            states[d] = cds[n] * states[d] + dsts[n]
        for n, (t, d) in enumerate(items):
            acc_s[rows_of(chunk_of(jw, t, d)), :] += outs[n]
        for d in range(N_DIRS):
            st_s[d] = states[d]
        return carry

    lax.fori_loop(0, nchunk // GLA_WINDOW, step, 0)
    o_ref[...] = _bf(acc_s[...])


def _gla(gla_in, small, wup, bg, consts, batch, seq):
    T = gla_in.shape[0]
    tri, bd_v, st_mask, kq_mask = consts
    whole = lambda b: (b, 0)
    c2 = lambda b: (0, 0)
    c3 = lambda b: (0, 0, 0)
    return pl.pallas_call(
        _gla_kernel,
        grid=(batch,),
        in_specs=[
            pl.BlockSpec((seq, 512), whole),
            pl.BlockSpec((seq, LANES), whole),
            pl.BlockSpec(wup.shape, c3),
            pl.BlockSpec(bg.shape, c2),
            pl.BlockSpec(tri.shape, c3),
            pl.BlockSpec(bd_v.shape, c2),
            pl.BlockSpec(st_mask.shape, c2),
            pl.BlockSpec(kq_mask.shape, c2),
        ],
        out_specs=pl.BlockSpec((seq, 256), whole),
        out_shape=jax.ShapeDtypeStruct((T, 256), BF16),
        scratch_shapes=[pltpu.VMEM((N_DIRS, GLA_QW, GLA_VW), F32), pltpu.VMEM((seq, 256), F32)],
        compiler_params=pltpu.CompilerParams(
            dimension_semantics=("arbitrary",), vmem_limit_bytes=VMEM_LIMIT),
        name="gla",
    )(gla_in, small, wup, bg, tri, bd_v, st_mask, kq_mask)


def _outproj_kernel(ya_ref, ob_ref, oc_ref, ym_ref, gates_ref, gb_ref, gc_ref, ones_ref, w_ref, h_ref,
                    fg_ref, o_ref, *, final):
    ones256 = ones_ref[...]

    ob = ob_ref[...].astype(F32)
    oc = oc_ref[...].astype(F32)
    ssb, ssc = _group_sums([ob * ob, oc * oc], ones256)
    yb = ob * lax.rsqrt(ssb * (1.0 / HEAD_DIM) + EPS) * gb_ref[...] * gates_ref[:, 0:256].astype(F32)
    yc = oc * lax.rsqrt(ssc * (1.0 / HEAD_DIM) + EPS) * gc_ref[...] * gates_ref[:, 256:512].astype(F32)
    y = jnp.dot(ya_ref[...], w_ref[0:256, :], preferred_element_type=F32)
    y = y + jnp.dot(_bf(yb), w_ref[256:512, :], preferred_element_type=F32)
    y = y + jnp.dot(_bf(yc), w_ref[512:768, :], preferred_element_type=F32)
    y = y + jnp.dot(ym_ref[...], w_ref[768:1024, :], preferred_element_type=F32)
    hn = h_ref[...] + y
    if final:
        hn = hn * lax.rsqrt(jnp.mean(hn * hn, axis=-1, keepdims=True) + EPS) * fg_ref[...]
    o_ref[...] = hn


def _outproj(ya, ob, oc, ym, gates, gb, gc, ones256, w, h, fg, tm, final):
    T, D = h.shape
    row = lambda i: (i, 0)
    const = lambda i: (0, 0)
    return pl.pallas_call(
        functools.partial(_outproj_kernel, final=final),
        grid=(T // tm,),
        in_specs=[
            pl.BlockSpec((tm, 256), row), pl.BlockSpec((tm, 256), row),
            pl.BlockSpec((tm, 256), row), pl.BlockSpec((tm, 256), row),
            pl.BlockSpec((tm, 512), lambda i: (i, 1)),
            pl.BlockSpec((1, 256), const), pl.BlockSpec((1, 256), const),
            pl.BlockSpec((256, 256), const),
            pl.BlockSpec((D, D), const),
            pl.BlockSpec((tm, D), row),
            pl.BlockSpec((1, D), const),
        ],
        out_specs=pl.BlockSpec((tm, D), row),
        out_shape=jax.ShapeDtypeStruct((T, D), F32),
        compiler_params=pltpu.CompilerParams(
            dimension_semantics=("arbitrary",), vmem_limit_bytes=VMEM_LIMIT),
        name="outproj_final" if final else "outproj",
    )(ya, ob, oc, ym, gates, gb, gc, ones256, w, h, fg)


def _block_ones(n, group):
    i = np.arange(n)
    return jnp.asarray((i[:, None] // group) == (i[None, :] // group), dtype=BF16)


def _head_perm_cols():
    return np.concatenate([np.arange(h * HEAD_DIM, (h + 1) * HEAD_DIM) for h in ATT_HEAD_PERM])


def _prep_w_in(w_in):
    o = np.cumsum([0, 256, 128, 128, 256, 768, 8, 8, 256, 128, 128, 256, 32, 256, 256, 256])
    (a_q, a_k, a_v, a_g, d_qkv, d_beta, d_alpha, d_gate, l_q, l_k, l_v, l_low, l_gate, m_q, m_gate) = [
        w_in[:, :, o[i]:o[i + 1]] for i in range(15)]
    perm = _head_perm_cols()
    pad = jnp.zeros(w_in.shape[:2] + (N_COLS - C_SM - 48,), w_in.dtype)
    cols = [a_q[:, :, perm], a_k, a_v, a_g[:, :, perm], d_qkv, d_gate, l_q, l_k, l_v, l_gate, m_q, m_gate,
            d_beta, d_alpha, l_low, pad]
    return jnp.concatenate(cols, axis=-1).astype(BF16)


def _rope_tables(seq):
    rows = seq // GRID_W
    row_pos = jnp.repeat(jnp.arange(rows, dtype=F32), GRID_W)
    col_pos = jnp.tile(jnp.arange(GRID_W, dtype=F32), rows)
    half = HEAD_DIM // 4
    inv_freq = ROPE_THETA ** (-jnp.arange(0, 2 * half, 2, dtype=F32) / (2 * half))
    ang = jnp.stack([row_pos, col_pos], axis=-1)[..., None] * inv_freq
    cos, sin = jnp.cos(ang), jnp.sin(ang)
    cos_h = jnp.concatenate([cos, cos], axis=-1).reshape(seq, HEAD_DIM)
    sin_h = jnp.concatenate([-sin, sin], axis=-1).reshape(seq, HEAD_DIM)
    return jnp.tile(cos_h, (1, 2)), jnp.tile(sin_h, (1, 2))


def _trunk(x, mem, p, consts):
    batch, seq, d_model = x.shape
    T = batch * seq
    depth = p["w_in"].shape[0]
    tm = 512 if seq % 512 == 0 else seq
    tq = 256 if seq % 256 == 0 else seq
    kvc = 512 if seq % 512 == 0 else seq
    cos_t, sin_t = _rope_tables(seq)
    mem2d = mem.reshape(batch * mem.shape[1], d_model)
    memkv = _memkv(mem2d, p["mem_norm_g"], p["w_mem_kv"], min(512, mem2d.shape[0]))
    h = x.reshape(T, d_model)
    for d in range(depth):
        qkv, gates, gdn_qkv, gla_in, mq, small, gdn_sm3 = _inproj(
            h, p["norm_g"][d], p["w_in"][d], cos_t, sin_t, p["qkg"][d], consts["ones384"],
            p["conv_w"][d], p["gdn_par"][d], consts["ones256"], seq, tm)
        ya, ym = _attention(qkv, mq, memkv[d], gates, batch, seq, tq, kvc)
        ob = _gdn(gdn_qkv, gdn_sm3, consts["gdn"], batch, seq)
        oc = _gla(gla_in, small, p["wup"][d], p["bg"][d], consts["gla"], batch, seq)
        h = _outproj(ya, ob, oc, ym, gates, p["gb"][d], p["gc"][d],
                     consts["ones256"], p["w_out"][d], h, p["final_g"], tm, final=(d == depth - 1))
    return h.reshape(batch, seq, d_model)


def kernel(x_prompt, x_sample, mem_prompt, mem_sample, norm_g, w_in, att_q_norm_g, att_k_norm_g, gdn_conv_w,
           gdn_a_log, gdn_dt_bias, gdn_out_norm_g, gla_w_gate_up, gla_b_gate, gla_out_norm_g, mem_norm_g,
           w_mem_kv, w_out, final_norm_g):
    depth, d_model = norm_g.shape
    perm = _head_perm_cols()
    w_out_p = jnp.concatenate([w_out[:, :256][:, perm], w_out[:, 256:]], axis=1).astype(BF16)
    qkg = jnp.concatenate([jnp.tile(att_q_norm_g, (1, N_HEADS)), jnp.tile(att_k_norm_g, (1, KV_HEADS))], axis=1)
    zeros8 = jnp.zeros((depth, N_COMBO), F32)
    par_a = jnp.concatenate([zeros8, -jnp.exp(gdn_a_log.reshape(depth, N_COMBO)),
                             jnp.zeros((depth, LANES - 2 * N_COMBO), F32)], axis=1)
    par_b = jnp.concatenate([zeros8, gdn_dt_bias.reshape(depth, N_COMBO),
                             jnp.zeros((depth, LANES - 2 * N_COMBO), F32)], axis=1)
    gdn_par = jnp.concatenate([par_a[:, None], par_b[:, None], jnp.zeros((depth, 6, LANES), F32)], axis=1)
    conv_w = jnp.concatenate([gdn_conv_w, jnp.zeros((depth, 5, gdn_conv_w.shape[2]), F32)], axis=1)
    wup = jnp.zeros((depth, N_DIRS, LANES, GLA_QW), F32)
    for d in range(N_DIRS):
        lo = 2 * N_COMBO + d * GLA_RANK
        wup = wup.at[:, d, lo:lo + GLA_RANK, :].set(gla_w_gate_up[:, d])
    p = {
        "norm_g": norm_g.reshape(depth, 1, d_model),
        "w_in": _prep_w_in(w_in),
        "qkg": qkg.reshape(depth, 1, 384),
        "conv_w": conv_w,
        "gdn_par": gdn_par,
        "wup": wup.astype(BF16),
        "bg": jnp.concatenate([gla_b_gate, jnp.zeros((depth, 6, GLA_QW), F32)], axis=1),
        "gb": jnp.tile(gdn_out_norm_g, (1, N_HEADS)).reshape(depth, 1, 256),
        "gc": jnp.tile(gla_out_norm_g, (1, N_HEADS)).reshape(depth, 1, 256),
        "mem_norm_g": mem_norm_g.reshape(depth, 1, d_model),
        "w_mem_kv": w_mem_kv.astype(BF16),
        "w_out": w_out_p,
        "final_g": final_norm_g.reshape(1, d_model),
    }
    tri_c, bd, expand = _gdn_constants()
    consts = {
        "ones384": _block_ones(384, HEAD_DIM),
        "ones256": _block_ones(256, HEAD_DIM),
        "gdn": (jnp.asarray(tri_c), jnp.asarray(bd), jnp.asarray(expand, dtype=BF16)),
        "gla": tuple(jnp.asarray(a) for a in _gla_constants()),
    }
    y_prompt = _trunk(x_prompt, mem_prompt, p, consts)
    y_sample = _trunk(x_sample, mem_sample, p, consts)
    return (y_prompt, y_sample)
```

```python
import functools

import numpy as np
import jax
import jax.numpy as jnp
from jax import lax
from jax.experimental import pallas as pl
from jax.experimental.pallas import tpu as pltpu

F32 = jnp.float32
BF16 = jnp.bfloat16

N_HEADS = 4
HEAD_DIM = 64
KV_HEADS = 2
GRID_W = 64
ROPE_THETA = 10000.0
CHUNK = 64
GLA_DK = 32
GLA_RANK = 16
GLA_NORMALIZER = 16.0
EPS = 1e-6
N_DIRS = 2
LANES = 128
GDN_GROUP = 2
GDN_WINDOW = 4
VMEM_LIMIT = 56 * 1024 * 1024

C_AQ, C_AK, C_AV, C_AG = 0, 256, 384, 512
C_DQKV, C_DG = 768, 1536
C_LQ, C_LK, C_LV, C_LG = 1792, 1920, 2048, 2304
C_MQ, C_MG = 2560, 2816
C_SM = 3072
N_COLS = 3200
ATT_HEAD_PERM = (0, 2, 1, 3)
ATT_Q_SCALE = HEAD_DIM ** -0.5 * float(np.log2(np.e))


def _bf(x):
    return x.astype(BF16)


def _mm(a, b):
    return jnp.dot(_bf(a), _bf(b), preferred_element_type=F32)


def _mm_nt(a, b):
    return lax.dot_general(_bf(a), _bf(b), (((1,), (1,)), ((), ())), preferred_element_type=F32)


def _mm_tn(a, b):
    return lax.dot_general(_bf(a), _bf(b), (((0,), (0,)), ((), ())), preferred_element_type=F32)


def _split2(x):
    hi = _bf(x)
    lo = _bf(x - hi.astype(F32))
    return hi, lo


def _split3(x):
    hi = _bf(x).astype(F32)
    r = x - hi
    mid = _bf(r).astype(F32)
    lo = _bf(r - mid).astype(F32)
    return hi, mid, lo


def _group_sums(xs, ones_blocks):
    rows = xs[0].shape[0]
    parts = []
    for x in xs:
        parts.extend(_split2(x))
    s = jnp.dot(jnp.concatenate(parts, axis=0), ones_blocks, preferred_element_type=F32)
    return [s[2 * i * rows:(2 * i + 1) * rows] + s[(2 * i + 1) * rows:(2 * i + 2) * rows] for i in range(len(xs))]


def _silu(x):
    return x * (0.5 * jnp.tanh(0.5 * x) + 0.5)


def _softplus(x):
    return jnp.maximum(x, 0.0) + jnp.log1p(jnp.exp(-jnp.abs(x)))


def _cumsum_rows(x, reverse):
    n = x.shape[0]
    row = lax.broadcasted_iota(jnp.int32, x.shape, 0)
    s = 1
    while s < n:
        if reverse:
            x = x + jnp.where(row < n - s, pltpu.roll(x, n - s, axis=0), 0.0)
        else:
            x = x + jnp.where(row >= s, pltpu.roll(x, s, axis=0), 0.0)
        s *= 2
    return x


def _block_diag(x, mask, reps):
    xb = _bf(x)
    t = jnp.concatenate([xb] * reps, axis=0)
    return jnp.where(mask, t, jnp.zeros_like(t))


def _inproj_kernel(h_ref, hprev_ref, hnext_ref, g_ref, w_ref, cos_ref, sin_ref, qkg_ref, ones_ref,
                   cw_ref, par_ref, ones256_ref,
                   qkv_ref, gates_ref, gdn_ref, gla_ref, mq_ref, small_ref, sm3_ref, *, nseq):
    tm = h_ref.shape[0]

    def normed(x):
        return _bf(x * lax.rsqrt(jnp.mean(x * x, axis=-1, keepdims=True) + EPS) * g_ref[...])

    halo = jnp.concatenate([hprev_ref[0], hnext_ref[0]], axis=0)
    hp = jnp.dot(normed(halo), w_ref[:, C_DQKV:C_DG], preferred_element_type=F32)
    xn = normed(h_ref[...])
    groups = ((C_DQKV, C_DG), (C_SM, N_COLS), (0, C_DQKV), (C_DG, C_SM))
    pieces = [jnp.dot(xn, w_ref[:, a:b], preferred_element_type=F32) for a, b in groups]

    class _Proj:
        def __getitem__(self, idx):
            rows, cols = idx
            for (a, b), piece in zip(groups, pieces):
                if a <= cols.start and cols.stop <= b:
                    return piece[rows, cols.start - a:cols.stop - a]
            raise IndexError(cols)

    proj = _Proj()

    pos = pl.program_id(0) % nseq
    prev_row = hp[7:8] * jnp.where(pos > 0, 1.0, 0.0)
    next_row = hp[8:9] * jnp.where(pos < nseq - 1, 1.0, 0.0)
    xg = proj[:, C_DQKV:C_DG]
    trow = lax.broadcasted_iota(jnp.int32, (tm, 1), 0)
    x_prev = jnp.where(trow == 0, prev_row, pltpu.roll(xg, 1, axis=0))
    x_next = jnp.where(trow == tm - 1, next_row, pltpu.roll(xg, tm - 1, axis=0))
    y = x_prev * cw_ref[0:1, :] + xg * cw_ref[1:2, :]
    y = _silu(y + x_next * cw_ref[2:3, :])
    sss = _group_sums([y[:, 0:256] * y[:, 0:256], y[:, 256:512] * y[:, 256:512]], ones256_ref[...])
    gdn_ref[:, 0:256] = _bf(y[:, 0:256] * lax.rsqrt(sss[0] + EPS) * (HEAD_DIM ** -0.5))
    gdn_ref[:, 256:512] = _bf(y[:, 256:512] * lax.rsqrt(sss[1] + EPS))
    gdn_ref[:, 512:768] = _bf(y[:, 512:768])

    lane = lax.broadcasted_iota(jnp.int32, (1, LANES), 1)
    lane8 = lane < N_COMBO
    fwd_lane = lane < N_HEADS
    for c in range(tm // CHUNK):
        sm = proj[c * CHUNK:(c + 1) * CHUNK, C_SM:N_COLS]
        beta = 0.5 * jnp.tanh(0.5 * sm) + 0.5
        g_all = par_ref[0:1, :] * _softplus(sm + par_ref[1:2, :])
        g = pltpu.roll(g_all, LANES - N_COMBO, axis=1)
        b = jnp.where(fwd_lane, _cumsum_rows(g, reverse=False), _cumsum_rows(g, reverse=True))
        zero = jnp.zeros_like(b)
        q2 = jnp.where(lane8, beta, zero) + pltpu.roll(jnp.where(lane8, b, zero), N_COMBO, axis=1)
        hi, mid, lo3 = _split3(q2)
        sm3_ref[c * CHUNK:(c + 1) * CHUNK, :] = _bf(
            hi + pltpu.roll(mid, 2 * N_COMBO, axis=1) + pltpu.roll(lo3, 4 * N_COMBO, axis=1))

    qk = proj[:, C_AQ:C_AV]
    ms = _group_sums([qk * qk], ones_ref[...])[0] * (1.0 / HEAD_DIM)
    qkn = qk * lax.rsqrt(ms + EPS) * qkg_ref[...]
    cos = cos_ref[...]
    sin = sin_ref[...]
    lane = lax.broadcasted_iota(jnp.int32, (1, LANES), 1)
    first_half = (lane % 32) < 16
    outs = []
    for c in range(3):
        xc = qkn[:, c * LANES:(c + 1) * LANES]
        partner = jnp.where(first_half, pltpu.roll(xc, LANES - 16, axis=1), pltpu.roll(xc, 16, axis=1))
        outs.append(xc * cos + partner * sin)
    qkv_ref[:, 0:128] = _bf(outs[0] * ATT_Q_SCALE)
    qkv_ref[:, 128:256] = _bf(outs[1] * ATT_Q_SCALE)
    qkv_ref[:, 256:384] = _bf(outs[2])
    qkv_ref[:, 384:512] = _bf(proj[:, C_AV:C_AG])

    gates_ref[:, 0:256] = _bf(_silu(proj[:, C_AG:C_AG + 256]))
    gates_ref[:, 256:512] = _bf(_silu(proj[:, C_MG:C_MG + 256]))
    gates_ref[:, 512:768] = _bf(_silu(proj[:, C_DG:C_DG + 256]))
    gates_ref[:, 768:1024] = _bf(_silu(proj[:, C_LG:C_LG + 256]))
    gla_ref[...] = _bf(proj[:, C_LQ:C_LG])
    mq_ref[...] = _bf(proj[:, C_MQ:C_MG] * ATT_Q_SCALE)
    small_ref[...] = proj[:, C_SM:N_COLS]


def _inproj(h, g, w, cos_t, sin_t, qkg, ones384, conv_w, par, ones256, seq, tm):
    T, D = h.shape
    nseq = seq // tm
    row = lambda i: (i, 0)
    const = lambda i: (0, 0)
    h8 = h.reshape(T // 8, 8, D)
    g8 = tm // 8
    return pl.pallas_call(
        functools.partial(_inproj_kernel, nseq=nseq),
        grid=(T // tm,),
        in_specs=[
            pl.BlockSpec((tm, D), row),
            pl.BlockSpec((1, 8, D), lambda i: (jnp.maximum(i * g8 - 1, 0), 0, 0)),
            pl.BlockSpec((1, 8, D), lambda i: (jnp.minimum((i + 1) * g8, T // 8 - 1), 0, 0)),
            pl.BlockSpec((1, D), const),
            pl.BlockSpec((D, N_COLS), const),
            pl.BlockSpec((tm, LANES), lambda i: (i % nseq, 0)),
            pl.BlockSpec((tm, LANES), lambda i: (i % nseq, 0)),
            pl.BlockSpec((1, 384), const),
            pl.BlockSpec((384, 384), const),
            pl.BlockSpec((8, 768), const),
            pl.BlockSpec((8, LANES), const),
            pl.BlockSpec((256, 256), const),
        ],
        out_specs=[
            pl.BlockSpec((tm, 512), row),
            pl.BlockSpec((tm, 1024), row),
            pl.BlockSpec((tm, 768), row),
            pl.BlockSpec((tm, 512), row),
            pl.BlockSpec((tm, 256), row),
            pl.BlockSpec((tm, LANES), row),
            pl.BlockSpec((tm, LANES), row),
        ],
        out_shape=[
            jax.ShapeDtypeStruct((T, 512), BF16),
            jax.ShapeDtypeStruct((T, 1024), BF16),
            jax.ShapeDtypeStruct((T, 768), BF16),
            jax.ShapeDtypeStruct((T, 512), BF16),
            jax.ShapeDtypeStruct((T, 256), BF16),
            jax.ShapeDtypeStruct((T, LANES), F32),
            jax.ShapeDtypeStruct((T, LANES), BF16),
        ],
        compiler_params=pltpu.CompilerParams(
            dimension_semantics=("arbitrary",), vmem_limit_bytes=VMEM_LIMIT),
        name="inproj",
    )(h, h8, h8, g, w, cos_t, sin_t, qkg, ones384, conv_w, par, ones256)


def _memkv_kernel(m_ref, g_ref, w_ref, o_ref):
    x = m_ref[...]
    xn = x * lax.rsqrt(jnp.mean(x * x, axis=-1, keepdims=True) + EPS) * g_ref[0]
    o_ref[0] = _bf(jnp.dot(_bf(xn), w_ref[0], preferred_element_type=F32))


def _memkv(mem2d, g, w, tr):
    R, D = mem2d.shape
    depth = w.shape[0]
    return pl.pallas_call(
        _memkv_kernel,
        grid=(depth, R // tr),
        in_specs=[
            pl.BlockSpec((tr, D), lambda d, i: (i, 0)),
            pl.BlockSpec((1, 1, D), lambda d, i: (d, 0, 0)),
            pl.BlockSpec((1, D, 512), lambda d, i: (d, 0, 0)),
        ],
        out_specs=pl.BlockSpec((1, tr, 512), lambda d, i: (d, i, 0)),
        out_shape=jax.ShapeDtypeStruct((depth, R, 512), BF16),
        compiler_params=pltpu.CompilerParams(
            dimension_semantics=("arbitrary", "arbitrary"), vmem_limit_bytes=VMEM_LIMIT),
        name="memkv",
    )(mem2d, g, w)


def _attn_kernel(q_ref, k_ref, v_ref, mq_ref, km_ref, vm_ref, ga_ref, gm_ref, ya_ref, ym_ref, vt_s, *, kvc):
    tq = q_ref.shape[0]
    seq = k_ref.shape[0]
    lane = lax.broadcasted_iota(jnp.int32, (1, LANES), 1)
    lo = lane < HEAD_DIM

    def stack_heads(x):
        z = jnp.zeros_like(x)
        return [jnp.where(lo, x, z), jnp.where(lo, z, x)]

    @pl.when(pl.program_id(1) == 0)
    def _():
        sub = lax.broadcasted_iota(jnp.int32, (LANES, 1), 0) < HEAD_DIM
        for c in range(seq // kvc):
            vt = jnp.transpose(v_ref[c * kvc:(c + 1) * kvc, :].astype(F32))
            vt_s[0, c] = _bf(jnp.where(sub, vt, 1.0))
            vt_s[1, c] = _bf(jnp.where(sub, 1.0, vt))

    q = q_ref[...]
    qm = stack_heads(q[:, :LANES]) + stack_heads(q[:, LANES:])
    n_heads = len(qm)

    def scores(c):
        kc = k_ref[c * kvc:(c + 1) * kvc, :]
        return [lax.dot_general(kc, qm[h], (((1,), (1,)), ((), ())), preferred_element_type=F32)
                for h in range(n_heads)]

    ms = [jnp.full((1, tq), -jnp.inf, F32) for _ in range(n_heads)]
    accs = [jnp.zeros((LANES, tq), F32) for _ in range(n_heads)]
    nsub = seq // kvc
    sts = scores(0)
    for c in range(nsub):
        nxt = scores(c + 1) if c + 1 < nsub else None
        new_ms = [jnp.maximum(ms[h], jnp.max(sts[h], axis=0, keepdims=True)) for h in range(n_heads)]
        ps = [_bf(jnp.exp2(sts[h] - new_ms[h])) for h in range(n_heads)]
        accs = [jnp.exp2(ms[h] - new_ms[h]) * accs[h]
                + jnp.dot(vt_s[h % 2, c], ps[h], preferred_element_type=F32) for h in range(n_heads)]
        ms, sts = new_ms, nxt
    o_t = []
    for h in range(n_heads):
        if h % 2 == 0:
            o_t.append(accs[h][0:HEAD_DIM] / accs[h][HEAD_DIM:HEAD_DIM + 1])
        else:
            o_t.append(accs[h][HEAD_DIM:LANES] / accs[h][0:1])
    ya = jnp.transpose(jnp.concatenate(o_t, axis=0))
    ya_ref[...] = _bf(ya * ga_ref[...].astype(F32))

    mq = mq_ref[...]
    outs = []
    for c in range(2):
        qc = jnp.concatenate(stack_heads(mq[:, c * LANES:(c + 1) * LANES]), axis=0)
        kc = km_ref[:, c * LANES:(c + 1) * LANES]
        vc = vm_ref[:, c * LANES:(c + 1) * LANES]
        s = lax.dot_general(qc, kc, (((1,), (1,)), ((), ())), preferred_element_type=F32)
        p = jnp.exp2(s - jnp.max(s, axis=-1, keepdims=True))
        om = jnp.dot(_bf(p), vc, preferred_element_type=F32) / jnp.sum(p, axis=-1, keepdims=True)
        outs.append(jnp.where(lo, om[0:tq], om[tq:2 * tq]))
    ym_ref[...] = _bf(jnp.concatenate(outs, axis=1) * gm_ref[...].astype(F32))


def _attention(qkv, mq, memkv, gates, batch, seq, tq, kvc):
    T = qkv.shape[0]
    nq = seq // tq
    mem_len = memkv.shape[0] // batch
    qrow = lambda b, i: (b * nq + i, 0)
    return pl.pallas_call(
        functools.partial(_attn_kernel, kvc=kvc),
        grid=(batch, nq),
        in_specs=[
            pl.BlockSpec((tq, 256), qrow),
            pl.BlockSpec((seq, LANES), lambda b, i: (b, 2)),
            pl.BlockSpec((seq, LANES), lambda b, i: (b, 3)),
            pl.BlockSpec((tq, 256), qrow),
            pl.BlockSpec((mem_len, 256), lambda b, i: (b, 0)),
            pl.BlockSpec((mem_len, 256), lambda b, i: (b, 1)),
            pl.BlockSpec((tq, 256), lambda b, i: (b * nq + i, 0)),
            pl.BlockSpec((tq, 256), lambda b, i: (b * nq + i, 1)),
        ],
        out_specs=[pl.BlockSpec((tq, 256), qrow), pl.BlockSpec((tq, 256), qrow)],
        out_shape=[jax.ShapeDtypeStruct((T, 256), BF16), jax.ShapeDtypeStruct((T, 256), BF16)],
        scratch_shapes=[pltpu.VMEM((KV_HEADS, seq // kvc, LANES, kvc), BF16)],
        compiler_params=pltpu.CompilerParams(
            dimension_semantics=("arbitrary", "arbitrary"), vmem_limit_bytes=VMEM_LIMIT),
        name="attention",
    )(qkv, qkv, qkv, mq, memkv, memkv, gates, gates)


GDN_W = GDN_GROUP * HEAD_DIM
GDN_NGRP = N_HEADS // GDN_GROUP
N_COMBO = N_DIRS * N_HEADS
N_QTY = 2


def _gdn_constants():
    w = GDN_W
    i = np.arange(CHUNK)[:, None]
    lane = np.arange(w)[None, :]
    j = lane % CHUNK
    tri = np.stack([i >= j, i <= j]).astype(np.float32)
    stri = np.stack([i > j, i < j]).astype(np.float32)
    eye = (i == j).astype(np.float32)[None]
    tri_c = np.concatenate([tri, stri, eye], axis=0)
    r = np.arange(w)[:, None]
    bd = ((r // CHUNK) == (lane // CHUNK)).astype(np.float32)
    expand = np.zeros((N_DIRS * GDN_NGRP, LANES, N_QTY * w), np.float32)
    for d in range(N_DIRS):
        for g in range(GDN_NGRP):
            for qty in range(N_QTY):
                for hh in range(GDN_GROUP):
                    combo = d * N_HEADS + g * GDN_GROUP + hh
                    for piece in range(3):
                        src = piece * (N_QTY * N_COMBO) + qty * N_COMBO + combo
                        expand[d * GDN_NGRP + g, src, qty * w + hh * CHUNK: qty * w + (hh + 1) * CHUNK] = 1.0
    return tri_c, bd, expand


def _gdn_kernel(qkv_ref, sm3_ref, tri_ref, bd_ref, exp_ref, o_ref, st_s, pa_s, pb_s, cd_s, acc_s):
    seq = qkv_ref.shape[0]
    nchunk = seq // CHUNK
    w = GDN_W
    acc_s[...] = jnp.zeros_like(acc_s)
    st_s[...] = jnp.zeros_like(st_s)
    bd_mask = bd_ref[...] > 0.5

    combos = [(d, grp) for d in range(N_DIRS) for grp in range(GDN_NGRP)]

    def load_chunk(chunk, d, grp):
        rs = pl.ds(pl.multiple_of(chunk * CHUNK, CHUNK), CHUNK)
        return (qkv_ref[rs, grp * w:(grp + 1) * w], qkv_ref[rs, 256 + grp * w:256 + (grp + 1) * w],
                qkv_ref[rs, 512 + grp * w:512 + (grp + 1) * w], sm3_ref[rs, :])

    nc = len(combos)
    head_masks = [(lax.broadcasted_iota(jnp.int32, (1, w), 1) // HEAD_DIM) == hh for hh in range(GDN_GROUP)]

    items = [(d, grp) for _ in range(GDN_WINDOW) for d, grp in combos]
    ni = len(items)

    def transform_stages(raws, outs):
        exs, diffs, kqs = [], [], []
        a_mats, ps, rs_ = [], [], []

        def stage_products():
            ex_all = [jnp.dot(jnp.concatenate([raws(t * nc + c)[3] for t in range(GDN_WINDOW)], axis=0),
                              exp_ref[c], preferred_element_type=F32) for c in range(nc)]
            for n, (d, grp) in enumerate(items):
                q, k, vb, _ = raws(n)
                ex = ex_all[n % nc][(n // nc) * CHUNK:(n // nc + 1) * CHUNK]
                betax, bx = ex[:, 0:w], ex[:, w:2 * w]
                ebx = jnp.exp(bx)
                b_end = bx[CHUNK - 1:CHUNK] if d == 0 else bx[0:1]
                exs.append([betax, betax * ebx, ebx, jnp.exp(b_end - bx)])
                diffs.append(bx - jnp.sum(bx * tri_ref[4], axis=0, keepdims=True))
                kbd = jnp.concatenate([jnp.where(hm, k, jnp.zeros_like(k)) for hm in head_masks], axis=0)
                kqs.append(lax.dot_general(jnp.concatenate([k, q], axis=0), kbd, (((1,), (1,)), ((), ())),
                                           preferred_element_type=F32))

        def stage_square():
            for n, (d, grp) in enumerate(items):
                tri = tri_ref[d] > 0.5
                stri = tri_ref[2 + d] > 0.5
                decay = jnp.where(tri, jnp.exp(jnp.where(tri, diffs[n], 0.0)), 0.0)
                low = jnp.where(stri, exs[n][0] * kqs[n][:CHUNK] * decay, 0.0)
                a_mats.append(kqs[n][CHUNK:] * decay)
                ps.append(_mm(low, _block_diag(low, bd_mask, GDN_GROUP)))
                rs_.append(tri_ref[4] - low)

        def stage_level(last):
            for n in range(ni):
                pbd = _block_diag(ps[n], bd_mask, GDN_GROUP)
                if last:
                    rs_[n] = rs_[n] + _mm(rs_[n], pbd)
                else:
                    rp = _mm(jnp.concatenate([rs_[n], ps[n]], axis=0), pbd)
                    rs_[n] = rs_[n] + rp[:CHUNK]
                    ps[n] = rp[CHUNK:]

        def stage_apply():
            for n, (d, grp) in enumerate(items):
                q, k, vb = raws(n)[:3]
                betax, bebx, ebx, edx = exs[n]
                kf = k.astype(F32)
                u = _mm(rs_[n], _block_diag(betax * vb.astype(F32), bd_mask, GDN_GROUP))
                wm = _mm(rs_[n], _block_diag(bebx * kf, bd_mask, GDN_GROUP))
                cd_row = ebx[CHUNK - 1:CHUNK] if d == 0 else ebx[0:1]
                outs.append((a_mats[n], u, _bf(wm), _bf(q.astype(F32) * ebx), _bf(kf * edx), cd_row))

        return ([stage_products, stage_square] + [functools.partial(stage_level, False)] * 4
                + [functools.partial(stage_level, True), stage_apply])

    def store_transformed(c, t):
        a_mat, u, wm, q_dec, k_dec, cd_row = t
        pa_s[2 * c] = a_mat
        pa_s[2 * c + 1] = u
        pb_s[3 * c] = wm
        pb_s[3 * c + 1] = q_dec
        pb_s[3 * c + 2] = k_dec
        cd_s[c] = jnp.broadcast_to(cd_row, (8, w))

    def load_transformed(c):
        return pa_s[2 * c], pa_s[2 * c + 1], pb_s[3 * c], pb_s[3 * c + 1], pb_s[3 * c + 2], cd_s[c][0:1]

    def scan_stages(states, cur, os_):
        tmp = {}

        def stage_ws(t):
            tmp["cur"] = [cur(t * nc + c) for c in range(nc)]
            tmp["ws"] = [_mm(jnp.concatenate([tmp["cur"][c][2], tmp["cur"][c][3]], axis=0), states[c])
                         for c in range(nc)]

        def stage_update(t):
            for c in range(nc):
                a_mat, u, _, _, k_dec, cd_row = tmp["cur"][c]
                v_new = u - tmp["ws"][c][:CHUNK]
                os_.append(tmp["ws"][c][CHUNK:] + _mm(a_mat, _block_diag(v_new, bd_mask, GDN_GROUP)))
                states[c] = cd_row * states[c] + jnp.where(bd_mask, _mm_tn(k_dec, v_new), 0.0)

        stages = []
        for t in range(GDN_WINDOW):
            stages += [functools.partial(stage_ws, t), functools.partial(stage_update, t)]
        return stages

    def chunk_of(j, t, d):
        i = j * GDN_WINDOW + t
        return i if d == 0 else nchunk - 1 - i

    def window_loader(j):
        def load(n):
            d, grp = items[n]
            return load_chunk(chunk_of(j, n // nc, d), d, grp)
        return load

    first = []
    for stage in transform_stages(window_loader(0), first):
        stage()
    for n in range(ni):
        store_transformed(n, first[n])

    nwin = nchunk // GDN_WINDOW

    def step(j, carry):
        states = [st_s[c] for c in range(nc)]
        os_, nexts = [], []
        t_stages = transform_stages(window_loader(jnp.minimum(j + 1, nwin - 1)), nexts)
        s_stages = scan_stages(states, load_transformed, os_)
        for k in range(max(len(t_stages), len(s_stages))):
            if k < len(t_stages):
                t_stages[k]()
            if k < len(s_stages):
                s_stages[k]()
        for t in range(GDN_WINDOW):
            for c, (d, grp) in enumerate(combos):
                rs = pl.ds(pl.multiple_of(chunk_of(j, t, d) * CHUNK, CHUNK), CHUNK)
                acc_s[rs, grp * w:(grp + 1) * w] += os_[t * nc + c]
        for c in range(nc):
            st_s[c] = states[c]
        for n in range(ni):
            store_transformed(n, nexts[n])
        return carry

    lax.fori_loop(0, nwin, step, 0)
    o_ref[...] = _bf(acc_s[...])


def _gdn(gdn_qkv, gdn_sm3, consts, batch, seq):
    T = gdn_qkv.shape[0]
    assert seq % (CHUNK * GDN_WINDOW) == 0
    tri_c, bd, expand = consts
    whole = lambda b: (b, 0)
    c2 = lambda b: (0, 0)
    c3 = lambda b: (0, 0, 0)
    return pl.pallas_call(
        _gdn_kernel,
        grid=(batch,),
        in_specs=[
            pl.BlockSpec((seq, 768), whole),
            pl.BlockSpec((seq, LANES), whole),
            pl.BlockSpec(tri_c.shape, c3),
            pl.BlockSpec(bd.shape, c2),
            pl.BlockSpec(expand.shape, c3),
        ],
        out_specs=pl.BlockSpec((seq, 256), whole),
        out_shape=jax.ShapeDtypeStruct((T, 256), BF16),
        scratch_shapes=[
            pltpu.VMEM((N_DIRS * GDN_NGRP, GDN_W, GDN_W), F32),
            pltpu.VMEM((2 * GDN_WINDOW * N_DIRS * GDN_NGRP, CHUNK, GDN_W), F32),
            pltpu.VMEM((3 * GDN_WINDOW * N_DIRS * GDN_NGRP, CHUNK, GDN_W), BF16),
            pltpu.VMEM((GDN_WINDOW * N_DIRS * GDN_NGRP, 8, GDN_W), F32),
            pltpu.VMEM((seq, 256), F32),
        ],
        compiler_params=pltpu.CompilerParams(
            dimension_semantics=("arbitrary",), vmem_limit_bytes=VMEM_LIMIT),
        name="gdn",
    )(gdn_qkv, gdn_sm3, tri_c, bd, expand)


GLA_QW = N_HEADS * GLA_DK
GLA_VW = N_HEADS * HEAD_DIM
GLA_WINDOW = 4


def _gla_constants():
    i = np.arange(CHUNK)[:, None]
    lane = np.arange(GLA_VW)[None, :]
    j = lane % CHUNK
    tri = np.stack([i >= j, i <= j]).astype(np.float32)
    r = np.arange(GLA_VW)[:, None]
    bd_v = ((r // CHUNK) == (lane // HEAD_DIM)).astype(np.float32)
    rs = np.arange(GLA_QW)[:, None]
    st_mask = ((rs // GLA_DK) == (lane // HEAD_DIM)).astype(np.float32)
    kq_mask = ((r // CHUNK) == (np.arange(GLA_QW)[None, :] // GLA_DK)).astype(np.float32)
    return tri, bd_v, st_mask, kq_mask


def _gla_kernel(x_ref, sm_ref, wup_ref, bg_ref, tri_ref, bdv_ref, stm_ref, kqm_ref, o_ref, st_s, acc_s):
    seq = x_ref.shape[0]
    nchunk = seq // CHUNK
    acc_s[...] = jnp.zeros_like(acc_s)
    st_s[...] = jnp.zeros_like(st_s)
    bdv_mask = bdv_ref[...] > 0.5
    st_mask = stm_ref[...] > 0.5
    kq_mask = kqm_ref[...] > 0.5

    def rows_of(chunk):
        return pl.ds(pl.multiple_of(chunk * CHUNK, CHUNK), CHUNK)

    def load_chunk(chunk):
        rs = rows_of(chunk)
        return (x_ref[rs, 0:GLA_QW].astype(F32), x_ref[rs, GLA_QW:2 * GLA_QW].astype(F32),
                x_ref[rs, 2 * GLA_QW:2 * GLA_QW + GLA_VW], sm_ref[rs, :])

    items = [(t, d) for t in range(GLA_WINDOW) for d in range(N_DIRS)]
    ni = len(items)

    def chunk_of(jw, t, d):
        i = jw * GLA_WINDOW + t
        return i if d == 0 else nchunk - 1 - i

    def step(jw, carry):
        raws = [load_chunk(chunk_of(jw, t, d)) for t, d in items]
        states = [st_s[d] for d in range(N_DIRS)]
        logits = [jnp.dot(_bf(raws[n][3]), wup_ref[d], preferred_element_type=F32) + bg_ref[d:d + 1, :]
                  for n, (t, d) in enumerate(items)]
        q_es, attns, dsts, cds = [], [], [], []
        for n, (t, d) in enumerate(items):
            q, k, v, _ = raws[n]
            gk = (jnp.minimum(logits[n], 0.0) - jnp.log1p(jnp.exp(-jnp.abs(logits[n])))) * (1.0 / GLA_NORMALIZER)
            b = _cumsum_rows(gk, reverse=(d == 1))
            b_end = b[CHUNK - 1:CHUNK] if d == 0 else b[0:1]
            q_e = q * (GLA_DK ** -0.5) * jnp.exp(b)
            keb = _bf(k * jnp.exp(-b))
            kebd = jnp.where(kq_mask, jnp.concatenate([keb] * N_HEADS, axis=0), jnp.zeros((GLA_VW, GLA_QW), BF16))
            q_es.append(_bf(q_e))
            attns.append(jnp.where(tri_ref[d] > 0.5, _mm_nt(q_es[n], kebd), 0.0))
            dsts.append(jnp.where(st_mask, _mm_tn(k * jnp.exp(b_end - b), v), 0.0))
            cd_col = jnp.transpose(jnp.broadcast_to(jnp.exp(b_end), (GLA_QW, GLA_QW)))
            cds.append(jnp.concatenate([cd_col, cd_col], axis=1))
        outs = [_mm(attns[n], _block_diag(raws[n][2], bdv_mask, N_HEADS)) for n in range(ni)]
        for n, (t, d) in enumerate(items):
            outs[n] = outs[n] + _mm(q_es[n], states[d])
            states[d] = cds[n] * states[d] + dsts[n]
        for n, (t, d) in enumerate(items):
            acc_s[rows_of(chunk_of(jw, t, d)), :] += outs[n]
        for d in range(N_DIRS):
            st_s[d] = states[d]
        return carry

    lax.fori_loop(0, nchunk // GLA_WINDOW, step, 0)
    o_ref[...] = _bf(acc_s[...])


def _gla(gla_in, small, wup, bg, consts, batch, seq):
    T = gla_in.shape[0]
    tri, bd_v, st_mask, kq_mask = consts
    whole = lambda b: (b, 0)
    c2 = lambda b: (0, 0)
    c3 = lambda b: (0, 0, 0)
    return pl.pallas_call(
        _gla_kernel,
        grid=(batch,),
        in_specs=[
            pl.BlockSpec((seq, 512), whole),
            pl.BlockSpec((seq, LANES), whole),
            pl.BlockSpec(wup.shape, c3),
            pl.BlockSpec(bg.shape, c2),
            pl.BlockSpec(tri.shape, c3),
            pl.BlockSpec(bd_v.shape, c2),
            pl.BlockSpec(st_mask.shape, c2),
            pl.BlockSpec(kq_mask.shape, c2),
        ],
        out_specs=pl.BlockSpec((seq, 256), whole),
        out_shape=jax.ShapeDtypeStruct((T, 256), BF16),
        scratch_shapes=[pltpu.VMEM((N_DIRS, GLA_QW, GLA_VW), F32), pltpu.VMEM((seq, 256), F32)],
        compiler_params=pltpu.CompilerParams(
            dimension_semantics=("arbitrary",), vmem_limit_bytes=VMEM_LIMIT),
        name="gla",
    )(gla_in, small, wup, bg, tri, bd_v, st_mask, kq_mask)


def _outproj_kernel(ya_ref, ob_ref, oc_ref, ym_ref, gates_ref, gb_ref, gc_ref, ones_ref, w_ref, h_ref,
                    fg_ref, o_ref, *, final):
    ones256 = ones_ref[...]

    ob = ob_ref[...].astype(F32)
    oc = oc_ref[...].astype(F32)
    ssb, ssc = _group_sums([ob * ob, oc * oc], ones256)
    yb = ob * lax.rsqrt(ssb * (1.0 / HEAD_DIM) + EPS) * gb_ref[...] * gates_ref[:, 0:256].astype(F32)
    yc = oc * lax.rsqrt(ssc * (1.0 / HEAD_DIM) + EPS) * gc_ref[...] * gates_ref[:, 256:512].astype(F32)
    y = jnp.dot(ya_ref[...], w_ref[0:256, :], preferred_element_type=F32)
    y = y + jnp.dot(_bf(yb), w_ref[256:512, :], preferred_element_type=F32)
    y = y + jnp.dot(_bf(yc), w_ref[512:768, :], preferred_element_type=F32)
    y = y + jnp.dot(ym_ref[...], w_ref[768:1024, :], preferred_element_type=F32)
    hn = h_ref[...] + y
    if final:
        hn = hn * lax.rsqrt(jnp.mean(hn * hn, axis=-1, keepdims=True) + EPS) * fg_ref[...]
    o_ref[...] = hn


def _outproj(ya, ob, oc, ym, gates, gb, gc, ones256, w, h, fg, tm, final):
    T, D = h.shape
    row = lambda i: (i, 0)
    const = lambda i: (0, 0)
    return pl.pallas_call(
        functools.partial(_outproj_kernel, final=final),
        grid=(T // tm,),
        in_specs=[
            pl.BlockSpec((tm, 256), row), pl.BlockSpec((tm, 256), row),
            pl.BlockSpec((tm, 256), row), pl.BlockSpec((tm, 256), row),
            pl.BlockSpec((tm, 512), lambda i: (i, 1)),
            pl.BlockSpec((1, 256), const), pl.BlockSpec((1, 256), const),
            pl.BlockSpec((256, 256), const),
            pl.BlockSpec((D, D), const),
            pl.BlockSpec((tm, D), row),
            pl.BlockSpec((1, D), const),
        ],
        out_specs=pl.BlockSpec((tm, D), row),
        out_shape=jax.ShapeDtypeStruct((T, D), F32),
        compiler_params=pltpu.CompilerParams(
            dimension_semantics=("arbitrary",), vmem_limit_bytes=VMEM_LIMIT),
        name="outproj_final" if final else "outproj",
    )(ya, ob, oc, ym, gates, gb, gc, ones256, w, h, fg)


def _block_ones(n, group):
    i = np.arange(n)
    return jnp.asarray((i[:, None] // group) == (i[None, :] // group), dtype=BF16)


def _head_perm_cols():
    return np.concatenate([np.arange(h * HEAD_DIM, (h + 1) * HEAD_DIM) for h in ATT_HEAD_PERM])


def _prep_w_in(w_in):
    o = np.cumsum([0, 256, 128, 128, 256, 768, 8, 8, 256, 128, 128, 256, 32, 256, 256, 256])
    (a_q, a_k, a_v, a_g, d_qkv, d_beta, d_alpha, d_gate, l_q, l_k, l_v, l_low, l_gate, m_q, m_gate) = [
        w_in[:, :, o[i]:o[i + 1]] for i in range(15)]
    perm = _head_perm_cols()
    pad = jnp.zeros(w_in.shape[:2] + (N_COLS - C_SM - 48,), w_in.dtype)
    cols = [a_q[:, :, perm], a_k, a_v, a_g[:, :, perm], d_qkv, d_gate, l_q, l_k, l_v, l_gate, m_q, m_gate,
            d_beta, d_alpha, l_low, pad]
    return jnp.concatenate(cols, axis=-1).astype(BF16)


def _rope_tables(seq):
    rows = seq // GRID_W
    row_pos = jnp.repeat(jnp.arange(rows, dtype=F32), GRID_W)
    col_pos = jnp.tile(jnp.arange(GRID_W, dtype=F32), rows)
    half = HEAD_DIM // 4
    inv_freq = ROPE_THETA ** (-jnp.arange(0, 2 * half, 2, dtype=F32) / (2 * half))
    ang = jnp.stack([row_pos, col_pos], axis=-1)[..., None] * inv_freq
    cos, sin = jnp.cos(ang), jnp.sin(ang)
    cos_h = jnp.concatenate([cos, cos], axis=-1).reshape(seq, HEAD_DIM)
    sin_h = jnp.concatenate([-sin, sin], axis=-1).reshape(seq, HEAD_DIM)
    return jnp.tile(cos_h, (1, 2)), jnp.tile(sin_h, (1, 2))


def _trunk(x, mem, p, consts):
    batch, seq, d_model = x.shape
    T = batch * seq
    depth = p["w_in"].shape[0]
    tm = 1024 if seq % 1024 == 0 else (512 if seq % 512 == 0 else seq)
    tq = 256 if seq % 256 == 0 else seq
    kvc = 512 if seq % 512 == 0 else seq
    cos_t, sin_t = _rope_tables(seq)
    mem2d = mem.reshape(batch * mem.shape[1], d_model)
    memkv = _memkv(mem2d, p["mem_norm_g"], p["w_mem_kv"], min(512, mem2d.shape[0]))
    h = x.reshape(T, d_model)
    for d in range(depth):
        qkv, gates, gdn_qkv, gla_in, mq, small, gdn_sm3 = _inproj(
            h, p["norm_g"][d], p["w_in"][d], cos_t, sin_t, p["qkg"][d], consts["ones384"],
            p["conv_w"][d], p["gdn_par"][d], consts["ones256"], seq, tm)
        ya, ym = _attention(qkv, mq, memkv[d], gates, batch, seq, tq, kvc)
        ob = _gdn(gdn_qkv, gdn_sm3, consts["gdn"], batch, seq)
        oc = _gla(gla_in, small, p["wup"][d], p["bg"][d], consts["gla"], batch, seq)
        h = _outproj(ya, ob, oc, ym, gates, p["gb"][d], p["gc"][d],
                     consts["ones256"], p["w_out"][d], h, p["final_g"], tm, final=(d == depth - 1))
    return h.reshape(batch, seq, d_model)


def kernel(x_prompt, x_sample, mem_prompt, mem_sample, norm_g, w_in, att_q_norm_g, att_k_norm_g, gdn_conv_w,
           gdn_a_log, gdn_dt_bias, gdn_out_norm_g, gla_w_gate_up, gla_b_gate, gla_out_norm_g, mem_norm_g,
           w_mem_kv, w_out, final_norm_g):
    depth, d_model = norm_g.shape
    perm = _head_perm_cols()
    w_out_p = jnp.concatenate([w_out[:, :256][:, perm], w_out[:, 256:]], axis=1).astype(BF16)
    qkg = jnp.concatenate([jnp.tile(att_q_norm_g, (1, N_HEADS)), jnp.tile(att_k_norm_g, (1, KV_HEADS))], axis=1)
    zeros8 = jnp.zeros((depth, N_COMBO), F32)
    par_a = jnp.concatenate([zeros8, -jnp.exp(gdn_a_log.reshape(depth, N_COMBO)),
                             jnp.zeros((depth, LANES - 2 * N_COMBO), F32)], axis=1)
    par_b = jnp.concatenate([zeros8, gdn_dt_bias.reshape(depth, N_COMBO),
                             jnp.zeros((depth, LANES - 2 * N_COMBO), F32)], axis=1)
    gdn_par = jnp.concatenate([par_a[:, None], par_b[:, None], jnp.zeros((depth, 6, LANES), F32)], axis=1)
    conv_w = jnp.concatenate([gdn_conv_w, jnp.zeros((depth, 5, gdn_conv_w.shape[2]), F32)], axis=1)
    wup = jnp.zeros((depth, N_DIRS, LANES, GLA_QW), F32)
    for d in range(N_DIRS):
        lo = 2 * N_COMBO + d * GLA_RANK
        wup = wup.at[:, d, lo:lo + GLA_RANK, :].set(gla_w_gate_up[:, d])
    p = {
        "norm_g": norm_g.reshape(depth, 1, d_model),
        "w_in": _prep_w_in(w_in),
        "qkg": qkg.reshape(depth, 1, 384),
        "conv_w": conv_w,
        "gdn_par": gdn_par,
        "wup": wup.astype(BF16),
        "bg": jnp.concatenate([gla_b_gate, jnp.zeros((depth, 6, GLA_QW), F32)], axis=1),
        "gb": jnp.tile(gdn_out_norm_g, (1, N_HEADS)).reshape(depth, 1, 256),
        "gc": jnp.tile(gla_out_norm_g, (1, N_HEADS)).reshape(depth, 1, 256),
        "mem_norm_g": mem_norm_g.reshape(depth, 1, d_model),
        "w_mem_kv": w_mem_kv.astype(BF16),
        "w_out": w_out_p,
        "final_g": final_norm_g.reshape(1, d_model),
    }
    tri_c, bd, expand = _gdn_constants()
    consts = {
        "ones384": _block_ones(384, HEAD_DIM),
        "ones256": _block_ones(256, HEAD_DIM),
        "gdn": (jnp.asarray(tri_c), jnp.asarray(bd), jnp.asarray(expand, dtype=BF16)),
        "gla": tuple(jnp.asarray(a) for a in _gla_constants()),
    }
    y_prompt = _trunk(x_prompt, mem_prompt, p, consts)
    y_sample = _trunk(x_sample, mem_sample, p, consts)
    return (y_prompt, y_sample)
```

```python
import functools

import numpy as np
import jax
import jax.numpy as jnp
from jax import lax
from jax.experimental import pallas as pl
from jax.experimental.pallas import tpu as pltpu

F32 = jnp.float32
BF16 = jnp.bfloat16

N_HEADS = 4
HEAD_DIM = 64
KV_HEADS = 2
GRID_W = 64
ROPE_THETA = 10000.0
CHUNK = 64
GLA_DK = 32
GLA_RANK = 16
GLA_NORMALIZER = 16.0
EPS = 1e-6
N_DIRS = 2
LANES = 128
GDN_GROUP = 2
GDN_WINDOW = 4
VMEM_LIMIT = 56 * 1024 * 1024

C_AQ, C_AK, C_AV, C_AG = 0, 256, 384, 512
C_DQKV, C_DG = 768, 1536
C_LQ, C_LK, C_LV, C_LG = 1792, 1920, 2048, 2304
C_MQ, C_MG = 2560, 2816
C_SM = 3072
N_COLS = 3200
ATT_HEAD_PERM = (0, 2, 1, 3)
ATT_Q_SCALE = HEAD_DIM ** -0.5 * float(np.log2(np.e))


def _bf(x):
    return x.astype(BF16)


def _mm(a, b):
    return jnp.dot(_bf(a), _bf(b), preferred_element_type=F32)


def _mm_nt(a, b):
    return lax.dot_general(_bf(a), _bf(b), (((1,), (1,)), ((), ())), preferred_element_type=F32)


def _mm_tn(a, b):
    return lax.dot_general(_bf(a), _bf(b), (((0,), (0,)), ((), ())), preferred_element_type=F32)


def _split2(x):
    hi = _bf(x)
    lo = _bf(x - hi.astype(F32))
    return hi, lo


def _split3(x):
    hi = _bf(x).astype(F32)
    r = x - hi
    mid = _bf(r).astype(F32)
    lo = _bf(r - mid).astype(F32)
    return hi, mid, lo


def _group_sums(xs, ones_blocks):
    rows = xs[0].shape[0]
    parts = []
    for x in xs:
        parts.extend(_split2(x))
    s = jnp.dot(jnp.concatenate(parts, axis=0), ones_blocks, preferred_element_type=F32)
    return [s[2 * i * rows:(2 * i + 1) * rows] + s[(2 * i + 1) * rows:(2 * i + 2) * rows] for i in range(len(xs))]


def _silu(x):
    return x * (0.5 * jnp.tanh(0.5 * x) + 0.5)


def _softplus(x):
    return jnp.maximum(x, 0.0) + jnp.log1p(jnp.exp(-jnp.abs(x)))


def _cumsum_rows(x, reverse):
    n = x.shape[0]
    row = lax.broadcasted_iota(jnp.int32, x.shape, 0)
    s = 1
    while s < n:
        if reverse:
            x = x + jnp.where(row < n - s, pltpu.roll(x, n - s, axis=0), 0.0)
        else:
            x = x + jnp.where(row >= s, pltpu.roll(x, s, axis=0), 0.0)
        s *= 2
    return x


def _block_diag(x, mask, reps):
    xb = _bf(x)
    t = jnp.concatenate([xb] * reps, axis=0)
    return jnp.where(mask, t, jnp.zeros_like(t))


def _inproj_kernel(h_ref, hprev_ref, hnext_ref, g_ref, w_ref, cos_ref, sin_ref, qkg_ref, ones_ref,
                   cw_ref, par_ref, ones256_ref,
                   qkv_ref, gates_ref, gdn_ref, gla_ref, mq_ref, small_ref, sm3_ref, *, nseq):
    tm = h_ref.shape[0]

    def normed(x):
        return _bf(x * lax.rsqrt(jnp.mean(x * x, axis=-1, keepdims=True) + EPS) * g_ref[...])

    halo = jnp.concatenate([hprev_ref[0], hnext_ref[0]], axis=0)
    hp = jnp.dot(normed(halo), w_ref[:, C_DQKV:C_DG], preferred_element_type=F32)
    xn = normed(h_ref[...])
    groups = ((C_DQKV, C_DG), (C_SM, N_COLS), (0, C_DQKV), (C_DG, C_SM))
    pieces = [jnp.dot(xn, w_ref[:, a:b], preferred_element_type=F32) for a, b in groups]

    class _Proj:
        def __getitem__(self, idx):
            rows, cols = idx
            for (a, b), piece in zip(groups, pieces):
                if a <= cols.start and cols.stop <= b:
                    return piece[rows, cols.start - a:cols.stop - a]
            raise IndexError(cols)

    proj = _Proj()

    pos = pl.program_id(0) % nseq
    prev_row = hp[7:8] * jnp.where(pos > 0, 1.0, 0.0)
    next_row = hp[8:9] * jnp.where(pos < nseq - 1, 1.0, 0.0)
    xg = proj[:, C_DQKV:C_DG]
    trow = lax.broadcasted_iota(jnp.int32, (tm, 1), 0)
    x_prev = jnp.where(trow == 0, prev_row, pltpu.roll(xg, 1, axis=0))
    x_next = jnp.where(trow == tm - 1, next_row, pltpu.roll(xg, tm - 1, axis=0))
    y = x_prev * cw_ref[0:1, :] + xg * cw_ref[1:2, :]
    y = _silu(y + x_next * cw_ref[2:3, :])
    sss = _group_sums([y[:, 0:256] * y[:, 0:256], y[:, 256:512] * y[:, 256:512]], ones256_ref[...])
    gdn_ref[:, 0:256] = _bf(y[:, 0:256] * lax.rsqrt(sss[0] + EPS) * (HEAD_DIM ** -0.5))
    gdn_ref[:, 256:512] = _bf(y[:, 256:512] * lax.rsqrt(sss[1] + EPS))
    gdn_ref[:, 512:768] = _bf(y[:, 512:768])

    lane = lax.broadcasted_iota(jnp.int32, (1, LANES), 1)
    lane8 = lane < N_COMBO
    fwd_lane = lane < N_HEADS
    for c in range(tm // CHUNK):
        sm = proj[c * CHUNK:(c + 1) * CHUNK, C_SM:N_COLS]
        beta = 0.5 * jnp.tanh(0.5 * sm) + 0.5
        g_all = par_ref[0:1, :] * _softplus(sm + par_ref[1:2, :])
        g = pltpu.roll(g_all, LANES - N_COMBO, axis=1)
        b = jnp.where(fwd_lane, _cumsum_rows(g, reverse=False), _cumsum_rows(g, reverse=True))
        zero = jnp.zeros_like(b)
        q2 = jnp.where(lane8, beta, zero) + pltpu.roll(jnp.where(lane8, b, zero), N_COMBO, axis=1)
        hi, mid, lo3 = _split3(q2)
        sm3_ref[c * CHUNK:(c + 1) * CHUNK, :] = _bf(
            hi + pltpu.roll(mid, 2 * N_COMBO, axis=1) + pltpu.roll(lo3, 4 * N_COMBO, axis=1))

    qk = proj[:, C_AQ:C_AV]
    ms = _group_sums([qk * qk], ones_ref[...])[0] * (1.0 / HEAD_DIM)
    qkn = qk * lax.rsqrt(ms + EPS) * qkg_ref[...]
    cos = cos_ref[...]
    sin = sin_ref[...]
    lane = lax.broadcasted_iota(jnp.int32, (1, LANES), 1)
    first_half = (lane % 32) < 16
    outs = []
    for c in range(3):
        xc = qkn[:, c * LANES:(c + 1) * LANES]
        partner = jnp.where(first_half, pltpu.roll(xc, LANES - 16, axis=1), pltpu.roll(xc, 16, axis=1))
        outs.append(xc * cos + partner * sin)
    qkv_ref[:, 0:128] = _bf(outs[0] * ATT_Q_SCALE)
    qkv_ref[:, 128:256] = _bf(outs[1] * ATT_Q_SCALE)
    qkv_ref[:, 256:384] = _bf(outs[2])
    qkv_ref[:, 384:512] = _bf(proj[:, C_AV:C_AG])

    gates_ref[:, 0:256] = _bf(_silu(proj[:, C_AG:C_AG + 256]))
    gates_ref[:, 256:512] = _bf(_silu(proj[:, C_MG:C_MG + 256]))
    gates_ref[:, 512:768] = _bf(_silu(proj[:, C_DG:C_DG + 256]))
    gates_ref[:, 768:1024] = _bf(_silu(proj[:, C_LG:C_LG + 256]))
    gla_ref[...] = _bf(proj[:, C_LQ:C_LG])
    mq_ref[...] = _bf(proj[:, C_MQ:C_MG] * ATT_Q_SCALE)
    small_ref[...] = proj[:, C_SM:N_COLS]


def _inproj(h, g, w, cos_t, sin_t, qkg, ones384, conv_w, par, ones256, seq, tm):
    T, D = h.shape
    nseq = seq // tm
    row = lambda i: (i, 0)
    const = lambda i: (0, 0)
    h8 = h.reshape(T // 8, 8, D)
    g8 = tm // 8
    return pl.pallas_call(
        functools.partial(_inproj_kernel, nseq=nseq),
        grid=(T // tm,),
        in_specs=[
            pl.BlockSpec((tm, D), row),
            pl.BlockSpec((1, 8, D), lambda i: (jnp.maximum(i * g8 - 1, 0), 0, 0)),
            pl.BlockSpec((1, 8, D), lambda i: (jnp.minimum((i + 1) * g8, T // 8 - 1), 0, 0)),
            pl.BlockSpec((1, D), const),
            pl.BlockSpec((D, N_COLS), const),
            pl.BlockSpec((tm, LANES), lambda i: (i % nseq, 0)),
            pl.BlockSpec((tm, LANES), lambda i: (i % nseq, 0)),
            pl.BlockSpec((1, 384), const),
            pl.BlockSpec((384, 384), const),
            pl.BlockSpec((8, 768), const),
            pl.BlockSpec((8, LANES), const),
            pl.BlockSpec((256, 256), const),
        ],
        out_specs=[
            pl.BlockSpec((tm, 512), row),
            pl.BlockSpec((tm, 1024), row),
            pl.BlockSpec((tm, 768), row),
            pl.BlockSpec((tm, 512), row),
            pl.BlockSpec((tm, 256), row),
            pl.BlockSpec((tm, LANES), row),
            pl.BlockSpec((tm, LANES), row),
        ],
        out_shape=[
            jax.ShapeDtypeStruct((T, 512), BF16),
            jax.ShapeDtypeStruct((T, 1024), BF16),
            jax.ShapeDtypeStruct((T, 768), BF16),
            jax.ShapeDtypeStruct((T, 512), BF16),
            jax.ShapeDtypeStruct((T, 256), BF16),
            jax.ShapeDtypeStruct((T, LANES), F32),
            jax.ShapeDtypeStruct((T, LANES), BF16),
        ],
        compiler_params=pltpu.CompilerParams(
            dimension_semantics=("arbitrary",), vmem_limit_bytes=VMEM_LIMIT),
        name="inproj",
    )(h, h8, h8, g, w, cos_t, sin_t, qkg, ones384, conv_w, par, ones256)


def _memkv_kernel(m_ref, g_ref, w_ref, o_ref):
    x = m_ref[...]
    xn = x * lax.rsqrt(jnp.mean(x * x, axis=-1, keepdims=True) + EPS) * g_ref[0]
    o_ref[0] = _bf(jnp.dot(_bf(xn), w_ref[0], preferred_element_type=F32))


def _memkv(mem2d, g, w, tr):
    R, D = mem2d.shape
    depth = w.shape[0]
    return pl.pallas_call(
        _memkv_kernel,
        grid=(depth, R // tr),
        in_specs=[
            pl.BlockSpec((tr, D), lambda d, i: (i, 0)),
            pl.BlockSpec((1, 1, D), lambda d, i: (d, 0, 0)),
            pl.BlockSpec((1, D, 512), lambda d, i: (d, 0, 0)),
        ],
        out_specs=pl.BlockSpec((1, tr, 512), lambda d, i: (d, i, 0)),
        out_shape=jax.ShapeDtypeStruct((depth, R, 512), BF16),
        compiler_params=pltpu.CompilerParams(
            dimension_semantics=("arbitrary", "arbitrary"), vmem_limit_bytes=VMEM_LIMIT),
        name="memkv",
    )(mem2d, g, w)


def _attn_kernel(q_ref, k_ref, v_ref, mq_ref, km_ref, vm_ref, ga_ref, gm_ref, ya_ref, ym_ref, vt_s, *, kvc):
    tq = q_ref.shape[0]
    seq = k_ref.shape[0]
    lane = lax.broadcasted_iota(jnp.int32, (1, LANES), 1)
    lo = lane < HEAD_DIM

    def stack_heads(x):
        z = jnp.zeros_like(x)
        return [jnp.where(lo, x, z), jnp.where(lo, z, x)]

    @pl.when(pl.program_id(1) == 0)
    def _():
        sub = lax.broadcasted_iota(jnp.int32, (LANES, 1), 0) < HEAD_DIM
        for c in range(seq // kvc):
            vt = jnp.transpose(v_ref[c * kvc:(c + 1) * kvc, :].astype(F32))
            vt_s[0, c] = _bf(jnp.where(sub, vt, 1.0))
            vt_s[1, c] = _bf(jnp.where(sub, 1.0, vt))

    q = q_ref[...]
    qm = stack_heads(q[:, :LANES]) + stack_heads(q[:, LANES:])
    n_heads = len(qm)

    def scores(c):
        kc = k_ref[c * kvc:(c + 1) * kvc, :]
        return [lax.dot_general(kc, qm[h], (((1,), (1,)), ((), ())), preferred_element_type=F32)
                for h in range(n_heads)]

    ms = [jnp.full((1, tq), -jnp.inf, F32) for _ in range(n_heads)]
    accs = [jnp.zeros((LANES, tq), F32) for _ in range(n_heads)]
    nsub = seq // kvc
    sts = scores(0)
    for c in range(nsub):
        nxt = scores(c + 1) if c + 1 < nsub else None
        new_ms = [jnp.maximum(ms[h], jnp.max(sts[h], axis=0, keepdims=True)) for h in range(n_heads)]
        ps = [_bf(jnp.exp2(sts[h] - new_ms[h])) for h in range(n_heads)]
        accs = [jnp.exp2(ms[h] - new_ms[h]) * accs[h]
                + jnp.dot(vt_s[h % 2, c], ps[h], preferred_element_type=F32) for h in range(n_heads)]
        ms, sts = new_ms, nxt
    o_t = []
    for h in range(n_heads):
        if h % 2 == 0:
            o_t.append(accs[h][0:HEAD_DIM] / accs[h][HEAD_DIM:HEAD_DIM + 1])
        else:
            o_t.append(accs[h][HEAD_DIM:LANES] / accs[h][0:1])
    ya = jnp.transpose(jnp.concatenate(o_t, axis=0))
    ya_ref[...] = _bf(ya * ga_ref[...].astype(F32))

    mq = mq_ref[...]
    outs = []
    for c in range(2):
        qc = jnp.concatenate(stack_heads(mq[:, c * LANES:(c + 1) * LANES]), axis=0)
        kc = km_ref[:, c * LANES:(c + 1) * LANES]
        vc = vm_ref[:, c * LANES:(c + 1) * LANES]
        s = lax.dot_general(qc, kc, (((1,), (1,)), ((), ())), preferred_element_type=F32)
        p = jnp.exp2(s - jnp.max(s, axis=-1, keepdims=True))
        om = jnp.dot(_bf(p), vc, preferred_element_type=F32) / jnp.sum(p, axis=-1, keepdims=True)
        outs.append(jnp.where(lo, om[0:tq], om[tq:2 * tq]))
    ym_ref[...] = _bf(jnp.concatenate(outs, axis=1) * gm_ref[...].astype(F32))


def _attention(qkv, mq, memkv, gates, batch, seq, tq, kvc):
    T = qkv.shape[0]
    nq = seq // tq
    mem_len = memkv.shape[0] // batch
    qrow = lambda b, i: (b * nq + i, 0)
    return pl.pallas_call(
        functools.partial(_attn_kernel, kvc=kvc),
        grid=(batch, nq),
        in_specs=[
            pl.BlockSpec((tq, 256), qrow),
            pl.BlockSpec((seq, LANES), lambda b, i: (b, 2)),
            pl.BlockSpec((seq, LANES), lambda b, i: (b, 3)),
            pl.BlockSpec((tq, 256), qrow),
            pl.BlockSpec((mem_len, 256), lambda b, i: (b, 0)),
            pl.BlockSpec((mem_len, 256), lambda b, i: (b, 1)),
            pl.BlockSpec((tq, 256), lambda b, i: (b * nq + i, 0)),
            pl.BlockSpec((tq, 256), lambda b, i: (b * nq + i, 1)),
        ],
        out_specs=[pl.BlockSpec((tq, 256), qrow), pl.BlockSpec((tq, 256), qrow)],
        out_shape=[jax.ShapeDtypeStruct((T, 256), BF16), jax.ShapeDtypeStruct((T, 256), BF16)],
        scratch_shapes=[pltpu.VMEM((KV_HEADS, seq // kvc, LANES, kvc), BF16)],
        compiler_params=pltpu.CompilerParams(
            dimension_semantics=("arbitrary", "arbitrary"), vmem_limit_bytes=VMEM_LIMIT),
        name="attention",
    )(qkv, qkv, qkv, mq, memkv, memkv, gates, gates)


GDN_W = GDN_GROUP * HEAD_DIM
GDN_NGRP = N_HEADS // GDN_GROUP
N_COMBO = N_DIRS * N_HEADS
N_QTY = 2


def _gdn_constants():
    w = GDN_W
    i = np.arange(CHUNK)[:, None]
    lane = np.arange(w)[None, :]
    j = lane % CHUNK
    tri = np.stack([i >= j, i <= j]).astype(np.float32)
    stri = np.stack([i > j, i < j]).astype(np.float32)
    eye = (i == j).astype(np.float32)[None]
    tri_c = np.concatenate([tri, stri, eye], axis=0)
    r = np.arange(w)[:, None]
    bd = ((r // CHUNK) == (lane // CHUNK)).astype(np.float32)
    expand = np.zeros((N_DIRS * GDN_NGRP, LANES, N_QTY * w), np.float32)
    for d in range(N_DIRS):
        for g in range(GDN_NGRP):
            for qty in range(N_QTY):
                for hh in range(GDN_GROUP):
                    combo = d * N_HEADS + g * GDN_GROUP + hh
                    for piece in range(3):
                        src = piece * (N_QTY * N_COMBO) + qty * N_COMBO + combo
                        expand[d * GDN_NGRP + g, src, qty * w + hh * CHUNK: qty * w + (hh + 1) * CHUNK] = 1.0
    return tri_c, bd, expand


def _recurrent_kernel(qkv_ref, sm3_ref, tri_ref, bd_ref, exp_ref,
                      gx_ref, gsm_ref, wup_ref, bg_ref, gtri_ref, bdv_ref, stm_ref, kqm_ref,
                      o_ref, oc_ref, st_s, pa_s, pb_s, cd_s, acc_s, gst_s, gacc_s):
    seq = qkv_ref.shape[0]
    nchunk = seq // CHUNK
    w = GDN_W
    gla_init, gla_step, gla_finish = _gla_program(gx_ref, gsm_ref, wup_ref, bg_ref, gtri_ref, bdv_ref, stm_ref,
                                                  kqm_ref, oc_ref, gst_s, gacc_s)
    gla_init()
    acc_s[...] = jnp.zeros_like(acc_s)
    st_s[...] = jnp.zeros_like(st_s)
    bd_mask = bd_ref[...] > 0.5

    combos = [(d, grp) for d in range(N_DIRS) for grp in range(GDN_NGRP)]

    def load_chunk(chunk, d, grp):
        rs = pl.ds(pl.multiple_of(chunk * CHUNK, CHUNK), CHUNK)
        return (qkv_ref[rs, grp * w:(grp + 1) * w], qkv_ref[rs, 256 + grp * w:256 + (grp + 1) * w],
                qkv_ref[rs, 512 + grp * w:512 + (grp + 1) * w], sm3_ref[rs, :])

    nc = len(combos)
    head_masks = [(lax.broadcasted_iota(jnp.int32, (1, w), 1) // HEAD_DIM) == hh for hh in range(GDN_GROUP)]

    items = [(d, grp) for _ in range(GDN_WINDOW) for d, grp in combos]
    ni = len(items)

    def transform_stages(raws, outs):
        exs, diffs, kqs = [], [], []
        a_mats, ps, rs_ = [], [], []

        def stage_products():
            ex_all = [jnp.dot(jnp.concatenate([raws(t * nc + c)[3] for t in range(GDN_WINDOW)], axis=0),
                              exp_ref[c], preferred_element_type=F32) for c in range(nc)]
            for n, (d, grp) in enumerate(items):
                q, k, vb, _ = raws(n)
                ex = ex_all[n % nc][(n // nc) * CHUNK:(n // nc + 1) * CHUNK]
                betax, bx = ex[:, 0:w], ex[:, w:2 * w]
                ebx = jnp.exp(bx)
                b_end = bx[CHUNK - 1:CHUNK] if d == 0 else bx[0:1]
                exs.append([betax, betax * ebx, ebx, jnp.exp(b_end - bx)])
                diffs.append(bx - jnp.sum(bx * tri_ref[4], axis=0, keepdims=True))
                kbd = jnp.concatenate([jnp.where(hm, k, jnp.zeros_like(k)) for hm in head_masks], axis=0)
                kqs.append(lax.dot_general(jnp.concatenate([k, q], axis=0), kbd, (((1,), (1,)), ((), ())),
                                           preferred_element_type=F32))

        def stage_square():
            for n, (d, grp) in enumerate(items):
                tri = tri_ref[d] > 0.5
                stri = tri_ref[2 + d] > 0.5
                decay = jnp.where(tri, jnp.exp(jnp.where(tri, diffs[n], 0.0)), 0.0)
                low = jnp.where(stri, exs[n][0] * kqs[n][:CHUNK] * decay, 0.0)
                a_mats.append(kqs[n][CHUNK:] * decay)
                ps.append(_mm(low, _block_diag(low, bd_mask, GDN_GROUP)))
                rs_.append(tri_ref[4] - low)

        def stage_level(last):
            for n in range(ni):
                pbd = _block_diag(ps[n], bd_mask, GDN_GROUP)
                if last:
                    rs_[n] = rs_[n] + _mm(rs_[n], pbd)
                else:
                    rp = _mm(jnp.concatenate([rs_[n], ps[n]], axis=0), pbd)
                    rs_[n] = rs_[n] + rp[:CHUNK]
                    ps[n] = rp[CHUNK:]

        def stage_apply():
            for n, (d, grp) in enumerate(items):
                q, k, vb = raws(n)[:3]
                betax, bebx, ebx, edx = exs[n]
                kf = k.astype(F32)
                u = _mm(rs_[n], _block_diag(betax * vb.astype(F32), bd_mask, GDN_GROUP))
                wm = _mm(rs_[n], _block_diag(bebx * kf, bd_mask, GDN_GROUP))
                cd_row = ebx[CHUNK - 1:CHUNK] if d == 0 else ebx[0:1]
                outs.append((a_mats[n], u, _bf(wm), _bf(q.astype(F32) * ebx), _bf(kf * edx), cd_row))

        return ([stage_products, stage_square] + [functools.partial(stage_level, False)] * 4
                + [functools.partial(stage_level, True), stage_apply])

    def store_transformed(c, t):
        a_mat, u, wm, q_dec, k_dec, cd_row = t
        pa_s[2 * c] = a_mat
        pa_s[2 * c + 1] = u
        pb_s[3 * c] = wm
        pb_s[3 * c + 1] = q_dec
        pb_s[3 * c + 2] = k_dec
        cd_s[c] = jnp.broadcast_to(cd_row, (8, w))

    def load_transformed(c):
        return pa_s[2 * c], pa_s[2 * c + 1], pb_s[3 * c], pb_s[3 * c + 1], pb_s[3 * c + 2], cd_s[c][0:1]

    def scan_stages(states, cur, os_):
        tmp = {}

        def stage_ws(t):
            tmp["cur"] = [cur(t * nc + c) for c in range(nc)]
            tmp["ws"] = [_mm(jnp.concatenate([tmp["cur"][c][2], tmp["cur"][c][3]], axis=0), states[c])
                         for c in range(nc)]

        def stage_update(t):
            for c in range(nc):
                a_mat, u, _, _, k_dec, cd_row = tmp["cur"][c]
                v_new = u - tmp["ws"][c][:CHUNK]
                os_.append(tmp["ws"][c][CHUNK:] + _mm(a_mat, _block_diag(v_new, bd_mask, GDN_GROUP)))
                states[c] = cd_row * states[c] + jnp.where(bd_mask, _mm_tn(k_dec, v_new), 0.0)

        stages = []
        for t in range(GDN_WINDOW):
            stages += [functools.partial(stage_ws, t), functools.partial(stage_update, t)]
        return stages

    def chunk_of(j, t, d):
        i = j * GDN_WINDOW + t
        return i if d == 0 else nchunk - 1 - i

    def window_loader(j):
        def load(n):
            d, grp = items[n]
            return load_chunk(chunk_of(j, n // nc, d), d, grp)
        return load

    first = []
    for stage in transform_stages(window_loader(0), first):
        stage()
    for n in range(ni):
        store_transformed(n, first[n])

    nwin = nchunk // GDN_WINDOW

    def step(j, carry):
        states = [st_s[c] for c in range(nc)]
        os_, nexts = [], []
        t_stages = transform_stages(window_loader(jnp.minimum(j + 1, nwin - 1)), nexts)
        s_stages = scan_stages(states, load_transformed, os_)
        g_stages, gla_stores = gla_step(j)
        g_at = {2 * i: g for i, g in enumerate(g_stages)}
        for k in range(max(len(t_stages), len(s_stages))):
            if k < len(t_stages):
                t_stages[k]()
            if k < len(s_stages):
                s_stages[k]()
            if k in g_at:
                g_at[k]()
        for t in range(GDN_WINDOW):
            for c, (d, grp) in enumerate(combos):
                rs = pl.ds(pl.multiple_of(chunk_of(j, t, d) * CHUNK, CHUNK), CHUNK)
                acc_s[rs, grp * w:(grp + 1) * w] += os_[t * nc + c]
        for c in range(nc):
            st_s[c] = states[c]
        for n in range(ni):
            store_transformed(n, nexts[n])
        gla_stores()
        return carry

    lax.fori_loop(0, nwin, step, 0)
    o_ref[...] = _bf(acc_s[...])
    gla_finish()


def _recurrent(gdn_qkv, gdn_sm3, gla_in, small, wup, bg, gdn_consts, gla_consts, batch, seq):
    T = gdn_qkv.shape[0]
    assert GDN_WINDOW == GLA_WINDOW and seq % (CHUNK * GDN_WINDOW) == 0
    tri_c, bd, expand = gdn_consts
    tri, bd_v, st_mask, kq_mask = gla_consts
    whole = lambda b: (b, 0)
    c2 = lambda b: (0, 0)
    c3 = lambda b: (0, 0, 0)
    out = pl.BlockSpec((seq, 256), whole)
    return pl.pallas_call(
        _recurrent_kernel,
        grid=(batch,),
        in_specs=[
            pl.BlockSpec((seq, 768), whole),
            pl.BlockSpec((seq, LANES), whole),
            pl.BlockSpec(tri_c.shape, c3),
            pl.BlockSpec(bd.shape, c2),
            pl.BlockSpec(expand.shape, c3),
            pl.BlockSpec((seq, 512), whole),
            pl.BlockSpec((seq, LANES), whole),
            pl.BlockSpec(wup.shape, c3),
            pl.BlockSpec(bg.shape, c2),
            pl.BlockSpec(tri.shape, c3),
            pl.BlockSpec(bd_v.shape, c2),
            pl.BlockSpec(st_mask.shape, c2),
            pl.BlockSpec(kq_mask.shape, c2),
        ],
        out_specs=[out, out],
        out_shape=[jax.ShapeDtypeStruct((T, 256), BF16), jax.ShapeDtypeStruct((T, 256), BF16)],
        scratch_shapes=[
            pltpu.VMEM((N_DIRS * GDN_NGRP, GDN_W, GDN_W), F32),
            pltpu.VMEM((2 * GDN_WINDOW * N_DIRS * GDN_NGRP, CHUNK, GDN_W), F32),
            pltpu.VMEM((3 * GDN_WINDOW * N_DIRS * GDN_NGRP, CHUNK, GDN_W), BF16),
            pltpu.VMEM((GDN_WINDOW * N_DIRS * GDN_NGRP, 8, GDN_W), F32),
            pltpu.VMEM((seq, 256), F32),
            pltpu.VMEM((N_DIRS, GLA_QW, GLA_VW), F32),
            pltpu.VMEM((seq, 256), F32),
        ],
        compiler_params=pltpu.CompilerParams(
            dimension_semantics=("arbitrary",), vmem_limit_bytes=VMEM_LIMIT),
        name="recurrent",
    )(gdn_qkv, gdn_sm3, tri_c, bd, expand, gla_in, small, wup, bg, tri, bd_v, st_mask, kq_mask)


GLA_QW = N_HEADS * GLA_DK
GLA_VW = N_HEADS * HEAD_DIM
GLA_WINDOW = 4


def _gla_constants():
    i = np.arange(CHUNK)[:, None]
    lane = np.arange(GLA_VW)[None, :]
    j = lane % CHUNK
    tri = np.stack([i >= j, i <= j]).astype(np.float32)
    r = np.arange(GLA_VW)[:, None]
    bd_v = ((r // CHUNK) == (lane // HEAD_DIM)).astype(np.float32)
    rs = np.arange(GLA_QW)[:, None]
    st_mask = ((rs // GLA_DK) == (lane // HEAD_DIM)).astype(np.float32)
    kq_mask = ((r // CHUNK) == (np.arange(GLA_QW)[None, :] // GLA_DK)).astype(np.float32)
    return tri, bd_v, st_mask, kq_mask


def _gla_program(x_ref, sm_ref, wup_ref, bg_ref, tri_ref, bdv_ref, stm_ref, kqm_ref, o_ref, st_s, acc_s):
    seq = x_ref.shape[0]
    nchunk = seq // CHUNK
    bdv_mask = bdv_ref[...] > 0.5
    st_mask = stm_ref[...] > 0.5
    kq_mask = kqm_ref[...] > 0.5

    def init():
        acc_s[...] = jnp.zeros_like(acc_s)
        st_s[...] = jnp.zeros_like(st_s)

    def rows_of(chunk):
        return pl.ds(pl.multiple_of(chunk * CHUNK, CHUNK), CHUNK)

    def load_chunk(chunk):
        rs = rows_of(chunk)
        return (x_ref[rs, 0:GLA_QW].astype(F32), x_ref[rs, GLA_QW:2 * GLA_QW].astype(F32),
                x_ref[rs, 2 * GLA_QW:2 * GLA_QW + GLA_VW], sm_ref[rs, :])

    items = [(t, d) for t in range(GLA_WINDOW) for d in range(N_DIRS)]
    ni = len(items)

    def chunk_of(jw, t, d):
        i = jw * GLA_WINDOW + t
        return i if d == 0 else nchunk - 1 - i

    def step(jw):
        raws, logits, q_es, attns, dsts, cds, outs = [], [], [], [], [], [], []
        states = []

        def stage_logits():
            for n, (t, d) in enumerate(items):
                raws.append(load_chunk(chunk_of(jw, t, d)))
                logits.append(jnp.dot(_bf(raws[n][3]), wup_ref[d], preferred_element_type=F32)
                              + bg_ref[d:d + 1, :])

        def stage_intra():
            for n, (t, d) in enumerate(items):
                q, k, v, _ = raws[n]
                gk = (jnp.minimum(logits[n], 0.0) - jnp.log1p(jnp.exp(-jnp.abs(logits[n])))) \
                    * (1.0 / GLA_NORMALIZER)
                b = _cumsum_rows(gk, reverse=(d == 1))
                b_end = b[CHUNK - 1:CHUNK] if d == 0 else b[0:1]
                q_e = q * (GLA_DK ** -0.5) * jnp.exp(b)
                keb = _bf(k * jnp.exp(-b))
                kebd = jnp.where(kq_mask, jnp.concatenate([keb] * N_HEADS, axis=0),
                                 jnp.zeros((GLA_VW, GLA_QW), BF16))
                q_es.append(_bf(q_e))
                attns.append(jnp.where(tri_ref[d] > 0.5, _mm_nt(q_es[n], kebd), 0.0))
                dsts.append(jnp.where(st_mask, _mm_tn(k * jnp.exp(b_end - b), v), 0.0))
                cd_col = jnp.transpose(jnp.broadcast_to(jnp.exp(b_end), (GLA_QW, GLA_QW)))
                cds.append(jnp.concatenate([cd_col, cd_col], axis=1))

        def stage_out():
            states.extend(st_s[d] for d in range(N_DIRS))
            for n in range(ni):
                outs.append(_mm(attns[n], _block_diag(raws[n][2], bdv_mask, N_HEADS)))
            for n, (t, d) in enumerate(items):
                outs[n] = outs[n] + _mm(q_es[n], states[d])
                states[d] = cds[n] * states[d] + dsts[n]

        def stores():
            for n, (t, d) in enumerate(items):
                acc_s[rows_of(chunk_of(jw, t, d)), :] += outs[n]
            for d in range(N_DIRS):
                st_s[d] = states[d]

        return [stage_logits, stage_intra, stage_out], stores

    def finish():
        o_ref[...] = _bf(acc_s[...])

    return init, step, finish


def _outproj_kernel(ya_ref, ob_ref, oc_ref, ym_ref, gates_ref, gb_ref, gc_ref, ones_ref, w_ref, h_ref,
                    fg_ref, o_ref, *, final):
    ones256 = ones_ref[...]

    ob = ob_ref[...].astype(F32)
    oc = oc_ref[...].astype(F32)
    ssb, ssc = _group_sums([ob * ob, oc * oc], ones256)
    yb = ob * lax.rsqrt(ssb * (1.0 / HEAD_DIM) + EPS) * gb_ref[...] * gates_ref[:, 0:256].astype(F32)
    yc = oc * lax.rsqrt(ssc * (1.0 / HEAD_DIM) + EPS) * gc_ref[...] * gates_ref[:, 256:512].astype(F32)
    y = jnp.dot(ya_ref[...], w_ref[0:256, :], preferred_element_type=F32)
    y = y + jnp.dot(_bf(yb), w_ref[256:512, :], preferred_element_type=F32)
    y = y + jnp.dot(_bf(yc), w_ref[512:768, :], preferred_element_type=F32)
    y = y + jnp.dot(ym_ref[...], w_ref[768:1024, :], preferred_element_type=F32)
    hn = h_ref[...] + y
    if final:
        hn = hn * lax.rsqrt(jnp.mean(hn * hn, axis=-1, keepdims=True) + EPS) * fg_ref[...]
    o_ref[...] = hn


def _outproj(ya, ob, oc, ym, gates, gb, gc, ones256, w, h, fg, tm, final):
    T, D = h.shape
    row = lambda i: (i, 0)
    const = lambda i: (0, 0)
    return pl.pallas_call(
        functools.partial(_outproj_kernel, final=final),
        grid=(T // tm,),
        in_specs=[
            pl.BlockSpec((tm, 256), row), pl.BlockSpec((tm, 256), row),
            pl.BlockSpec((tm, 256), row), pl.BlockSpec((tm, 256), row),
            pl.BlockSpec((tm, 512), lambda i: (i, 1)),
            pl.BlockSpec((1, 256), const), pl.BlockSpec((1, 256), const),
            pl.BlockSpec((256, 256), const),
            pl.BlockSpec((D, D), const),
            pl.BlockSpec((tm, D), row),
            pl.BlockSpec((1, D), const),
        ],
        out_specs=pl.BlockSpec((tm, D), row),
        out_shape=jax.ShapeDtypeStruct((T, D), F32),
        compiler_params=pltpu.CompilerParams(
            dimension_semantics=("arbitrary",), vmem_limit_bytes=VMEM_LIMIT),
        name="outproj_final" if final else "outproj",
    )(ya, ob, oc, ym, gates, gb, gc, ones256, w, h, fg)


def _block_ones(n, group):
    i = np.arange(n)
    return jnp.asarray((i[:, None] // group) == (i[None, :] // group), dtype=BF16)


def _head_perm_cols():
    return np.concatenate([np.arange(h * HEAD_DIM, (h + 1) * HEAD_DIM) for h in ATT_HEAD_PERM])


def _prep_w_in(w_in):
    o = np.cumsum([0, 256, 128, 128, 256, 768, 8, 8, 256, 128, 128, 256, 32, 256, 256, 256])
    (a_q, a_k, a_v, a_g, d_qkv, d_beta, d_alpha, d_gate, l_q, l_k, l_v, l_low, l_gate, m_q, m_gate) = [
        w_in[:, :, o[i]:o[i + 1]] for i in range(15)]
    perm = _head_perm_cols()
    pad = jnp.zeros(w_in.shape[:2] + (N_COLS - C_SM - 48,), w_in.dtype)
    cols = [a_q[:, :, perm], a_k, a_v, a_g[:, :, perm], d_qkv, d_gate, l_q, l_k, l_v, l_gate, m_q, m_gate,
            d_beta, d_alpha, l_low, pad]
    return jnp.concatenate(cols, axis=-1).astype(BF16)


def _rope_tables(seq):
    rows = seq // GRID_W
    row_pos = jnp.repeat(jnp.arange(rows, dtype=F32), GRID_W)
    col_pos = jnp.tile(jnp.arange(GRID_W, dtype=F32), rows)
    half = HEAD_DIM // 4
    inv_freq = ROPE_THETA ** (-jnp.arange(0, 2 * half, 2, dtype=F32) / (2 * half))
    ang = jnp.stack([row_pos, col_pos], axis=-1)[..., None] * inv_freq
    cos, sin = jnp.cos(ang), jnp.sin(ang)
    cos_h = jnp.concatenate([cos, cos], axis=-1).reshape(seq, HEAD_DIM)
    sin_h = jnp.concatenate([-sin, sin], axis=-1).reshape(seq, HEAD_DIM)
    return jnp.tile(cos_h, (1, 2)), jnp.tile(sin_h, (1, 2))


def _trunk(x, mem, p, consts):
    batch, seq, d_model = x.shape
    T = batch * seq
    depth = p["w_in"].shape[0]
    tm = 1024 if seq % 1024 == 0 else (512 if seq % 512 == 0 else seq)
    tq = 256 if seq % 256 == 0 else seq
    kvc = 512 if seq % 512 == 0 else seq
    cos_t, sin_t = _rope_tables(seq)
    mem2d = mem.reshape(batch * mem.shape[1], d_model)
    memkv = _memkv(mem2d, p["mem_norm_g"], p["w_mem_kv"], min(512, mem2d.shape[0]))
    h = x.reshape(T, d_model)
    for d in range(depth):
        qkv, gates, gdn_qkv, gla_in, mq, small, gdn_sm3 = _inproj(
            h, p["norm_g"][d], p["w_in"][d], cos_t, sin_t, p["qkg"][d], consts["ones384"],
            p["conv_w"][d], p["gdn_par"][d], consts["ones256"], seq, tm)
        ya, ym = _attention(qkv, mq, memkv[d], gates, batch, seq, tq, kvc)
        ob, oc = _recurrent(gdn_qkv, gdn_sm3, gla_in, small, p["wup"][d], p["bg"][d],
                            consts["gdn"], consts["gla"], batch, seq)
        h = _outproj(ya, ob, oc, ym, gates, p["gb"][d], p["gc"][d],
                     consts["ones256"], p["w_out"][d], h, p["final_g"], tm, final=(d == depth - 1))
    return h.reshape(batch, seq, d_model)


def kernel(x_prompt, x_sample, mem_prompt, mem_sample, norm_g, w_in, att_q_norm_g, att_k_norm_g, gdn_conv_w,
           gdn_a_log, gdn_dt_bias, gdn_out_norm_g, gla_w_gate_up, gla_b_gate, gla_out_norm_g, mem_norm_g,
           w_mem_kv, w_out, final_norm_g):
    depth, d_model = norm_g.shape
    perm = _head_perm_cols()
    w_out_p = jnp.concatenate([w_out[:, :256][:, perm], w_out[:, 256:]], axis=1).astype(BF16)
    qkg = jnp.concatenate([jnp.tile(att_q_norm_g, (1, N_HEADS)), jnp.tile(att_k_norm_g, (1, KV_HEADS))], axis=1)
    zeros8 = jnp.zeros((depth, N_COMBO), F32)
    par_a = jnp.concatenate([zeros8, -jnp.exp(gdn_a_log.reshape(depth, N_COMBO)),
                             jnp.zeros((depth, LANES - 2 * N_COMBO), F32)], axis=1)
    par_b = jnp.concatenate([zeros8, gdn_dt_bias.reshape(depth, N_COMBO),
                             jnp.zeros((depth, LANES - 2 * N_COMBO), F32)], axis=1)
    gdn_par = jnp.concatenate([par_a[:, None], par_b[:, None], jnp.zeros((depth, 6, LANES), F32)], axis=1)
    conv_w = jnp.concatenate([gdn_conv_w, jnp.zeros((depth, 5, gdn_conv_w.shape[2]), F32)], axis=1)
    wup = jnp.zeros((depth, N_DIRS, LANES, GLA_QW), F32)
    for d in range(N_DIRS):
        lo = 2 * N_COMBO + d * GLA_RANK
        wup = wup.at[:, d, lo:lo + GLA_RANK, :].set(gla_w_gate_up[:, d])
    p = {
        "norm_g": norm_g.reshape(depth, 1, d_model),
        "w_in": _prep_w_in(w_in),
        "qkg": qkg.reshape(depth, 1, 384),
        "conv_w": conv_w,
        "gdn_par": gdn_par,
        "wup": wup.astype(BF16),
        "bg": jnp.concatenate([gla_b_gate, jnp.zeros((depth, 6, GLA_QW), F32)], axis=1),
        "gb": jnp.tile(gdn_out_norm_g, (1, N_HEADS)).reshape(depth, 1, 256),
        "gc": jnp.tile(gla_out_norm_g, (1, N_HEADS)).reshape(depth, 1, 256),
        "mem_norm_g": mem_norm_g.reshape(depth, 1, d_model),
        "w_mem_kv": w_mem_kv.astype(BF16),
        "w_out": w_out_p,
        "final_g": final_norm_g.reshape(1, d_model),
    }
    tri_c, bd, expand = _gdn_constants()
    consts = {
        "ones384": _block_ones(384, HEAD_DIM),
        "ones256": _block_ones(256, HEAD_DIM),
        "gdn": (jnp.asarray(tri_c), jnp.asarray(bd), jnp.asarray(expand, dtype=BF16)),
        "gla": tuple(jnp.asarray(a) for a in _gla_constants()),
    }
    y_prompt = _trunk(x_prompt, mem_prompt, p, consts)
    y_sample = _trunk(x_sample, mem_sample, p, consts)
    return (y_prompt, y_sample)
```

```python
import functools

import numpy as np
import jax
import jax.numpy as jnp
from jax import lax
from jax.experimental import pallas as pl
from jax.experimental.pallas import tpu as pltpu

F32 = jnp.float32
BF16 = jnp.bfloat16

N_HEADS = 4
HEAD_DIM = 64
KV_HEADS = 2
GRID_W = 64
ROPE_THETA = 10000.0
CHUNK = 64
GLA_DK = 32
GLA_RANK = 16
GLA_NORMALIZER = 16.0
EPS = 1e-6
N_DIRS = 2
LANES = 128
GDN_GROUP = 2
GDN_WINDOW = 4
VMEM_LIMIT = 56 * 1024 * 1024

C_AQ, C_AK, C_AV, C_AG = 0, 256, 384, 512
C_DQKV, C_DG = 768, 1536
C_LQ, C_LK, C_LV, C_LG = 1792, 1920, 2048, 2304
C_MQ, C_MG = 2560, 2816
C_SM = 3072
N_COLS = 3200
ATT_HEAD_PERM = (0, 2, 1, 3)
ATT_VT_ROWS = HEAD_DIM + 16
LOG2_E = float(np.log2(np.e))
ATT_Q_SCALE = HEAD_DIM ** -0.5 * LOG2_E


def _bf(x):
    return x.astype(BF16)


def _mm(a, b):
    return jnp.dot(_bf(a), _bf(b), preferred_element_type=F32)


def _mm_nt(a, b):
    return lax.dot_general(_bf(a), _bf(b), (((1,), (1,)), ((), ())), preferred_element_type=F32)


def _mm_tn(a, b):
    return lax.dot_general(_bf(a), _bf(b), (((0,), (0,)), ((), ())), preferred_element_type=F32)


def _split2(x):
    hi = _bf(x)
    lo = _bf(x - hi.astype(F32))
    return hi, lo


def _split3(x):
    hi = _bf(x).astype(F32)
    r = x - hi
    mid = _bf(r).astype(F32)
    lo = _bf(r - mid).astype(F32)
    return hi, mid, lo


def _group_sums(xs, ones_blocks):
    rows = xs[0].shape[0]
    parts = []
    for x in xs:
        parts.extend(_split2(x))
    s = jnp.dot(jnp.concatenate(parts, axis=0), ones_blocks, preferred_element_type=F32)
    return [s[2 * i * rows:(2 * i + 1) * rows] + s[(2 * i + 1) * rows:(2 * i + 2) * rows] for i in range(len(xs))]


def _silu(x):
    return x * (0.5 * jnp.tanh(0.5 * x) + 0.5)


def _softplus(x):
    return jnp.maximum(x, 0.0) + jnp.log1p(jnp.exp(-jnp.abs(x)))


def _cumsum_rows(x, reverse):
    n = x.shape[0]
    row = lax.broadcasted_iota(jnp.int32, x.shape, 0)
    s = 1
    while s < n:
        if reverse:
            x = x + jnp.where(row < n - s, pltpu.roll(x, n - s, axis=0), 0.0)
        else:
            x = x + jnp.where(row >= s, pltpu.roll(x, s, axis=0), 0.0)
        s *= 2
    return x


def _block_diag(x, mask, reps):
    xb = _bf(x)
    t = jnp.concatenate([xb] * reps, axis=0)
    return jnp.where(mask, t, jnp.zeros_like(t))


def _inproj_kernel(h_ref, hprev_ref, hnext_ref, g_ref, w_ref, cos_ref, sin_ref, qkg_ref, ones_ref,
                   cw_ref, par_ref, ones256_ref,
                   qkv_ref, gates_ref, gdn_ref, gla_ref, mq_ref, small_ref, sm3_ref, *, nseq):
    tm = h_ref.shape[0]

    def normed(x):
        return _bf(x * lax.rsqrt(jnp.mean(x * x, axis=-1, keepdims=True) + EPS) * g_ref[...])

    halo = jnp.concatenate([hprev_ref[0], hnext_ref[0]], axis=0)
    hp = jnp.dot(normed(halo), w_ref[:, C_DQKV:C_DG], preferred_element_type=F32)
    xn = normed(h_ref[...])
    groups = ((C_DQKV, C_DG), (C_SM, N_COLS), (0, C_DQKV), (C_DG, C_SM))
    pieces = [jnp.dot(xn, w_ref[:, a:b], preferred_element_type=F32) for a, b in groups]

    class _Proj:
        def __getitem__(self, idx):
            rows, cols = idx
            for (a, b), piece in zip(groups, pieces):
                if a <= cols.start and cols.stop <= b:
                    return piece[rows, cols.start - a:cols.stop - a]
            raise IndexError(cols)

    proj = _Proj()

    pos = pl.program_id(0) % nseq
    prev_row = hp[7:8] * jnp.where(pos > 0, 1.0, 0.0)
    next_row = hp[8:9] * jnp.where(pos < nseq - 1, 1.0, 0.0)
    xg = proj[:, C_DQKV:C_DG]
    trow = lax.broadcasted_iota(jnp.int32, (tm, 1), 0)
    x_prev = jnp.where(trow == 0, prev_row, pltpu.roll(xg, 1, axis=0))
    x_next = jnp.where(trow == tm - 1, next_row, pltpu.roll(xg, tm - 1, axis=0))
    y = x_prev * cw_ref[0:1, :] + xg * cw_ref[1:2, :]
    y = _silu(y + x_next * cw_ref[2:3, :])
    sss = _group_sums([y[:, 0:256] * y[:, 0:256], y[:, 256:512] * y[:, 256:512]], ones256_ref[...])
    gdn_ref[:, 0:256] = _bf(y[:, 0:256] * lax.rsqrt(sss[0] + EPS) * (HEAD_DIM ** -0.5))
    gdn_ref[:, 256:512] = _bf(y[:, 256:512] * lax.rsqrt(sss[1] + EPS))
    gdn_ref[:, 512:768] = _bf(y[:, 512:768])

    lane = lax.broadcasted_iota(jnp.int32, (1, LANES), 1)
    lane8 = lane < N_COMBO
    fwd_lane = lane < N_HEADS
    for c in range(tm // CHUNK):
        sm = proj[c * CHUNK:(c + 1) * CHUNK, C_SM:N_COLS]
        beta = 0.5 * jnp.tanh(0.5 * sm) + 0.5
        g_all = par_ref[0:1, :] * _softplus(sm + par_ref[1:2, :])
        g = pltpu.roll(g_all, LANES - N_COMBO, axis=1)
        b = jnp.where(fwd_lane, _cumsum_rows(g, reverse=False), _cumsum_rows(g, reverse=True))
        zero = jnp.zeros_like(b)
        q2 = jnp.where(lane8, beta, zero) + pltpu.roll(jnp.where(lane8, b, zero), N_COMBO, axis=1)
        hi, mid, lo3 = _split3(q2)
        sm3_ref[c * CHUNK:(c + 1) * CHUNK, :] = _bf(
            hi + pltpu.roll(mid, 2 * N_COMBO, axis=1) + pltpu.roll(lo3, 4 * N_COMBO, axis=1))

    qk = proj[:, C_AQ:C_AV]
    ms = _group_sums([qk * qk], ones_ref[...])[0] * (1.0 / HEAD_DIM)
    qkn = qk * lax.rsqrt(ms + EPS) * qkg_ref[...]
    cos = cos_ref[...]
    sin = sin_ref[...]
    lane = lax.broadcasted_iota(jnp.int32, (1, LANES), 1)
    first_half = (lane % 32) < 16
    outs = []
    for c in range(3):
        xc = qkn[:, c * LANES:(c + 1) * LANES]
        partner = jnp.where(first_half, pltpu.roll(xc, LANES - 16, axis=1), pltpu.roll(xc, 16, axis=1))
        outs.append(xc * cos + partner * sin)
    qkv_ref[:, 0:128] = _bf(outs[0] * ATT_Q_SCALE)
    qkv_ref[:, 128:256] = _bf(outs[1] * ATT_Q_SCALE)
    qkv_ref[:, 256:384] = _bf(outs[2])
    qkv_ref[:, 384:512] = _bf(proj[:, C_AV:C_AG])

    gates_ref[:, 0:256] = _bf(_silu(proj[:, C_AG:C_AG + 256]))
    gates_ref[:, 256:512] = _bf(_silu(proj[:, C_MG:C_MG + 256]))
    gates_ref[:, 512:768] = _bf(_silu(proj[:, C_DG:C_DG + 256]))
    gates_ref[:, 768:1024] = _bf(_silu(proj[:, C_LG:C_LG + 256]))
    gla_ref[...] = _bf(proj[:, C_LQ:C_LG])
    mq_ref[...] = _bf(proj[:, C_MQ:C_MG] * ATT_Q_SCALE)
    small_ref[...] = proj[:, C_SM:N_COLS]


def _inproj(h, g, w, cos_t, sin_t, qkg, ones384, conv_w, par, ones256, seq, tm):
    T, D = h.shape
    nseq = seq // tm
    row = lambda i: (i, 0)
    const = lambda i: (0, 0)
    h8 = h.reshape(T // 8, 8, D)
    g8 = tm // 8
    return pl.pallas_call(
        functools.partial(_inproj_kernel, nseq=nseq),
        grid=(T // tm,),
        in_specs=[
            pl.BlockSpec((tm, D), row),
            pl.BlockSpec((1, 8, D), lambda i: (jnp.maximum(i * g8 - 1, 0), 0, 0)),
            pl.BlockSpec((1, 8, D), lambda i: (jnp.minimum((i + 1) * g8, T // 8 - 1), 0, 0)),
            pl.BlockSpec((1, D), const),
            pl.BlockSpec((D, N_COLS), const),
            pl.BlockSpec((tm, LANES), lambda i: (i % nseq, 0)),
            pl.BlockSpec((tm, LANES), lambda i: (i % nseq, 0)),
            pl.BlockSpec((1, 384), const),
            pl.BlockSpec((384, 384), const),
            pl.BlockSpec((8, 768), const),
            pl.BlockSpec((8, LANES), const),
            pl.BlockSpec((256, 256), const),
        ],
        out_specs=[
            pl.BlockSpec((tm, 512), row),
            pl.BlockSpec((tm, 1024), row),
            pl.BlockSpec((tm, 768), row),
            pl.BlockSpec((tm, 512), row),
            pl.BlockSpec((tm, 256), row),
            pl.BlockSpec((tm, LANES), row),
            pl.BlockSpec((tm, LANES), row),
        ],
        out_shape=[
            jax.ShapeDtypeStruct((T, 512), BF16),
            jax.ShapeDtypeStruct((T, 1024), BF16),
            jax.ShapeDtypeStruct((T, 768), BF16),
            jax.ShapeDtypeStruct((T, 512), BF16),
            jax.ShapeDtypeStruct((T, 256), BF16),
            jax.ShapeDtypeStruct((T, LANES), F32),
            jax.ShapeDtypeStruct((T, LANES), BF16),
        ],
        compiler_params=pltpu.CompilerParams(
            dimension_semantics=("arbitrary",), vmem_limit_bytes=VMEM_LIMIT),
        name="inproj",
    )(h, h8, h8, g, w, cos_t, sin_t, qkg, ones384, conv_w, par, ones256)


def _memkv_kernel(m_ref, g_ref, w_ref, o_ref):
    x = m_ref[...]
    xn = x * lax.rsqrt(jnp.mean(x * x, axis=-1, keepdims=True) + EPS) * g_ref[0]
    o_ref[0] = _bf(jnp.dot(_bf(xn), w_ref[0], preferred_element_type=F32))


def _memkv(mem2d, g, w, tr):
    R, D = mem2d.shape
    depth = w.shape[0]
    return pl.pallas_call(
        _memkv_kernel,
        grid=(depth, R // tr),
        in_specs=[
            pl.BlockSpec((tr, D), lambda d, i: (i, 0)),
            pl.BlockSpec((1, 1, D), lambda d, i: (d, 0, 0)),
            pl.BlockSpec((1, D, 512), lambda d, i: (d, 0, 0)),
        ],
        out_specs=pl.BlockSpec((1, tr, 512), lambda d, i: (d, i, 0)),
        out_shape=jax.ShapeDtypeStruct((depth, R, 512), BF16),
        compiler_params=pltpu.CompilerParams(
            dimension_semantics=("arbitrary", "arbitrary"), vmem_limit_bytes=VMEM_LIMIT),
        name="memkv",
    )(mem2d, g, w)


def _attn_kernel(q_ref, k_ref, v_ref, mq_ref, km_ref, vm_ref, ga_ref, gm_ref, ya_ref, ym_ref, vt_s, *, kvc):
    tq = q_ref.shape[0]
    seq = k_ref.shape[0]
    lane = lax.broadcasted_iota(jnp.int32, (1, LANES), 1)
    lo = lane < HEAD_DIM

    def stack_heads(x):
        z = jnp.zeros_like(x)
        return [jnp.where(lo, x, z), jnp.where(lo, z, x)]

    @pl.when(pl.program_id(1) == 0)
    def _():
        ones = jnp.ones((ATT_VT_ROWS - HEAD_DIM, kvc), F32)
        for c in range(seq // kvc):
            vt = jnp.transpose(v_ref[c * kvc:(c + 1) * kvc, :].astype(F32))
            vt_s[0, c] = _bf(jnp.concatenate([vt[0:HEAD_DIM], ones], axis=0))
            vt_s[1, c] = _bf(jnp.concatenate([vt[HEAD_DIM:LANES], ones], axis=0))

    q = q_ref[...]
    qm = stack_heads(q[:, :LANES]) + stack_heads(q[:, LANES:])
    n_heads = len(qm)

    def scores(c):
        kc = k_ref[c * kvc:(c + 1) * kvc, :]
        return [lax.dot_general(kc, qm[h], (((1,), (1,)), ((), ())), preferred_element_type=F32)
                for h in range(n_heads)]

    ms = [jnp.full((1, tq), -jnp.inf, F32) for _ in range(n_heads)]
    accs = [jnp.zeros((ATT_VT_ROWS, tq), F32) for _ in range(n_heads)]
    nsub = seq // kvc
    sts = scores(0)
    for c in range(nsub):
        nxt = scores(c + 1) if c + 1 < nsub else None
        new_ms = [jnp.maximum(ms[h], jnp.max(sts[h], axis=0, keepdims=True)) for h in range(n_heads)]
        ps = [_bf(jnp.exp2(sts[h] - new_ms[h])) for h in range(n_heads)]
        accs = [jnp.exp2(ms[h] - new_ms[h]) * accs[h]
                + jnp.dot(vt_s[h % 2, c], ps[h], preferred_element_type=F32) for h in range(n_heads)]
        ms, sts = new_ms, nxt
    o_t = [accs[h][0:HEAD_DIM] / accs[h][HEAD_DIM:HEAD_DIM + 1] for h in range(n_heads)]
    ya = jnp.transpose(jnp.concatenate(o_t, axis=0))
    ya_ref[...] = _bf(ya * ga_ref[...].astype(F32))

    mq = mq_ref[...]
    outs = []
    for c in range(2):
        qc = jnp.concatenate(stack_heads(mq[:, c * LANES:(c + 1) * LANES]), axis=0)
        kc = km_ref[:, c * LANES:(c + 1) * LANES]
        vc = vm_ref[:, c * LANES:(c + 1) * LANES]
        s = lax.dot_general(qc, kc, (((1,), (1,)), ((), ())), preferred_element_type=F32)
        p = jnp.exp2(s - jnp.max(s, axis=-1, keepdims=True))
        om = jnp.dot(_bf(p), vc, preferred_element_type=F32) / jnp.sum(p, axis=-1, keepdims=True)
        outs.append(jnp.where(lo, om[0:tq], om[tq:2 * tq]))
    ym_ref[...] = _bf(jnp.concatenate(outs, axis=1) * gm_ref[...].astype(F32))


def _attention(qkv, mq, memkv, gates, batch, seq, tq, kvc):
    T = qkv.shape[0]
    nq = seq // tq
    mem_len = memkv.shape[0] // batch
    qrow = lambda b, i: (b * nq + i, 0)
    return pl.pallas_call(
        functools.partial(_attn_kernel, kvc=kvc),
        grid=(batch, nq),
        in_specs=[
            pl.BlockSpec((tq, 256), qrow),
            pl.BlockSpec((seq, LANES), lambda b, i: (b, 2)),
            pl.BlockSpec((seq, LANES), lambda b, i: (b, 3)),
            pl.BlockSpec((tq, 256), qrow),
            pl.BlockSpec((mem_len, 256), lambda b, i: (b, 0)),
            pl.BlockSpec((mem_len, 256), lambda b, i: (b, 1)),
            pl.BlockSpec((tq, 256), lambda b, i: (b * nq + i, 0)),
            pl.BlockSpec((tq, 256), lambda b, i: (b * nq + i, 1)),
        ],
        out_specs=[pl.BlockSpec((tq, 256), qrow), pl.BlockSpec((tq, 256), qrow)],
        out_shape=[jax.ShapeDtypeStruct((T, 256), BF16), jax.ShapeDtypeStruct((T, 256), BF16)],
        scratch_shapes=[pltpu.VMEM((KV_HEADS, seq // kvc, ATT_VT_ROWS, kvc), BF16)],
        compiler_params=pltpu.CompilerParams(
            dimension_semantics=("arbitrary", "arbitrary"), vmem_limit_bytes=VMEM_LIMIT),
        name="attention",
    )(qkv, qkv, qkv, mq, memkv, memkv, gates, gates)


GDN_W = GDN_GROUP * HEAD_DIM
GDN_NGRP = N_HEADS // GDN_GROUP
N_COMBO = N_DIRS * N_HEADS
N_QTY = 2


def _gdn_constants():
    w = GDN_W
    i = np.arange(CHUNK)[:, None]
    lane = np.arange(w)[None, :]
    j = lane % CHUNK
    tri = np.stack([i >= j, i <= j]).astype(np.float32)
    stri = np.stack([i > j, i < j]).astype(np.float32)
    eye = (i == j).astype(np.float32)[None]
    tri_c = np.concatenate([tri, stri, eye], axis=0)
    r = np.arange(w)[:, None]
    bd = ((r // CHUNK) == (lane // CHUNK)).astype(np.float32)
    expand = np.zeros((N_DIRS * GDN_NGRP, LANES, N_QTY * w), np.float32)
    for d in range(N_DIRS):
        for g in range(GDN_NGRP):
            for qty in range(N_QTY):
                for hh in range(GDN_GROUP):
                    combo = d * N_HEADS + g * GDN_GROUP + hh
                    for piece in range(3):
                        src = piece * (N_QTY * N_COMBO) + qty * N_COMBO + combo
                        expand[d * GDN_NGRP + g, src, qty * w + hh * CHUNK: qty * w + (hh + 1) * CHUNK] = 1.0
    return tri_c, bd, expand


def _recurrent_kernel(qkv_ref, sm3_ref, tri_ref, bd_ref, exp_ref,
                      gx_ref, gsm_ref, wup_ref, bg_ref, gtri_ref, bdv_ref, stm_ref, kqm_ref,
                      o_ref, oc_ref, st_s, pa_s, pb_s, cd_s, acc_s, gst_s, gacc_s):
    seq = qkv_ref.shape[0]
    nchunk = seq // CHUNK
    w = GDN_W
    gla_init, gla_step, gla_finish = _gla_program(gx_ref, gsm_ref, wup_ref, bg_ref, gtri_ref, bdv_ref, stm_ref,
                                                  kqm_ref, oc_ref, gst_s, gacc_s)
    gla_init()
    acc_s[...] = jnp.zeros_like(acc_s)
    st_s[...] = jnp.zeros_like(st_s)
    bd_mask = bd_ref[...] > 0.5

    combos = [(d, grp) for d in range(N_DIRS) for grp in range(GDN_NGRP)]

    def load_chunk(chunk, d, grp):
        rs = pl.ds(pl.multiple_of(chunk * CHUNK, CHUNK), CHUNK)
        return (qkv_ref[rs, grp * w:(grp + 1) * w], qkv_ref[rs, 256 + grp * w:256 + (grp + 1) * w],
                qkv_ref[rs, 512 + grp * w:512 + (grp + 1) * w], sm3_ref[rs, :])

    nc = len(combos)
    head_masks = [(lax.broadcasted_iota(jnp.int32, (1, w), 1) // HEAD_DIM) == hh for hh in range(GDN_GROUP)]

    items = [(d, grp) for _ in range(GDN_WINDOW) for d, grp in combos]
    ni = len(items)

    def transform_stages(raws, outs):
        exs, diffs, kqs = [], [], []
        a_mats, ps, rs_ = [], [], []

        def stage_products():
            ex_all = [jnp.dot(jnp.concatenate([raws(t * nc + c)[3] for t in range(GDN_WINDOW)], axis=0),
                              exp_ref[c], preferred_element_type=F32) for c in range(nc)]
            for n, (d, grp) in enumerate(items):
                q, k, vb, _ = raws(n)
                ex = ex_all[n % nc][(n // nc) * CHUNK:(n // nc + 1) * CHUNK]
                betax, bx = ex[:, 0:w], ex[:, w:2 * w]
                ebx = jnp.exp2(bx)
                b_end = bx[CHUNK - 1:CHUNK] if d == 0 else bx[0:1]
                exs.append([betax, betax * ebx, ebx, jnp.exp2(b_end - bx)])
                diffs.append(bx - jnp.sum(bx * tri_ref[4], axis=0, keepdims=True))
                kbd = jnp.concatenate([jnp.where(hm, k, jnp.zeros_like(k)) for hm in head_masks], axis=0)
                kqs.append(lax.dot_general(jnp.concatenate([k, q], axis=0), kbd, (((1,), (1,)), ((), ())),
                                           preferred_element_type=F32))

        def stage_square():
            tris = [tri_ref[d] > 0.5 for d in range(N_DIRS)]
            stris = [tri_ref[2 + d] > 0.5 for d in range(N_DIRS)]
            for n, (d, grp) in enumerate(items):
                tri = tris[d]
                stri = stris[d]
                decay = jnp.where(tri, jnp.exp2(jnp.where(tri, diffs[n], 0.0)), 0.0)
                low = jnp.where(stri, exs[n][0] * kqs[n][:CHUNK] * decay, 0.0)
                a_mats.append(kqs[n][CHUNK:] * decay)
                ps.append(_mm(low, _block_diag(low, bd_mask, GDN_GROUP)))
                rs_.append(tri_ref[4] - low)

        def stage_level(last):
            for n in range(ni):
                pbd = _block_diag(ps[n], bd_mask, GDN_GROUP)
                if last:
                    rs_[n] = rs_[n] + _mm(rs_[n], pbd)
                else:
                    rp = _mm(jnp.concatenate([rs_[n], ps[n]], axis=0), pbd)
                    rs_[n] = rs_[n] + rp[:CHUNK]
                    ps[n] = rp[CHUNK:]

        def stage_apply():
            for n, (d, grp) in enumerate(items):
                q, k, vb = raws(n)[:3]
                betax, bebx, ebx, edx = exs[n]
                kf = k.astype(F32)
                u = _mm(rs_[n], _block_diag(betax * vb.astype(F32), bd_mask, GDN_GROUP))
                wm = _mm(rs_[n], _block_diag(bebx * kf, bd_mask, GDN_GROUP))
                cd_row = ebx[CHUNK - 1:CHUNK] if d == 0 else ebx[0:1]
                outs.append((a_mats[n], u, _bf(wm), _bf(q.astype(F32) * ebx), _bf(kf * edx), cd_row))

        return ([stage_products, stage_square] + [functools.partial(stage_level, False)] * 4
                + [functools.partial(stage_level, True), stage_apply])

    def store_transformed(c, t):
        a_mat, u, wm, q_dec, k_dec, cd_row = t
        pa_s[2 * c] = a_mat
        pa_s[2 * c + 1] = u
        pb_s[3 * c] = wm
        pb_s[3 * c + 1] = q_dec
        pb_s[3 * c + 2] = k_dec
        cd_s[c] = jnp.broadcast_to(cd_row, (8, w))

    def load_transformed(c):
        return pa_s[2 * c], pa_s[2 * c + 1], pb_s[3 * c], pb_s[3 * c + 1], pb_s[3 * c + 2], cd_s[c][0:1]

    def scan_stages(states, cur, os_):
        tmp = {}

        def stage_ws(t):
            tmp["cur"] = [cur(t * nc + c) for c in range(nc)]
            tmp["ws"] = [_mm(jnp.concatenate([tmp["cur"][c][2], tmp["cur"][c][3]], axis=0), states[c])
                         for c in range(nc)]

        def stage_update(t):
            for c in range(nc):
                a_mat, u, _, _, k_dec, cd_row = tmp["cur"][c]
                v_new = u - tmp["ws"][c][:CHUNK]
                os_.append(tmp["ws"][c][CHUNK:] + _mm(a_mat, _block_diag(v_new, bd_mask, GDN_GROUP)))
                states[c] = cd_row * states[c] + jnp.where(bd_mask, _mm_tn(k_dec, v_new), 0.0)

        stages = []
        for t in range(GDN_WINDOW):
            stages += [functools.partial(stage_ws, t), functools.partial(stage_update, t)]
        return stages

    def chunk_of(j, t, d):
        i = j * GDN_WINDOW + t
        return i if d == 0 else nchunk - 1 - i

    def window_loader(j):
        def load(n):
            d, grp = items[n]
            return load_chunk(chunk_of(j, n // nc, d), d, grp)
        return load

    first = []
    for stage in transform_stages(window_loader(0), first):
        stage()
    for n in range(ni):
        store_transformed(n, first[n])

    nwin = nchunk // GDN_WINDOW

    def step(j, carry):
        states = [st_s[c] for c in range(nc)]
        os_, nexts = [], []
        t_stages = transform_stages(window_loader(jnp.minimum(j + 1, nwin - 1)), nexts)
        s_stages = scan_stages(states, load_transformed, os_)
        g_stages, gla_stores = gla_step(j)
        g_at = {2 * i: g for i, g in enumerate(g_stages)}
        for k in range(max(len(t_stages), len(s_stages))):
            if k < len(t_stages):
                t_stages[k]()
            if k < len(s_stages):
                s_stages[k]()
            if k in g_at:
                g_at[k]()
        for t in range(GDN_WINDOW):
            for c, (d, grp) in enumerate(combos):
                rs = pl.ds(pl.multiple_of(chunk_of(j, t, d) * CHUNK, CHUNK), CHUNK)
                acc_s[rs, grp * w:(grp + 1) * w] += os_[t * nc + c]
        for c in range(nc):
            st_s[c] = states[c]
        for n in range(ni):
            store_transformed(n, nexts[n])
        gla_stores()
        return carry

    lax.fori_loop(0, nwin, step, 0)
    o_ref[...] = _bf(acc_s[...])
    gla_finish()


def _recurrent(gdn_qkv, gdn_sm3, gla_in, small, wup, bg, gdn_consts, gla_consts, batch, seq):
    T = gdn_qkv.shape[0]
    assert GDN_WINDOW == GLA_WINDOW and seq % (CHUNK * GDN_WINDOW) == 0
    tri_c, bd, expand = gdn_consts
    tri, bd_v, st_mask, kq_mask = gla_consts
    whole = lambda b: (b, 0)
    c2 = lambda b: (0, 0)
    c3 = lambda b: (0, 0, 0)
    out = pl.BlockSpec((seq, 256), whole)
    return pl.pallas_call(
        _recurrent_kernel,
        grid=(batch,),
        in_specs=[
            pl.BlockSpec((seq, 768), whole),
            pl.BlockSpec((seq, LANES), whole),
            pl.BlockSpec(tri_c.shape, c3),
            pl.BlockSpec(bd.shape, c2),
            pl.BlockSpec(expand.shape, c3),
            pl.BlockSpec((seq, 512), whole),
            pl.BlockSpec((seq, LANES), whole),
            pl.BlockSpec(wup.shape, c3),
            pl.BlockSpec(bg.shape, c2),
            pl.BlockSpec(tri.shape, c3),
            pl.BlockSpec(bd_v.shape, c2),
            pl.BlockSpec(st_mask.shape, c2),
            pl.BlockSpec(kq_mask.shape, c2),
        ],
        out_specs=[out, out],
        out_shape=[jax.ShapeDtypeStruct((T, 256), BF16), jax.ShapeDtypeStruct((T, 256), BF16)],
        scratch_shapes=[
            pltpu.VMEM((N_DIRS * GDN_NGRP, GDN_W, GDN_W), F32),
            pltpu.VMEM((2 * GDN_WINDOW * N_DIRS * GDN_NGRP, CHUNK, GDN_W), F32),
            pltpu.VMEM((3 * GDN_WINDOW * N_DIRS * GDN_NGRP, CHUNK, GDN_W), BF16),
            pltpu.VMEM((GDN_WINDOW * N_DIRS * GDN_NGRP, 8, GDN_W), F32),
            pltpu.VMEM((seq, 256), F32),
            pltpu.VMEM((N_DIRS, GLA_QW, GLA_VW), F32),
            pltpu.VMEM((seq, 256), F32),
        ],
        compiler_params=pltpu.CompilerParams(
            dimension_semantics=("arbitrary",), vmem_limit_bytes=VMEM_LIMIT),
        name="recurrent",
    )(gdn_qkv, gdn_sm3, tri_c, bd, expand, gla_in, small, wup, bg, tri, bd_v, st_mask, kq_mask)


GLA_QW = N_HEADS * GLA_DK
GLA_VW = N_HEADS * HEAD_DIM
GLA_WINDOW = 4


def _gla_constants():
    i = np.arange(CHUNK)[:, None]
    lane = np.arange(GLA_VW)[None, :]
    j = lane % CHUNK
    tri = np.stack([i >= j, i <= j]).astype(np.float32)
    r = np.arange(GLA_VW)[:, None]
    bd_v = ((r // CHUNK) == (lane // HEAD_DIM)).astype(np.float32)
    rs = np.arange(GLA_QW)[:, None]
    st_mask = ((rs // GLA_DK) == (lane // HEAD_DIM)).astype(np.float32)
    kq_mask = ((r // CHUNK) == (np.arange(GLA_QW)[None, :] // GLA_DK)).astype(np.float32)
    return tri, bd_v, st_mask, kq_mask


def _gla_program(x_ref, sm_ref, wup_ref, bg_ref, tri_ref, bdv_ref, stm_ref, kqm_ref, o_ref, st_s, acc_s):
    seq = x_ref.shape[0]
    nchunk = seq // CHUNK
    bdv_mask = bdv_ref[...] > 0.5
    st_mask = stm_ref[...] > 0.5
    kq_mask = kqm_ref[...] > 0.5

    def init():
        acc_s[...] = jnp.zeros_like(acc_s)
        st_s[...] = jnp.zeros_like(st_s)

    def rows_of(chunk):
        return pl.ds(pl.multiple_of(chunk * CHUNK, CHUNK), CHUNK)

    def load_chunk(chunk):
        rs = rows_of(chunk)
        return (x_ref[rs, 0:GLA_QW].astype(F32), x_ref[rs, GLA_QW:2 * GLA_QW].astype(F32),
                x_ref[rs, 2 * GLA_QW:2 * GLA_QW + GLA_VW], sm_ref[rs, :])

    items = [(t, d) for t in range(GLA_WINDOW) for d in range(N_DIRS)]
    ni = len(items)

    def chunk_of(jw, t, d):
        i = jw * GLA_WINDOW + t
        return i if d == 0 else nchunk - 1 - i

    def step(jw):
        raws, logits, q_es, attns, dsts, cds, outs = [], [], [], [], [], [], []
        states = []

        def stage_logits():
            for n, (t, d) in enumerate(items):
                raws.append(load_chunk(chunk_of(jw, t, d)))
                logits.append(jnp.dot(_bf(raws[n][3]), wup_ref[d], preferred_element_type=F32)
                              + bg_ref[d:d + 1, :])

        def stage_intra():
            tri_masks = [tri_ref[d] > 0.5 for d in range(N_DIRS)]
            for n, (t, d) in enumerate(items):
                q, k, v, _ = raws[n]
                gk = (jnp.minimum(logits[n], 0.0) - jnp.log1p(jnp.exp(-jnp.abs(logits[n])))) \
                    * (LOG2_E / GLA_NORMALIZER)
                b = _cumsum_rows(gk, reverse=(d == 1))
                b_end = b[CHUNK - 1:CHUNK] if d == 0 else b[0:1]
                q_e = q * (GLA_DK ** -0.5) * jnp.exp2(b)
                keb = _bf(k * jnp.exp2(-b))
                kebd = jnp.where(kq_mask, jnp.concatenate([keb] * N_HEADS, axis=0),
                                 jnp.zeros((GLA_VW, GLA_QW), BF16))
                q_es.append(_bf(q_e))
                attns.append(jnp.where(tri_masks[d], _mm_nt(q_es[n], kebd), 0.0))
                dsts.append(jnp.where(st_mask, _mm_tn(k * jnp.exp2(b_end - b), v), 0.0))
                cd_col = jnp.transpose(jnp.broadcast_to(jnp.exp2(b_end), (GLA_QW, GLA_QW)))
                cds.append(jnp.concatenate([cd_col, cd_col], axis=1))

        def stage_out():
            states.extend(st_s[d] for d in range(N_DIRS))
            for n in range(ni):
                outs.append(_mm(attns[n], _block_diag(raws[n][2], bdv_mask, N_HEADS)))
            for n, (t, d) in enumerate(items):
                outs[n] = outs[n] + _mm(q_es[n], states[d])
                states[d] = cds[n] * states[d] + dsts[n]

        def stores():
            for n, (t, d) in enumerate(items):
                acc_s[rows_of(chunk_of(jw, t, d)), :] += outs[n]
            for d in range(N_DIRS):
                st_s[d] = states[d]

        return [stage_logits, stage_intra, stage_out], stores

    def finish():
        o_ref[...] = _bf(acc_s[...])

    return init, step, finish


def _outproj_kernel(ya_ref, ob_ref, oc_ref, ym_ref, gates_ref, gb_ref, gc_ref, ones_ref, w_ref, h_ref,
                    fg_ref, o_ref, *, final):
    ones256 = ones_ref[...]

    ob = ob_ref[...].astype(F32)
    oc = oc_ref[...].astype(F32)
    ssb, ssc = _group_sums([ob * ob, oc * oc], ones256)
    yb = ob * lax.rsqrt(ssb * (1.0 / HEAD_DIM) + EPS) * gb_ref[...] * gates_ref[:, 0:256].astype(F32)
    yc = oc * lax.rsqrt(ssc * (1.0 / HEAD_DIM) + EPS) * gc_ref[...] * gates_ref[:, 256:512].astype(F32)
    y = jnp.dot(ya_ref[...], w_ref[0:256, :], preferred_element_type=F32)
    y = y + jnp.dot(_bf(yb), w_ref[256:512, :], preferred_element_type=F32)
    y = y + jnp.dot(_bf(yc), w_ref[512:768, :], preferred_element_type=F32)
    y = y + jnp.dot(ym_ref[...], w_ref[768:1024, :], preferred_element_type=F32)
    hn = h_ref[...] + y
    if final:
        hn = hn * lax.rsqrt(jnp.mean(hn * hn, axis=-1, keepdims=True) + EPS) * fg_ref[...]
    o_ref[...] = hn


def _outproj(ya, ob, oc, ym, gates, gb, gc, ones256, w, h, fg, tm, final):
    T, D = h.shape
    row = lambda i: (i, 0)
    const = lambda i: (0, 0)
    return pl.pallas_call(
        functools.partial(_outproj_kernel, final=final),
        grid=(T // tm,),
        in_specs=[
            pl.BlockSpec((tm, 256), row), pl.BlockSpec((tm, 256), row),
            pl.BlockSpec((tm, 256), row), pl.BlockSpec((tm, 256), row),
            pl.BlockSpec((tm, 512), lambda i: (i, 1)),
            pl.BlockSpec((1, 256), const), pl.BlockSpec((1, 256), const),
            pl.BlockSpec((256, 256), const),
            pl.BlockSpec((D, D), const),
            pl.BlockSpec((tm, D), row),
            pl.BlockSpec((1, D), const),
        ],
        out_specs=pl.BlockSpec((tm, D), row),
        out_shape=jax.ShapeDtypeStruct((T, D), F32),
        compiler_params=pltpu.CompilerParams(
            dimension_semantics=("arbitrary",), vmem_limit_bytes=VMEM_LIMIT),
        name="outproj_final" if final else "outproj",
    )(ya, ob, oc, ym, gates, gb, gc, ones256, w, h, fg)


def _block_ones(n, group):
    i = np.arange(n)
    return jnp.asarray((i[:, None] // group) == (i[None, :] // group), dtype=BF16)


def _head_perm_cols():
    return np.concatenate([np.arange(h * HEAD_DIM, (h + 1) * HEAD_DIM) for h in ATT_HEAD_PERM])


def _prep_w_in(w_in):
    o = np.cumsum([0, 256, 128, 128, 256, 768, 8, 8, 256, 128, 128, 256, 32, 256, 256, 256])
    (a_q, a_k, a_v, a_g, d_qkv, d_beta, d_alpha, d_gate, l_q, l_k, l_v, l_low, l_gate, m_q, m_gate) = [
        w_in[:, :, o[i]:o[i + 1]] for i in range(15)]
    perm = _head_perm_cols()
    pad = jnp.zeros(w_in.shape[:2] + (N_COLS - C_SM - 48,), w_in.dtype)
    cols = [a_q[:, :, perm], a_k, a_v, a_g[:, :, perm], d_qkv, d_gate, l_q, l_k, l_v, l_gate, m_q, m_gate,
            d_beta, d_alpha, l_low, pad]
    return jnp.concatenate(cols, axis=-1).astype(BF16)


def _rope_tables(seq):
    rows = seq // GRID_W
    row_pos = jnp.repeat(jnp.arange(rows, dtype=F32), GRID_W)
    col_pos = jnp.tile(jnp.arange(GRID_W, dtype=F32), rows)
    half = HEAD_DIM // 4
    inv_freq = ROPE_THETA ** (-jnp.arange(0, 2 * half, 2, dtype=F32) / (2 * half))
    ang = jnp.stack([row_pos, col_pos], axis=-1)[..., None] * inv_freq
    cos, sin = jnp.cos(ang), jnp.sin(ang)
    cos_h = jnp.concatenate([cos, cos], axis=-1).reshape(seq, HEAD_DIM)
    sin_h = jnp.concatenate([-sin, sin], axis=-1).reshape(seq, HEAD_DIM)
    return jnp.tile(cos_h, (1, 2)), jnp.tile(sin_h, (1, 2))


def _trunk(x, mem, p, consts):
    batch, seq, d_model = x.shape
    T = batch * seq
    depth = p["w_in"].shape[0]
    tm = 1024 if seq % 1024 == 0 else (512 if seq % 512 == 0 else seq)
    tq = 256 if seq % 256 == 0 else seq
    kvc = 512 if seq % 512 == 0 else seq
    cos_t, sin_t = _rope_tables(seq)
    mem2d = mem.reshape(batch * mem.shape[1], d_model)
    memkv = _memkv(mem2d, p["mem_norm_g"], p["w_mem_kv"], min(512, mem2d.shape[0]))
    h = x.reshape(T, d_model)
    for d in range(depth):
        qkv, gates, gdn_qkv, gla_in, mq, small, gdn_sm3 = _inproj(
            h, p["norm_g"][d], p["w_in"][d], cos_t, sin_t, p["qkg"][d], consts["ones384"],
            p["conv_w"][d], p["gdn_par"][d], consts["ones256"], seq, tm)
        ya, ym = _attention(qkv, mq, memkv[d], gates, batch, seq, tq, kvc)
        ob, oc = _recurrent(gdn_qkv, gdn_sm3, gla_in, small, p["wup"][d], p["bg"][d],
                            consts["gdn"], consts["gla"], batch, seq)
        h = _outproj(ya, ob, oc, ym, gates, p["gb"][d], p["gc"][d],
                     consts["ones256"], p["w_out"][d], h, p["final_g"], tm, final=(d == depth - 1))
    return h.reshape(batch, seq, d_model)


def kernel(x_prompt, x_sample, mem_prompt, mem_sample, norm_g, w_in, att_q_norm_g, att_k_norm_g, gdn_conv_w,
           gdn_a_log, gdn_dt_bias, gdn_out_norm_g, gla_w_gate_up, gla_b_gate, gla_out_norm_g, mem_norm_g,
           w_mem_kv, w_out, final_norm_g):
    depth, d_model = norm_g.shape
    perm = _head_perm_cols()
    w_out_p = jnp.concatenate([w_out[:, :256][:, perm], w_out[:, 256:]], axis=1).astype(BF16)
    qkg = jnp.concatenate([jnp.tile(att_q_norm_g, (1, N_HEADS)), jnp.tile(att_k_norm_g, (1, KV_HEADS))], axis=1)
    zeros8 = jnp.zeros((depth, N_COMBO), F32)
    par_a = jnp.concatenate([zeros8, -jnp.exp(gdn_a_log.reshape(depth, N_COMBO)) * LOG2_E,
                             jnp.zeros((depth, LANES - 2 * N_COMBO), F32)], axis=1)
    par_b = jnp.concatenate([zeros8, gdn_dt_bias.reshape(depth, N_COMBO),
                             jnp.zeros((depth, LANES - 2 * N_COMBO), F32)], axis=1)
    gdn_par = jnp.concatenate([par_a[:, None], par_b[:, None], jnp.zeros((depth, 6, LANES), F32)], axis=1)
    conv_w = jnp.concatenate([gdn_conv_w, jnp.zeros((depth, 5, gdn_conv_w.shape[2]), F32)], axis=1)
    wup = jnp.zeros((depth, N_DIRS, LANES, GLA_QW), F32)
    for d in range(N_DIRS):
        lo = 2 * N_COMBO + d * GLA_RANK
        wup = wup.at[:, d, lo:lo + GLA_RANK, :].set(gla_w_gate_up[:, d])
    p = {
        "norm_g": norm_g.reshape(depth, 1, d_model),
        "w_in": _prep_w_in(w_in),
        "qkg": qkg.reshape(depth, 1, 384),
        "conv_w": conv_w,
        "gdn_par": gdn_par,
        "wup": wup.astype(BF16),
        "bg": jnp.concatenate([gla_b_gate, jnp.zeros((depth, 6, GLA_QW), F32)], axis=1),
        "gb": jnp.tile(gdn_out_norm_g, (1, N_HEADS)).reshape(depth, 1, 256),
        "gc": jnp.tile(gla_out_norm_g, (1, N_HEADS)).reshape(depth, 1, 256),
        "mem_norm_g": mem_norm_g.reshape(depth, 1, d_model),
        "w_mem_kv": w_mem_kv.astype(BF16),
        "w_out": w_out_p,
        "final_g": final_norm_g.reshape(1, d_model),
    }
    tri_c, bd, expand = _gdn_constants()
    consts = {
        "ones384": _block_ones(384, HEAD_DIM),
        "ones256": _block_ones(256, HEAD_DIM),
        "gdn": (jnp.asarray(tri_c), jnp.asarray(bd), jnp.asarray(expand, dtype=BF16)),
        "gla": tuple(jnp.asarray(a) for a in _gla_constants()),
    }
    y_prompt = _trunk(x_prompt, mem_prompt, p, consts)
    y_sample = _trunk(x_sample, mem_sample, p, consts)
    return (y_prompt, y_sample)
```

```python
import functools

import numpy as np
import jax
import jax.numpy as jnp
from jax import lax
from jax.experimental import pallas as pl
from jax.experimental.pallas import tpu as pltpu

F32 = jnp.float32
BF16 = jnp.bfloat16

N_HEADS = 4
HEAD_DIM = 64
KV_HEADS = 2
GRID_W = 64
ROPE_THETA = 10000.0
CHUNK = 64
GLA_DK = 32
GLA_RANK = 16
GLA_NORMALIZER = 16.0
EPS = 1e-6
N_DIRS = 2
LANES = 128
GDN_GROUP = 2
GDN_WINDOW = 4
VMEM_LIMIT = 56 * 1024 * 1024

C_AQ, C_AK, C_AV, C_AG = 0, 256, 384, 512
C_DQKV, C_DG = 768, 1536
C_LQ, C_LK, C_LV, C_LG = 1792, 1920, 2048, 2304
C_MQ, C_MG = 2560, 2816
C_SM = 3072
N_COLS = 3200
ATT_HEAD_PERM = (0, 2, 1, 3)
ATT_VT_ROWS = 2 * HEAD_DIM
LOG2_E = float(np.log2(np.e))
ATT_Q_SCALE = HEAD_DIM ** -0.5 * LOG2_E


def _bf(x):
    return x.astype(BF16)


def _mm(a, b):
    return jnp.dot(_bf(a), _bf(b), preferred_element_type=F32)


def _mm_nt(a, b):
    return lax.dot_general(_bf(a), _bf(b), (((1,), (1,)), ((), ())), preferred_element_type=F32)


def _mm_tn(a, b):
    return lax.dot_general(_bf(a), _bf(b), (((0,), (0,)), ((), ())), preferred_element_type=F32)


def _split2(x):
    hi = _bf(x)
    lo = _bf(x - hi.astype(F32))
    return hi, lo


def _split3(x):
    hi = _bf(x).astype(F32)
    r = x - hi
    mid = _bf(r).astype(F32)
    lo = _bf(r - mid).astype(F32)
    return hi, mid, lo


def _group_sums(xs, ones_blocks):
    rows = xs[0].shape[0]
    parts = []
    for x in xs:
        parts.extend(_split2(x))
    s = jnp.dot(jnp.concatenate(parts, axis=0), ones_blocks, preferred_element_type=F32)
    return [s[2 * i * rows:(2 * i + 1) * rows] + s[(2 * i + 1) * rows:(2 * i + 2) * rows] for i in range(len(xs))]


def _silu(x):
    return x * (0.5 * jnp.tanh(0.5 * x) + 0.5)


def _softplus(x):
    return jnp.maximum(x, 0.0) + jnp.log1p(jnp.exp(-jnp.abs(x)))


def _cumsum_rows(x, reverse):
    n = x.shape[0]
    row = lax.broadcasted_iota(jnp.int32, x.shape, 0)
    s = 1
    while s < n:
        if reverse:
            x = x + jnp.where(row < n - s, pltpu.roll(x, n - s, axis=0), 0.0)
        else:
            x = x + jnp.where(row >= s, pltpu.roll(x, s, axis=0), 0.0)
        s *= 2
    return x


def _block_diag(x, mask, reps):
    xb = _bf(x)
    t = jnp.concatenate([xb] * reps, axis=0)
    return jnp.where(mask, t, jnp.zeros_like(t))


def _inproj_kernel(h_ref, hprev_ref, hnext_ref, g_ref, w_ref, cos_ref, sin_ref, qkg_ref, ones_ref,
                   cw_ref, par_ref, ones256_ref,
                   qkv_ref, gates_ref, gdn_ref, gla_ref, mq_ref, small_ref, sm3_ref, *, nseq):
    tm = h_ref.shape[0]

    def normed(x):
        return _bf(x * lax.rsqrt(jnp.mean(x * x, axis=-1, keepdims=True) + EPS) * g_ref[...])

    halo = jnp.concatenate([hprev_ref[0], hnext_ref[0]], axis=0)
    hp = jnp.dot(normed(halo), w_ref[:, C_DQKV:C_DG], preferred_element_type=F32)
    xn = normed(h_ref[...])
    groups = ((C_DQKV, C_DG), (C_SM, N_COLS), (0, C_DQKV), (C_DG, C_SM))
    pieces = [jnp.dot(xn, w_ref[:, a:b], preferred_element_type=F32) for a, b in groups]

    class _Proj:
        def __getitem__(self, idx):
            rows, cols = idx
            for (a, b), piece in zip(groups, pieces):
                if a <= cols.start and cols.stop <= b:
                    return piece[rows, cols.start - a:cols.stop - a]
            raise IndexError(cols)

    proj = _Proj()

    pos = pl.program_id(0) % nseq
    prev_row = hp[7:8] * jnp.where(pos > 0, 1.0, 0.0)
    next_row = hp[8:9] * jnp.where(pos < nseq - 1, 1.0, 0.0)
    xg = proj[:, C_DQKV:C_DG]
    trow = lax.broadcasted_iota(jnp.int32, (tm, 1), 0)
    x_prev = jnp.where(trow == 0, prev_row, pltpu.roll(xg, 1, axis=0))
    x_next = jnp.where(trow == tm - 1, next_row, pltpu.roll(xg, tm - 1, axis=0))
    y = x_prev * cw_ref[0:1, :] + xg * cw_ref[1:2, :]
    y = _silu(y + x_next * cw_ref[2:3, :])
    sss = _group_sums([y[:, 0:256] * y[:, 0:256], y[:, 256:512] * y[:, 256:512]], ones256_ref[...])
    gdn_ref[:, 0:256] = _bf(y[:, 0:256] * lax.rsqrt(sss[0] + EPS) * (HEAD_DIM ** -0.5))
    gdn_ref[:, 256:512] = _bf(y[:, 256:512] * lax.rsqrt(sss[1] + EPS))
    gdn_ref[:, 512:768] = _bf(y[:, 512:768])

    lane = lax.broadcasted_iota(jnp.int32, (1, LANES), 1)
    lane8 = lane < N_COMBO
    fwd_lane = lane < N_HEADS
    for c in range(tm // CHUNK):
        sm = proj[c * CHUNK:(c + 1) * CHUNK, C_SM:N_COLS]
        beta = 0.5 * jnp.tanh(0.5 * sm) + 0.5
        g_all = par_ref[0:1, :] * _softplus(sm + par_ref[1:2, :])
        g = pltpu.roll(g_all, LANES - N_COMBO, axis=1)
        b = jnp.where(fwd_lane, _cumsum_rows(g, reverse=False), _cumsum_rows(g, reverse=True))
        zero = jnp.zeros_like(b)
        q2 = jnp.where(lane8, beta, zero) + pltpu.roll(jnp.where(lane8, b, zero), N_COMBO, axis=1)
        hi, mid, lo3 = _split3(q2)
        sm3_ref[c * CHUNK:(c + 1) * CHUNK, :] = _bf(
            hi + pltpu.roll(mid, 2 * N_COMBO, axis=1) + pltpu.roll(lo3, 4 * N_COMBO, axis=1))

    qk = proj[:, C_AQ:C_AV]
    ms = _group_sums([qk * qk], ones_ref[...])[0] * (1.0 / HEAD_DIM)
    qkn = qk * lax.rsqrt(ms + EPS) * qkg_ref[...]
    cos = cos_ref[...]
    sin = sin_ref[...]
    lane = lax.broadcasted_iota(jnp.int32, (1, LANES), 1)
    first_half = (lane % 32) < 16
    outs = []
    for c in range(3):
        xc = qkn[:, c * LANES:(c + 1) * LANES]
        partner = jnp.where(first_half, pltpu.roll(xc, LANES - 16, axis=1), pltpu.roll(xc, 16, axis=1))
        outs.append(xc * cos + partner * sin)
    qkv_ref[:, 0:128] = _bf(outs[0] * ATT_Q_SCALE)
    qkv_ref[:, 128:256] = _bf(outs[1] * ATT_Q_SCALE)
    qkv_ref[:, 256:384] = _bf(outs[2])
    qkv_ref[:, 384:512] = _bf(proj[:, C_AV:C_AG])

    gates_ref[:, 0:256] = _bf(_silu(proj[:, C_AG:C_AG + 256]))
    gates_ref[:, 256:512] = _bf(_silu(proj[:, C_MG:C_MG + 256]))
    gates_ref[:, 512:768] = _bf(_silu(proj[:, C_DG:C_DG + 256]))
    gates_ref[:, 768:1024] = _bf(_silu(proj[:, C_LG:C_LG + 256]))
    gla_ref[...] = _bf(proj[:, C_LQ:C_LG])
    mq_ref[...] = _bf(proj[:, C_MQ:C_MG] * ATT_Q_SCALE)
    small_ref[...] = proj[:, C_SM:N_COLS]


def _inproj(h, g, w, cos_t, sin_t, qkg, ones384, conv_w, par, ones256, seq, tm):
    T, D = h.shape
    nseq = seq // tm
    row = lambda i: (i, 0)
    const = lambda i: (0, 0)
    h8 = h.reshape(T // 8, 8, D)
    g8 = tm // 8
    return pl.pallas_call(
        functools.partial(_inproj_kernel, nseq=nseq),
        grid=(T // tm,),
        in_specs=[
            pl.BlockSpec((tm, D), row),
            pl.BlockSpec((1, 8, D), lambda i: (jnp.maximum(i * g8 - 1, 0), 0, 0)),
            pl.BlockSpec((1, 8, D), lambda i: (jnp.minimum((i + 1) * g8, T // 8 - 1), 0, 0)),
            pl.BlockSpec((1, D), const),
            pl.BlockSpec((D, N_COLS), const),
            pl.BlockSpec((tm, LANES), lambda i: (i % nseq, 0)),
            pl.BlockSpec((tm, LANES), lambda i: (i % nseq, 0)),
            pl.BlockSpec((1, 384), const),
            pl.BlockSpec((384, 384), const),
            pl.BlockSpec((8, 768), const),
            pl.BlockSpec((8, LANES), const),
            pl.BlockSpec((256, 256), const),
        ],
        out_specs=[
            pl.BlockSpec((tm, 512), row),
            pl.BlockSpec((tm, 1024), row),
            pl.BlockSpec((tm, 768), row),
            pl.BlockSpec((tm, 512), row),
            pl.BlockSpec((tm, 256), row),
            pl.BlockSpec((tm, LANES), row),
            pl.BlockSpec((tm, LANES), row),
        ],
        out_shape=[
            jax.ShapeDtypeStruct((T, 512), BF16),
            jax.ShapeDtypeStruct((T, 1024), BF16),
            jax.ShapeDtypeStruct((T, 768), BF16),
            jax.ShapeDtypeStruct((T, 512), BF16),
            jax.ShapeDtypeStruct((T, 256), BF16),
            jax.ShapeDtypeStruct((T, LANES), F32),
            jax.ShapeDtypeStruct((T, LANES), BF16),
        ],
        compiler_params=pltpu.CompilerParams(
            dimension_semantics=("arbitrary",), vmem_limit_bytes=VMEM_LIMIT),
        name="inproj",
    )(h, h8, h8, g, w, cos_t, sin_t, qkg, ones384, conv_w, par, ones256)


def _memkv_kernel(m_ref, g_ref, w_ref, o_ref):
    x = m_ref[...]
    xn = x * lax.rsqrt(jnp.mean(x * x, axis=-1, keepdims=True) + EPS) * g_ref[0]
    o_ref[0] = _bf(jnp.dot(_bf(xn), w_ref[0], preferred_element_type=F32))


def _memkv(mem2d, g, w, tr):
    R, D = mem2d.shape
    depth = w.shape[0]
    return pl.pallas_call(
        _memkv_kernel,
        grid=(depth, R // tr),
        in_specs=[
            pl.BlockSpec((tr, D), lambda d, i: (i, 0)),
            pl.BlockSpec((1, 1, D), lambda d, i: (d, 0, 0)),
            pl.BlockSpec((1, D, 512), lambda d, i: (d, 0, 0)),
        ],
        out_specs=pl.BlockSpec((1, tr, 512), lambda d, i: (d, i, 0)),
        out_shape=jax.ShapeDtypeStruct((depth, R, 512), BF16),
        compiler_params=pltpu.CompilerParams(
            dimension_semantics=("arbitrary", "arbitrary"), vmem_limit_bytes=VMEM_LIMIT),
        name="memkv",
    )(mem2d, g, w)


def _attn_kernel(q_ref, k_ref, v_ref, mq_ref, km_ref, vm_ref, ga_ref, gm_ref, ya_ref, ym_ref, vt_s, *, kvc):
    tq = q_ref.shape[0]
    seq = k_ref.shape[0]
    lane = lax.broadcasted_iota(jnp.int32, (1, LANES), 1)
    lo = lane < HEAD_DIM

    def stack_heads(x):
        z = jnp.zeros_like(x)
        return [jnp.where(lo, x, z), jnp.where(lo, z, x)]

    @pl.when(pl.program_id(1) == 0)
    def _():
        ones = jnp.ones((ATT_VT_ROWS - HEAD_DIM, kvc), F32)
        for c in range(seq // kvc):
            vt = jnp.transpose(v_ref[c * kvc:(c + 1) * kvc, :].astype(F32))
            vt_s[0, c] = _bf(jnp.concatenate([vt[0:HEAD_DIM], ones], axis=0))
            vt_s[1, c] = _bf(jnp.concatenate([vt[HEAD_DIM:LANES], ones], axis=0))

    q = q_ref[...]
    qm = stack_heads(q[:, :LANES]) + stack_heads(q[:, LANES:])
    n_heads = len(qm)

    def scores(c, h):
        return lax.dot_general(k_ref[c * kvc:(c + 1) * kvc, :], qm[h], (((1,), (1,)), ((), ())),
                               preferred_element_type=F32)

    ms = [jnp.full((1, tq), -jnp.inf, F32) for _ in range(n_heads)]
    accs = [jnp.zeros((ATT_VT_ROWS, tq), F32) for _ in range(n_heads)]
    nsub = seq // kvc
    sts = [scores(0, h) for h in range(n_heads)]
    for c in range(nsub):
        for h in range(n_heads):
            nxt = scores(c + 1, h) if c + 1 < nsub else None
            m_new = jnp.maximum(ms[h], jnp.max(sts[h], axis=0, keepdims=True))
            p = _bf(jnp.exp2(sts[h] - m_new))
            accs[h] = jnp.exp2(ms[h] - m_new) * accs[h] + jnp.dot(vt_s[h % 2, c], p, preferred_element_type=F32)
            ms[h], sts[h] = m_new, nxt
    o_t = [accs[h][0:HEAD_DIM] / accs[h][HEAD_DIM:HEAD_DIM + 1] for h in range(n_heads)]
    ya = jnp.transpose(jnp.concatenate(o_t, axis=0))
    ya_ref[...] = _bf(ya * ga_ref[...].astype(F32))

    mq = mq_ref[...]
    outs = []
    for c in range(2):
        qc = jnp.concatenate(stack_heads(mq[:, c * LANES:(c + 1) * LANES]), axis=0)
        kc = km_ref[:, c * LANES:(c + 1) * LANES]
        vc = vm_ref[:, c * LANES:(c + 1) * LANES]
        s = lax.dot_general(qc, kc, (((1,), (1,)), ((), ())), preferred_element_type=F32)
        p = jnp.exp2(s - jnp.max(s, axis=-1, keepdims=True))
        om = jnp.dot(_bf(p), vc, preferred_element_type=F32) / jnp.sum(p, axis=-1, keepdims=True)
        outs.append(jnp.where(lo, om[0:tq], om[tq:2 * tq]))
    ym_ref[...] = _bf(jnp.concatenate(outs, axis=1) * gm_ref[...].astype(F32))


def _attention(qkv, mq, memkv, gates, batch, seq, tq, kvc):
    T = qkv.shape[0]
    nq = seq // tq
    mem_len = memkv.shape[0] // batch
    qrow = lambda b, i: (b * nq + i, 0)
    return pl.pallas_call(
        functools.partial(_attn_kernel, kvc=kvc),
        grid=(batch, nq),
        in_specs=[
            pl.BlockSpec((tq, 256), qrow),
            pl.BlockSpec((seq, LANES), lambda b, i: (b, 2)),
            pl.BlockSpec((seq, LANES), lambda b, i: (b, 3)),
            pl.BlockSpec((tq, 256), qrow),
            pl.BlockSpec((mem_len, 256), lambda b, i: (b, 0)),
            pl.BlockSpec((mem_len, 256), lambda b, i: (b, 1)),
            pl.BlockSpec((tq, 256), lambda b, i: (b * nq + i, 0)),
            pl.BlockSpec((tq, 256), lambda b, i: (b * nq + i, 1)),
        ],
        out_specs=[pl.BlockSpec((tq, 256), qrow), pl.BlockSpec((tq, 256), qrow)],
        out_shape=[jax.ShapeDtypeStruct((T, 256), BF16), jax.ShapeDtypeStruct((T, 256), BF16)],
        scratch_shapes=[pltpu.VMEM((KV_HEADS, seq // kvc, ATT_VT_ROWS, kvc), BF16)],
        compiler_params=pltpu.CompilerParams(
            dimension_semantics=("arbitrary", "arbitrary"), vmem_limit_bytes=VMEM_LIMIT),
        name="attention",
    )(qkv, qkv, qkv, mq, memkv, memkv, gates, gates)


GDN_W = GDN_GROUP * HEAD_DIM
GDN_NGRP = N_HEADS // GDN_GROUP
N_COMBO = N_DIRS * N_HEADS
N_QTY = 2


def _gdn_constants():
    w = GDN_W
    i = np.arange(CHUNK)[:, None]
    lane = np.arange(w)[None, :]
    j = lane % CHUNK
    tri = np.stack([i >= j, i <= j]).astype(np.float32)
    stri = np.stack([i > j, i < j]).astype(np.float32)
    eye = (i == j).astype(np.float32)[None]
    tri_c = np.concatenate([tri, stri, eye], axis=0)
    r = np.arange(w)[:, None]
    bd = ((r // CHUNK) == (lane // CHUNK)).astype(np.float32)
    expand = np.zeros((N_DIRS * GDN_NGRP, LANES, N_QTY * w), np.float32)
    for d in range(N_DIRS):
        for g in range(GDN_NGRP):
            for qty in range(N_QTY):
                for hh in range(GDN_GROUP):
                    combo = d * N_HEADS + g * GDN_GROUP + hh
                    for piece in range(3):
                        src = piece * (N_QTY * N_COMBO) + qty * N_COMBO + combo
                        expand[d * GDN_NGRP + g, src, qty * w + hh * CHUNK: qty * w + (hh + 1) * CHUNK] = 1.0
    return tri_c, bd, expand


def _recurrent_kernel(qkv_ref, sm3_ref, tri_ref, bd_ref, exp_ref,
                      gx_ref, gsm_ref, wup_ref, bg_ref, gtri_ref, bdv_ref, stm_ref, kqm_ref,
                      o_ref, oc_ref, st_s, pa_s, pb_s, cd_s, acc_s, gst_s, gacc_s):
    seq = qkv_ref.shape[0]
    nchunk = seq // CHUNK
    w = GDN_W
    gla_init, gla_step, gla_finish = _gla_program(gx_ref, gsm_ref, wup_ref, bg_ref, gtri_ref, bdv_ref, stm_ref,
                                                  kqm_ref, oc_ref, gst_s, gacc_s)
    gla_init()
    acc_s[...] = jnp.zeros_like(acc_s)
    st_s[...] = jnp.zeros_like(st_s)
    bd_mask = bd_ref[...] > 0.5

    combos = [(d, grp) for d in range(N_DIRS) for grp in range(GDN_NGRP)]

    def load_chunk(chunk, d, grp):
        rs = pl.ds(pl.multiple_of(chunk * CHUNK, CHUNK), CHUNK)
        return (qkv_ref[rs, grp * w:(grp + 1) * w], qkv_ref[rs, 256 + grp * w:256 + (grp + 1) * w],
                qkv_ref[rs, 512 + grp * w:512 + (grp + 1) * w], sm3_ref[rs, :])

    nc = len(combos)
    head_masks = [(lax.broadcasted_iota(jnp.int32, (1, w), 1) // HEAD_DIM) == hh for hh in range(GDN_GROUP)]

    items = [(d, grp) for _ in range(GDN_WINDOW) for d, grp in combos]
    ni = len(items)

    def transform_stages(raws, outs):
        exs, diffs, kqs = [], [], []
        a_mats, ps, rs_ = [], [], []

        def stage_products():
            ex_all = [jnp.dot(jnp.concatenate([raws(t * nc + c)[3] for t in range(GDN_WINDOW)], axis=0),
                              exp_ref[c], preferred_element_type=F32) for c in range(nc)]
            for n, (d, grp) in enumerate(items):
                q, k, vb, _ = raws(n)
                ex = ex_all[n % nc][(n // nc) * CHUNK:(n // nc + 1) * CHUNK]
                betax, bx = ex[:, 0:w], ex[:, w:2 * w]
                ebx = jnp.exp2(bx)
                b_end = bx[CHUNK - 1:CHUNK] if d == 0 else bx[0:1]
                exs.append([betax, betax * ebx, ebx, jnp.exp2(b_end - bx)])
                diffs.append(bx - jnp.sum(bx * tri_ref[4], axis=0, keepdims=True))
                kbd = jnp.concatenate([jnp.where(hm, k, jnp.zeros_like(k)) for hm in head_masks], axis=0)
                kqs.append(lax.dot_general(jnp.concatenate([k, q], axis=0), kbd, (((1,), (1,)), ((), ())),
                                           preferred_element_type=F32))

        def stage_square():
            tris = [tri_ref[d] > 0.5 for d in range(N_DIRS)]
            stris = [tri_ref[2 + d] > 0.5 for d in range(N_DIRS)]
            for n, (d, grp) in enumerate(items):
                tri = tris[d]
                stri = stris[d]
                decay = jnp.where(tri, jnp.exp2(jnp.where(tri, diffs[n], 0.0)), 0.0)
                low = jnp.where(stri, exs[n][0] * kqs[n][:CHUNK] * decay, 0.0)
                a_mats.append(kqs[n][CHUNK:] * decay)
                ps.append(_mm(low, _block_diag(low, bd_mask, GDN_GROUP)))
                rs_.append(tri_ref[4] - low)

        def stage_level(last):
            for n in range(ni):
                pbd = _block_diag(ps[n], bd_mask, GDN_GROUP)
                if last:
                    rs_[n] = rs_[n] + _mm(rs_[n], pbd)
                else:
                    rp = _mm(jnp.concatenate([rs_[n], ps[n]], axis=0), pbd)
                    rs_[n] = rs_[n] + rp[:CHUNK]
                    ps[n] = rp[CHUNK:]

        def stage_apply():
            for n, (d, grp) in enumerate(items):
                q, k, vb = raws(n)[:3]
                betax, bebx, ebx, edx = exs[n]
                kf = k.astype(F32)
                u = _mm(rs_[n], _block_diag(betax * vb.astype(F32), bd_mask, GDN_GROUP))
                wm = _mm(rs_[n], _block_diag(bebx * kf, bd_mask, GDN_GROUP))
                cd_row = ebx[CHUNK - 1:CHUNK] if d == 0 else ebx[0:1]
                outs.append((a_mats[n], u, _bf(wm), _bf(q.astype(F32) * ebx), _bf(kf * edx), cd_row))

        return ([stage_products, stage_square] + [functools.partial(stage_level, False)] * 4
                + [functools.partial(stage_level, True), stage_apply])

    def store_transformed(c, t):
        a_mat, u, wm, q_dec, k_dec, cd_row = t
        pa_s[2 * c] = a_mat
        pa_s[2 * c + 1] = u
        pb_s[3 * c] = wm
        pb_s[3 * c + 1] = q_dec
        pb_s[3 * c + 2] = k_dec
        cd_s[c] = jnp.broadcast_to(cd_row, (8, w))

    def load_transformed(c):
        return pa_s[2 * c], pa_s[2 * c + 1], pb_s[3 * c], pb_s[3 * c + 1], pb_s[3 * c + 2], cd_s[c][0:1]

    def scan_stages(states, cur, os_):
        tmp = {}

        def stage_ws(t):
            tmp["cur"] = [cur(t * nc + c) for c in range(nc)]
            tmp["ws"] = [_mm(jnp.concatenate([tmp["cur"][c][2], tmp["cur"][c][3]], axis=0), states[c])
                         for c in range(nc)]

        def stage_update(t):
            for c in range(nc):
                a_mat, u, _, _, k_dec, cd_row = tmp["cur"][c]
                v_new = u - tmp["ws"][c][:CHUNK]
                os_.append(tmp["ws"][c][CHUNK:] + _mm(a_mat, _block_diag(v_new, bd_mask, GDN_GROUP)))
                states[c] = cd_row * states[c] + jnp.where(bd_mask, _mm_tn(k_dec, v_new), 0.0)

        stages = []
        for t in range(GDN_WINDOW):
            stages += [functools.partial(stage_ws, t), functools.partial(stage_update, t)]
        return stages

    def chunk_of(j, t, d):
        i = j * GDN_WINDOW + t
        return i if d == 0 else nchunk - 1 - i

    def window_loader(j):
        def load(n):
            d, grp = items[n]
            return load_chunk(chunk_of(j, n // nc, d), d, grp)
        return load

    first = []
    for stage in transform_stages(window_loader(0), first):
        stage()
    for n in range(ni):
        store_transformed(n, first[n])

    nwin = nchunk // GDN_WINDOW

    def step(j, carry):
        states = [st_s[c] for c in range(nc)]
        os_, nexts = [], []
        t_stages = transform_stages(window_loader(jnp.minimum(j + 1, nwin - 1)), nexts)
        s_stages = scan_stages(states, load_transformed, os_)
        g_stages, gla_stores = gla_step(j)
        g_at = {2 * i: g for i, g in enumerate(g_stages)}
        for k in range(max(len(t_stages), len(s_stages))):
            if k < len(t_stages):
                t_stages[k]()
            if k < len(s_stages):
                s_stages[k]()
            if k in g_at:
                g_at[k]()
        for t in range(GDN_WINDOW):
            for c, (d, grp) in enumerate(combos):
                rs = pl.ds(pl.multiple_of(chunk_of(j, t, d) * CHUNK, CHUNK), CHUNK)
                acc_s[rs, grp * w:(grp + 1) * w] += os_[t * nc + c]
        for c in range(nc):
            st_s[c] = states[c]
        for n in range(ni):
            store_transformed(n, nexts[n])
        gla_stores()
        return carry

    lax.fori_loop(0, nwin, step, 0)
    o_ref[...] = _bf(acc_s[...])
    gla_finish()


def _recurrent(gdn_qkv, gdn_sm3, gla_in, small, wup, bg, gdn_consts, gla_consts, batch, seq):
    T = gdn_qkv.shape[0]
    assert GDN_WINDOW == GLA_WINDOW and seq % (CHUNK * GDN_WINDOW) == 0
    tri_c, bd, expand = gdn_consts
    tri, bd_v, st_mask, kq_mask = gla_consts
    whole = lambda b: (b, 0)
    c2 = lambda b: (0, 0)
    c3 = lambda b: (0, 0, 0)
    out = pl.BlockSpec((seq, 256), whole)
    return pl.pallas_call(
        _recurrent_kernel,
        grid=(batch,),
        in_specs=[
            pl.BlockSpec((seq, 768), whole),
            pl.BlockSpec((seq, LANES), whole),
            pl.BlockSpec(tri_c.shape, c3),
            pl.BlockSpec(bd.shape, c2),
            pl.BlockSpec(expand.shape, c3),
            pl.BlockSpec((seq, 512), whole),
            pl.BlockSpec((seq, LANES), whole),
            pl.BlockSpec(wup.shape, c3),
            pl.BlockSpec(bg.shape, c2),
            pl.BlockSpec(tri.shape, c3),
            pl.BlockSpec(bd_v.shape, c2),
            pl.BlockSpec(st_mask.shape, c2),
            pl.BlockSpec(kq_mask.shape, c2),
        ],
        out_specs=[out, out],
        out_shape=[jax.ShapeDtypeStruct((T, 256), BF16), jax.ShapeDtypeStruct((T, 256), BF16)],
        scratch_shapes=[
            pltpu.VMEM((N_DIRS * GDN_NGRP, GDN_W, GDN_W), F32),
            pltpu.VMEM((2 * GDN_WINDOW * N_DIRS * GDN_NGRP, CHUNK, GDN_W), F32),
            pltpu.VMEM((3 * GDN_WINDOW * N_DIRS * GDN_NGRP, CHUNK, GDN_W), BF16),
            pltpu.VMEM((GDN_WINDOW * N_DIRS * GDN_NGRP, 8, GDN_W), F32),
            pltpu.VMEM((seq, 256), F32),
            pltpu.VMEM((N_DIRS, GLA_QW, GLA_VW), F32),
            pltpu.VMEM((seq, 256), F32),
        ],
        compiler_params=pltpu.CompilerParams(
            dimension_semantics=("arbitrary",), vmem_limit_bytes=VMEM_LIMIT),
        name="recurrent",
    )(gdn_qkv, gdn_sm3, tri_c, bd, expand, gla_in, small, wup, bg, tri, bd_v, st_mask, kq_mask)


GLA_QW = N_HEADS * GLA_DK
GLA_VW = N_HEADS * HEAD_DIM
GLA_WINDOW = 4


def _gla_constants():
    i = np.arange(CHUNK)[:, None]
    lane = np.arange(GLA_VW)[None, :]
    j = lane % CHUNK
    tri = np.stack([i >= j, i <= j]).astype(np.float32)
    r = np.arange(GLA_VW)[:, None]
    bd_v = ((r // CHUNK) == (lane // HEAD_DIM)).astype(np.float32)
    rs = np.arange(GLA_QW)[:, None]
    st_mask = ((rs // GLA_DK) == (lane // HEAD_DIM)).astype(np.float32)
    kq_mask = ((r // CHUNK) == (np.arange(GLA_QW)[None, :] // GLA_DK)).astype(np.float32)
    return tri, bd_v, st_mask, kq_mask


def _gla_program(x_ref, sm_ref, wup_ref, bg_ref, tri_ref, bdv_ref, stm_ref, kqm_ref, o_ref, st_s, acc_s):
    seq = x_ref.shape[0]
    nchunk = seq // CHUNK
    bdv_mask = bdv_ref[...] > 0.5
    st_mask = stm_ref[...] > 0.5
    kq_mask = kqm_ref[...] > 0.5

    def init():
        acc_s[...] = jnp.zeros_like(acc_s)
        st_s[...] = jnp.zeros_like(st_s)

    def rows_of(chunk):
        return pl.ds(pl.multiple_of(chunk * CHUNK, CHUNK), CHUNK)

    def load_chunk(chunk):
        rs = rows_of(chunk)
        return (x_ref[rs, 0:GLA_QW].astype(F32), x_ref[rs, GLA_QW:2 * GLA_QW].astype(F32),
                x_ref[rs, 2 * GLA_QW:2 * GLA_QW + GLA_VW], sm_ref[rs, :])

    items = [(t, d) for t in range(GLA_WINDOW) for d in range(N_DIRS)]
    ni = len(items)

    def chunk_of(jw, t, d):
        i = jw * GLA_WINDOW + t
        return i if d == 0 else nchunk - 1 - i

    def step(jw):
        raws, logits, q_es, attns, dsts, cds, outs = [], [], [], [], [], [], []
        states = []

        def stage_logits():
            for n, (t, d) in enumerate(items):
                raws.append(load_chunk(chunk_of(jw, t, d)))
                logits.append(jnp.dot(_bf(raws[n][3]), wup_ref[d], preferred_element_type=F32)
                              + bg_ref[d:d + 1, :])

        def stage_intra():
            tri_masks = [tri_ref[d] > 0.5 for d in range(N_DIRS)]
            for n, (t, d) in enumerate(items):
                q, k, v, _ = raws[n]
                gk = (jnp.minimum(logits[n], 0.0) - jnp.log1p(jnp.exp(-jnp.abs(logits[n])))) \
                    * (LOG2_E / GLA_NORMALIZER)
                b = _cumsum_rows(gk, reverse=(d == 1))
                b_end = b[CHUNK - 1:CHUNK] if d == 0 else b[0:1]
                q_e = q * (GLA_DK ** -0.5) * jnp.exp2(b)
                keb = _bf(k * jnp.exp2(-b))
                kebd = jnp.where(kq_mask, jnp.concatenate([keb] * N_HEADS, axis=0),
                                 jnp.zeros((GLA_VW, GLA_QW), BF16))
                q_es.append(_bf(q_e))
                attns.append(jnp.where(tri_masks[d], _mm_nt(q_es[n], kebd), 0.0))
                dsts.append(jnp.where(st_mask, _mm_tn(k * jnp.exp2(b_end - b), v), 0.0))
                cd_col = jnp.transpose(jnp.broadcast_to(jnp.exp2(b_end), (GLA_QW, GLA_QW)))
                cds.append(jnp.concatenate([cd_col, cd_col], axis=1))

        def stage_out():
            states.extend(st_s[d] for d in range(N_DIRS))
            for n in range(ni):
                outs.append(_mm(attns[n], _block_diag(raws[n][2], bdv_mask, N_HEADS)))
            for n, (t, d) in enumerate(items):
                outs[n] = outs[n] + _mm(q_es[n], states[d])
                states[d] = cds[n] * states[d] + dsts[n]

        def stores():
            for n, (t, d) in enumerate(items):
                acc_s[rows_of(chunk_of(jw, t, d)), :] += outs[n]
            for d in range(N_DIRS):
                st_s[d] = states[d]

        return [stage_logits, stage_intra, stage_out], stores

    def finish():
        o_ref[...] = _bf(acc_s[...])

    return init, step, finish


def _outproj_kernel(ya_ref, ob_ref, oc_ref, ym_ref, gates_ref, gb_ref, gc_ref, ones_ref, w_ref, h_ref,
                    fg_ref, o_ref, *, final):
    ones256 = ones_ref[...]

    ob = ob_ref[...].astype(F32)
    oc = oc_ref[...].astype(F32)
    ssb, ssc = _group_sums([ob * ob, oc * oc], ones256)
    yb = ob * lax.rsqrt(ssb * (1.0 / HEAD_DIM) + EPS) * gb_ref[...] * gates_ref[:, 0:256].astype(F32)
    yc = oc * lax.rsqrt(ssc * (1.0 / HEAD_DIM) + EPS) * gc_ref[...] * gates_ref[:, 256:512].astype(F32)
    y = jnp.dot(ya_ref[...], w_ref[0:256, :], preferred_element_type=F32)
    y = y + jnp.dot(_bf(yb), w_ref[256:512, :], preferred_element_type=F32)
    y = y + jnp.dot(_bf(yc), w_ref[512:768, :], preferred_element_type=F32)
    y = y + jnp.dot(ym_ref[...], w_ref[768:1024, :], preferred_element_type=F32)
    hn = h_ref[...] + y
    if final:
        hn = hn * lax.rsqrt(jnp.mean(hn * hn, axis=-1, keepdims=True) + EPS) * fg_ref[...]
    o_ref[...] = hn


def _outproj(ya, ob, oc, ym, gates, gb, gc, ones256, w, h, fg, tm, final):
    T, D = h.shape
    row = lambda i: (i, 0)
    const = lambda i: (0, 0)
    return pl.pallas_call(
        functools.partial(_outproj_kernel, final=final),
        grid=(T // tm,),
        in_specs=[
            pl.BlockSpec((tm, 256), row), pl.BlockSpec((tm, 256), row),
            pl.BlockSpec((tm, 256), row), pl.BlockSpec((tm, 256), row),
            pl.BlockSpec((tm, 512), lambda i: (i, 1)),
            pl.BlockSpec((1, 256), const), pl.BlockSpec((1, 256), const),
            pl.BlockSpec((256, 256), const),
            pl.BlockSpec((D, D), const),
            pl.BlockSpec((tm, D), row),
            pl.BlockSpec((1, D), const),
        ],
        out_specs=pl.BlockSpec((tm, D), row),
        out_shape=jax.ShapeDtypeStruct((T, D), F32),
        compiler_params=pltpu.CompilerParams(
            dimension_semantics=("arbitrary",), vmem_limit_bytes=VMEM_LIMIT),
        name="outproj_final" if final else "outproj",
    )(ya, ob, oc, ym, gates, gb, gc, ones256, w, h, fg)


def _block_ones(n, group):
    i = np.arange(n)
    return jnp.asarray((i[:, None] // group) == (i[None, :] // group), dtype=BF16)


def _head_perm_cols():
    return np.concatenate([np.arange(h * HEAD_DIM, (h + 1) * HEAD_DIM) for h in ATT_HEAD_PERM])


def _prep_w_in(w_in):
    o = np.cumsum([0, 256, 128, 128, 256, 768, 8, 8, 256, 128, 128, 256, 32, 256, 256, 256])
    (a_q, a_k, a_v, a_g, d_qkv, d_beta, d_alpha, d_gate, l_q, l_k, l_v, l_low, l_gate, m_q, m_gate) = [
        w_in[:, :, o[i]:o[i + 1]] for i in range(15)]
    perm = _head_perm_cols()
    pad = jnp.zeros(w_in.shape[:2] + (N_COLS - C_SM - 48,), w_in.dtype)
    cols = [a_q[:, :, perm], a_k, a_v, a_g[:, :, perm], d_qkv, d_gate, l_q, l_k, l_v, l_gate, m_q, m_gate,
            d_beta, d_alpha, l_low, pad]
    return jnp.concatenate(cols, axis=-1).astype(BF16)


def _rope_tables(seq):
    rows = seq // GRID_W
    row_pos = jnp.repeat(jnp.arange(rows, dtype=F32), GRID_W)
    col_pos = jnp.tile(jnp.arange(GRID_W, dtype=F32), rows)
    half = HEAD_DIM // 4
    inv_freq = ROPE_THETA ** (-jnp.arange(0, 2 * half, 2, dtype=F32) / (2 * half))
    ang = jnp.stack([row_pos, col_pos], axis=-1)[..., None] * inv_freq
    cos, sin = jnp.cos(ang), jnp.sin(ang)
    cos_h = jnp.concatenate([cos, cos], axis=-1).reshape(seq, HEAD_DIM)
    sin_h = jnp.concatenate([-sin, sin], axis=-1).reshape(seq, HEAD_DIM)
    return jnp.tile(cos_h, (1, 2)), jnp.tile(sin_h, (1, 2))


def _trunk(x, mem, p, consts):
    batch, seq, d_model = x.shape
    T = batch * seq
    depth = p["w_in"].shape[0]
    tm = 1024 if seq % 1024 == 0 else (512 if seq % 512 == 0 else seq)
    tq = 256 if seq % 256 == 0 else seq
    kvc = 512 if seq % 512 == 0 else seq
    cos_t, sin_t = _rope_tables(seq)
    mem2d = mem.reshape(batch * mem.shape[1], d_model)
    memkv = _memkv(mem2d, p["mem_norm_g"], p["w_mem_kv"], min(512, mem2d.shape[0]))
    h = x.reshape(T, d_model)
    for d in range(depth):
        qkv, gates, gdn_qkv, gla_in, mq, small, gdn_sm3 = _inproj(
            h, p["norm_g"][d], p["w_in"][d], cos_t, sin_t, p["qkg"][d], consts["ones384"],
            p["conv_w"][d], p["gdn_par"][d], consts["ones256"], seq, tm)
        ya, ym = _attention(qkv, mq, memkv[d], gates, batch, seq, tq, kvc)
        ob, oc = _recurrent(gdn_qkv, gdn_sm3, gla_in, small, p["wup"][d], p["bg"][d],
                            consts["gdn"], consts["gla"], batch, seq)
        h = _outproj(ya, ob, oc, ym, gates, p["gb"][d], p["gc"][d],
                     consts["ones256"], p["w_out"][d], h, p["final_g"], tm, final=(d == depth - 1))
    return h.reshape(batch, seq, d_model)


def kernel(x_prompt, x_sample, mem_prompt, mem_sample, norm_g, w_in, att_q_norm_g, att_k_norm_g, gdn_conv_w,
           gdn_a_log, gdn_dt_bias, gdn_out_norm_g, gla_w_gate_up, gla_b_gate, gla_out_norm_g, mem_norm_g,
           w_mem_kv, w_out, final_norm_g):
    depth, d_model = norm_g.shape
    perm = _head_perm_cols()
    w_out_p = jnp.concatenate([w_out[:, :256][:, perm], w_out[:, 256:]], axis=1).astype(BF16)
    qkg = jnp.concatenate([jnp.tile(att_q_norm_g, (1, N_HEADS)), jnp.tile(att_k_norm_g, (1, KV_HEADS))], axis=1)
    zeros8 = jnp.zeros((depth, N_COMBO), F32)
    par_a = jnp.concatenate([zeros8, -jnp.exp(gdn_a_log.reshape(depth, N_COMBO)) * LOG2_E,
                             jnp.zeros((depth, LANES - 2 * N_COMBO), F32)], axis=1)
    par_b = jnp.concatenate([zeros8, gdn_dt_bias.reshape(depth, N_COMBO),
                             jnp.zeros((depth, LANES - 2 * N_COMBO), F32)], axis=1)
    gdn_par = jnp.concatenate([par_a[:, None], par_b[:, None], jnp.zeros((depth, 6, LANES), F32)], axis=1)
    conv_w = jnp.concatenate([gdn_conv_w, jnp.zeros((depth, 5, gdn_conv_w.shape[2]), F32)], axis=1)
    wup = jnp.zeros((depth, N_DIRS, LANES, GLA_QW), F32)
    for d in range(N_DIRS):
        lo = 2 * N_COMBO + d * GLA_RANK
        wup = wup.at[:, d, lo:lo + GLA_RANK, :].set(gla_w_gate_up[:, d])
    p = {
        "norm_g": norm_g.reshape(depth, 1, d_model),
        "w_in": _prep_w_in(w_in),
        "qkg": qkg.reshape(depth, 1, 384),
        "conv_w": conv_w,
        "gdn_par": gdn_par,
        "wup": wup.astype(BF16),
        "bg": jnp.concatenate([gla_b_gate, jnp.zeros((depth, 6, GLA_QW), F32)], axis=1),
        "gb": jnp.tile(gdn_out_norm_g, (1, N_HEADS)).reshape(depth, 1, 256),
        "gc": jnp.tile(gla_out_norm_g, (1, N_HEADS)).reshape(depth, 1, 256),
        "mem_norm_g": mem_norm_g.reshape(depth, 1, d_model),
        "w_mem_kv": w_mem_kv.astype(BF16),
        "w_out": w_out_p,
        "final_g": final_norm_g.reshape(1, d_model),
    }
    tri_c, bd, expand = _gdn_constants()
    consts = {
        "ones384": _block_ones(384, HEAD_DIM),
        "ones256": _block_ones(256, HEAD_DIM),
        "gdn": (jnp.asarray(tri_c), jnp.asarray(bd), jnp.asarray(expand, dtype=BF16)),
        "gla": tuple(jnp.asarray(a) for a in _gla_constants()),
    }
    y_prompt = _trunk(x_prompt, mem_prompt, p, consts)
    y_sample = _trunk(x_sample, mem_sample, p, consts)
    return (y_prompt, y_sample)
```

```python
import functools

import numpy as np
import jax
import jax.numpy as jnp
from jax import lax
from jax.experimental import pallas as pl
from jax.experimental.pallas import tpu as pltpu

F32 = jnp.float32
BF16 = jnp.bfloat16

N_HEADS = 4
HEAD_DIM = 64
KV_HEADS = 2
GRID_W = 64
ROPE_THETA = 10000.0
CHUNK = 64
GLA_DK = 32
GLA_RANK = 16
GLA_NORMALIZER = 16.0
EPS = 1e-6
N_DIRS = 2
LANES = 128
GDN_GROUP = 2
GDN_WINDOW = 4
VMEM_LIMIT = 56 * 1024 * 1024

C_AQ, C_AK, C_AV, C_AG = 0, 256, 384, 512
C_DQKV, C_DG = 768, 1536
C_LQ, C_LK, C_LV, C_LG = 1792, 1920, 2048, 2304
C_MQ, C_MG = 2560, 2816
C_SM = 3072
N_COLS = 3200
ATT_HEAD_PERM = (0, 2, 1, 3)
ATT_VT_ROWS = 2 * HEAD_DIM
LOG2_E = float(np.log2(np.e))
ATT_Q_SCALE = HEAD_DIM ** -0.5 * LOG2_E


def _bf(x):
    return x.astype(BF16)


def _mm(a, b):
    return jnp.dot(_bf(a), _bf(b), preferred_element_type=F32)


def _mm_nt(a, b):
    return lax.dot_general(_bf(a), _bf(b), (((1,), (1,)), ((), ())), preferred_element_type=F32)


def _mm_tn(a, b):
    return lax.dot_general(_bf(a), _bf(b), (((0,), (0,)), ((), ())), preferred_element_type=F32)


def _split2(x):
    hi = _bf(x)
    lo = _bf(x - hi.astype(F32))
    return hi, lo


def _split3(x):
    hi = _bf(x).astype(F32)
    r = x - hi
    mid = _bf(r).astype(F32)
    lo = _bf(r - mid).astype(F32)
    return hi, mid, lo


def _group_sums(xs, ones_blocks):
    rows = xs[0].shape[0]
    parts = []
    for x in xs:
        parts.extend(_split2(x))
    s = jnp.dot(jnp.concatenate(parts, axis=0), ones_blocks, preferred_element_type=F32)
    return [s[2 * i * rows:(2 * i + 1) * rows] + s[(2 * i + 1) * rows:(2 * i + 2) * rows] for i in range(len(xs))]


def _silu(x):
    return x * (0.5 * jnp.tanh(0.5 * x) + 0.5)


def _softplus(x):
    return jnp.maximum(x, 0.0) + jnp.log1p(jnp.exp(-jnp.abs(x)))


def _cumsum_rows(x, reverse):
    n = x.shape[0]
    row = lax.broadcasted_iota(jnp.int32, x.shape, 0)
    s = 1
    while s < n:
        if reverse:
            x = x + jnp.where(row < n - s, pltpu.roll(x, n - s, axis=0), 0.0)
        else:
            x = x + jnp.where(row >= s, pltpu.roll(x, s, axis=0), 0.0)
        s *= 2
    return x


def _block_diag(x, mask, reps):
    xb = _bf(x)
    t = jnp.concatenate([xb] * reps, axis=0)
    return jnp.where(mask, t, jnp.zeros_like(t))


def _inproj_kernel(h_ref, hprev_ref, hnext_ref, g_ref, w_ref, cos_ref, sin_ref, qkg_ref, ones_ref,
                   cw_ref, par_ref, ones256_ref,
                   qkv_ref, gates_ref, gdn_ref, gla_ref, mq_ref, small_ref, sm3_ref, *, nseq):
    tm = h_ref.shape[0]

    def normed(x):
        return _bf(x * lax.rsqrt(jnp.mean(x * x, axis=-1, keepdims=True) + EPS) * g_ref[...])

    halo = jnp.concatenate([hprev_ref[0], hnext_ref[0]], axis=0)
    hp = jnp.dot(normed(halo), w_ref[:, C_DQKV:C_DG], preferred_element_type=F32)
    xn = normed(h_ref[...])
    groups = ((C_DQKV, C_DG), (C_SM, N_COLS), (0, C_DQKV), (C_DG, C_SM))
    pieces = [jnp.dot(xn, w_ref[:, a:b], preferred_element_type=F32) for a, b in groups]

    class _Proj:
        def __getitem__(self, idx):
            rows, cols = idx
            for (a, b), piece in zip(groups, pieces):
                if a <= cols.start and cols.stop <= b:
                    return piece[rows, cols.start - a:cols.stop - a]
            raise IndexError(cols)

    proj = _Proj()

    pos = pl.program_id(0) % nseq
    prev_row = hp[7:8] * jnp.where(pos > 0, 1.0, 0.0)
    next_row = hp[8:9] * jnp.where(pos < nseq - 1, 1.0, 0.0)
    xg = proj[:, C_DQKV:C_DG]
    trow = lax.broadcasted_iota(jnp.int32, (tm, 1), 0)
    x_prev = jnp.where(trow == 0, prev_row, pltpu.roll(xg, 1, axis=0))
    x_next = jnp.where(trow == tm - 1, next_row, pltpu.roll(xg, tm - 1, axis=0))
    y = x_prev * cw_ref[0:1, :] + xg * cw_ref[1:2, :]
    y = _silu(y + x_next * cw_ref[2:3, :])
    sss = _group_sums([y[:, 0:256] * y[:, 0:256], y[:, 256:512] * y[:, 256:512]], ones256_ref[...])
    gdn_ref[:, 0:256] = _bf(y[:, 0:256] * lax.rsqrt(sss[0] + EPS) * (HEAD_DIM ** -0.5))
    gdn_ref[:, 256:512] = _bf(y[:, 256:512] * lax.rsqrt(sss[1] + EPS))
    gdn_ref[:, 512:768] = _bf(y[:, 512:768])

    lane = lax.broadcasted_iota(jnp.int32, (1, LANES), 1)
    lane8 = lane < N_COMBO
    fwd_lane = lane < N_HEADS
    for c in range(tm // CHUNK):
        sm = proj[c * CHUNK:(c + 1) * CHUNK, C_SM:N_COLS]
        beta = 0.5 * jnp.tanh(0.5 * sm) + 0.5
        g_all = par_ref[0:1, :] * _softplus(sm + par_ref[1:2, :])
        g = pltpu.roll(g_all, LANES - N_COMBO, axis=1)
        b = jnp.where(fwd_lane, _cumsum_rows(g, reverse=False), _cumsum_rows(g, reverse=True))
        zero = jnp.zeros_like(b)
        q2 = jnp.where(lane8, beta, zero) + pltpu.roll(jnp.where(lane8, b, zero), N_COMBO, axis=1)
        hi, mid, lo3 = _split3(q2)
        sm3_ref[c * CHUNK:(c + 1) * CHUNK, :] = _bf(
            hi + pltpu.roll(mid, 2 * N_COMBO, axis=1) + pltpu.roll(lo3, 4 * N_COMBO, axis=1))

    qk = proj[:, C_AQ:C_AV]
    ms = _group_sums([qk * qk], ones_ref[...])[0] * (1.0 / HEAD_DIM)
    qkn = qk * lax.rsqrt(ms + EPS) * qkg_ref[...]
    cos = cos_ref[...]
    sin = sin_ref[...]
    lane = lax.broadcasted_iota(jnp.int32, (1, LANES), 1)
    first_half = (lane % 32) < 16
    outs = []
    for c in range(3):
        xc = qkn[:, c * LANES:(c + 1) * LANES]
        partner = jnp.where(first_half, pltpu.roll(xc, LANES - 16, axis=1), pltpu.roll(xc, 16, axis=1))
        outs.append(xc * cos + partner * sin)
    qkv_ref[:, 0:128] = _bf(outs[0] * ATT_Q_SCALE)
    qkv_ref[:, 128:256] = _bf(outs[1] * ATT_Q_SCALE)
    qkv_ref[:, 256:384] = _bf(outs[2])
    qkv_ref[:, 384:512] = _bf(proj[:, C_AV:C_AG])

    gates_ref[:, 0:256] = _bf(_silu(proj[:, C_AG:C_AG + 256]))
    gates_ref[:, 256:512] = _bf(_silu(proj[:, C_MG:C_MG + 256]))
    gates_ref[:, 512:768] = _bf(_silu(proj[:, C_DG:C_DG + 256]))
    gates_ref[:, 768:1024] = _bf(_silu(proj[:, C_LG:C_LG + 256]))
    gla_ref[...] = _bf(proj[:, C_LQ:C_LG])
    mq_ref[...] = _bf(proj[:, C_MQ:C_MG] * ATT_Q_SCALE)
    small_ref[...] = proj[:, C_SM:N_COLS]


def _inproj(h, g, w, cos_t, sin_t, qkg, ones384, conv_w, par, ones256, seq, tm):
    T, D = h.shape
    nseq = seq // tm
    row = lambda i: (i, 0)
    const = lambda i: (0, 0)
    h8 = h.reshape(T // 8, 8, D)
    g8 = tm // 8
    return pl.pallas_call(
        functools.partial(_inproj_kernel, nseq=nseq),
        grid=(T // tm,),
        in_specs=[
            pl.BlockSpec((tm, D), row),
            pl.BlockSpec((1, 8, D), lambda i: (jnp.maximum(i * g8 - 1, 0), 0, 0)),
            pl.BlockSpec((1, 8, D), lambda i: (jnp.minimum((i + 1) * g8, T // 8 - 1), 0, 0)),
            pl.BlockSpec((1, D), const),
            pl.BlockSpec((D, N_COLS), const),
            pl.BlockSpec((tm, LANES), lambda i: (i % nseq, 0)),
            pl.BlockSpec((tm, LANES), lambda i: (i % nseq, 0)),
            pl.BlockSpec((1, 384), const),
            pl.BlockSpec((384, 384), const),
            pl.BlockSpec((8, 768), const),
            pl.BlockSpec((8, LANES), const),
            pl.BlockSpec((256, 256), const),
        ],
        out_specs=[
            pl.BlockSpec((tm, 512), row),
            pl.BlockSpec((tm, 1024), row),
            pl.BlockSpec((tm, 768), row),
            pl.BlockSpec((tm, 512), row),
            pl.BlockSpec((tm, 256), row),
            pl.BlockSpec((tm, LANES), row),
            pl.BlockSpec((tm, LANES), row),
        ],
        out_shape=[
            jax.ShapeDtypeStruct((T, 512), BF16),
            jax.ShapeDtypeStruct((T, 1024), BF16),
            jax.ShapeDtypeStruct((T, 768), BF16),
            jax.ShapeDtypeStruct((T, 512), BF16),
            jax.ShapeDtypeStruct((T, 256), BF16),
            jax.ShapeDtypeStruct((T, LANES), F32),
            jax.ShapeDtypeStruct((T, LANES), BF16),
        ],
        compiler_params=pltpu.CompilerParams(
            dimension_semantics=("arbitrary",), vmem_limit_bytes=VMEM_LIMIT),
        name="inproj",
    )(h, h8, h8, g, w, cos_t, sin_t, qkg, ones384, conv_w, par, ones256)


def _memkv_kernel(m_ref, g_ref, w_ref, o_ref):
    x = m_ref[...]
    xn = x * lax.rsqrt(jnp.mean(x * x, axis=-1, keepdims=True) + EPS) * g_ref[0]
    o_ref[0] = _bf(jnp.dot(_bf(xn), w_ref[0], preferred_element_type=F32))


def _memkv(mem2d, g, w, tr):
    R, D = mem2d.shape
    depth = w.shape[0]
    return pl.pallas_call(
        _memkv_kernel,
        grid=(depth, R // tr),
        in_specs=[
            pl.BlockSpec((tr, D), lambda d, i: (i, 0)),
            pl.BlockSpec((1, 1, D), lambda d, i: (d, 0, 0)),
            pl.BlockSpec((1, D, 512), lambda d, i: (d, 0, 0)),
        ],
        out_specs=pl.BlockSpec((1, tr, 512), lambda d, i: (d, i, 0)),
        out_shape=jax.ShapeDtypeStruct((depth, R, 512), BF16),
        compiler_params=pltpu.CompilerParams(
            dimension_semantics=("arbitrary", "arbitrary"), vmem_limit_bytes=VMEM_LIMIT),
        name="memkv",
    )(mem2d, g, w)


def _attn_kernel(q_ref, k_ref, v_ref, mq_ref, km_ref, vm_ref, ga_ref, gm_ref, ya_ref, ym_ref, vt_s, *, kvc):
    tq = q_ref.shape[0]
    seq = k_ref.shape[0]
    lane = lax.broadcasted_iota(jnp.int32, (1, LANES), 1)
    lo = lane < HEAD_DIM

    def stack_heads(x):
        z = jnp.zeros_like(x)
        return [jnp.where(lo, x, z), jnp.where(lo, z, x)]

    @pl.when(pl.program_id(1) == 0)
    def _():
        ones = jnp.ones((ATT_VT_ROWS - HEAD_DIM, kvc), F32)
        for c in range(seq // kvc):
            vt = jnp.transpose(v_ref[c * kvc:(c + 1) * kvc, :].astype(F32))
            vt_s[0, c] = _bf(jnp.concatenate([vt[0:HEAD_DIM], ones], axis=0))
            vt_s[1, c] = _bf(jnp.concatenate([vt[HEAD_DIM:LANES], ones], axis=0))

    q = q_ref[...]
    qm = stack_heads(q[:, :LANES]) + stack_heads(q[:, LANES:])
    n_heads = len(qm)

    half = kvc // 2

    def scores(c, h, part):
        r0 = c * kvc + part * half
        return lax.dot_general(k_ref[r0:r0 + half, :], qm[h], (((1,), (1,)), ((), ())),
                               preferred_element_type=F32)

    ms = [jnp.full((1, tq), -jnp.inf, F32) for _ in range(n_heads)]
    accs = [jnp.zeros((ATT_VT_ROWS, tq), F32) for _ in range(n_heads)]
    nsub = seq // kvc
    sts = [[scores(0, h, part) for part in range(2)] for h in range(n_heads)]
    for c in range(nsub):
        for h in range(n_heads):
            m_new = jnp.maximum(ms[h], jnp.maximum(jnp.max(sts[h][0], axis=0, keepdims=True),
                                                   jnp.max(sts[h][1], axis=0, keepdims=True)))
            acc = jnp.exp2(ms[h] - m_new) * accs[h]
            nxt = [None, None]
            for part in range(2):
                if c + 1 < nsub:
                    nxt[part] = scores(c + 1, h, part)
                p = _bf(jnp.exp2(sts[h][part] - m_new))
                acc = acc + jnp.dot(vt_s[h % 2, c][:, part * half:(part + 1) * half], p,
                                    preferred_element_type=F32)
            accs[h] = acc
            ms[h], sts[h] = m_new, nxt
    o_t = [accs[h][0:HEAD_DIM] / accs[h][HEAD_DIM:HEAD_DIM + 1] for h in range(n_heads)]
    ya = jnp.transpose(jnp.concatenate(o_t, axis=0))
    ya_ref[...] = _bf(ya * ga_ref[...].astype(F32))

    mq = mq_ref[...]
    outs = []
    for c in range(2):
        qc = jnp.concatenate(stack_heads(mq[:, c * LANES:(c + 1) * LANES]), axis=0)
        kc = km_ref[:, c * LANES:(c + 1) * LANES]
        vc = vm_ref[:, c * LANES:(c + 1) * LANES]
        s = lax.dot_general(qc, kc, (((1,), (1,)), ((), ())), preferred_element_type=F32)
        p = jnp.exp2(s - jnp.max(s, axis=-1, keepdims=True))
        om = jnp.dot(_bf(p), vc, preferred_element_type=F32) / jnp.sum(p, axis=-1, keepdims=True)
        outs.append(jnp.where(lo, om[0:tq], om[tq:2 * tq]))
    ym_ref[...] = _bf(jnp.concatenate(outs, axis=1) * gm_ref[...].astype(F32))


def _attention(qkv, mq, memkv, gates, batch, seq, tq, kvc):
    T = qkv.shape[0]
    nq = seq // tq
    mem_len = memkv.shape[0] // batch
    qrow = lambda b, i: (b * nq + i, 0)
    return pl.pallas_call(
        functools.partial(_attn_kernel, kvc=kvc),
        grid=(batch, nq),
        in_specs=[
            pl.BlockSpec((tq, 256), qrow),
            pl.BlockSpec((seq, LANES), lambda b, i: (b, 2)),
            pl.BlockSpec((seq, LANES), lambda b, i: (b, 3)),
            pl.BlockSpec((tq, 256), qrow),
            pl.BlockSpec((mem_len, 256), lambda b, i: (b, 0)),
            pl.BlockSpec((mem_len, 256), lambda b, i: (b, 1)),
            pl.BlockSpec((tq, 256), lambda b, i: (b * nq + i, 0)),
            pl.BlockSpec((tq, 256), lambda b, i: (b * nq + i, 1)),
        ],
        out_specs=[pl.BlockSpec((tq, 256), qrow), pl.BlockSpec((tq, 256), qrow)],
        out_shape=[jax.ShapeDtypeStruct((T, 256), BF16), jax.ShapeDtypeStruct((T, 256), BF16)],
        scratch_shapes=[pltpu.VMEM((KV_HEADS, seq // kvc, ATT_VT_ROWS, kvc), BF16)],
        compiler_params=pltpu.CompilerParams(
            dimension_semantics=("arbitrary", "arbitrary"), vmem_limit_bytes=VMEM_LIMIT),
        name="attention",
    )(qkv, qkv, qkv, mq, memkv, memkv, gates, gates)


GDN_W = GDN_GROUP * HEAD_DIM
GDN_NGRP = N_HEADS // GDN_GROUP
N_COMBO = N_DIRS * N_HEADS
N_QTY = 2


def _gdn_constants():
    w = GDN_W
    i = np.arange(CHUNK)[:, None]
    lane = np.arange(w)[None, :]
    j = lane % CHUNK
    tri = np.stack([i >= j, i <= j]).astype(np.float32)
    stri = np.stack([i > j, i < j]).astype(np.float32)
    eye = (i == j).astype(np.float32)[None]
    tri_c = np.concatenate([tri, stri, eye], axis=0)
    r = np.arange(w)[:, None]
    bd = ((r // CHUNK) == (lane // CHUNK)).astype(np.float32)
    expand = np.zeros((N_DIRS * GDN_NGRP, LANES, N_QTY * w), np.float32)
    for d in range(N_DIRS):
        for g in range(GDN_NGRP):
            for qty in range(N_QTY):
                for hh in range(GDN_GROUP):
                    combo = d * N_HEADS + g * GDN_GROUP + hh
                    for piece in range(3):
                        src = piece * (N_QTY * N_COMBO) + qty * N_COMBO + combo
                        expand[d * GDN_NGRP + g, src, qty * w + hh * CHUNK: qty * w + (hh + 1) * CHUNK] = 1.0
    return tri_c, bd, expand


def _recurrent_kernel(qkv_ref, sm3_ref, tri_ref, bd_ref, exp_ref,
                      gx_ref, gsm_ref, wup_ref, bg_ref, gtri_ref, bdv_ref, stm_ref, kqm_ref,
                      o_ref, oc_ref, st_s, pa_s, pb_s, cd_s, acc_s, gst_s, gacc_s):
    seq = qkv_ref.shape[0]
    nchunk = seq // CHUNK
    w = GDN_W
    gla_init, gla_step, gla_finish = _gla_program(gx_ref, gsm_ref, wup_ref, bg_ref, gtri_ref, bdv_ref, stm_ref,
                                                  kqm_ref, oc_ref, gst_s, gacc_s)
    gla_init()
    acc_s[...] = jnp.zeros_like(acc_s)
    st_s[...] = jnp.zeros_like(st_s)
    bd_mask = bd_ref[...] > 0.5

    combos = [(d, grp) for d in range(N_DIRS) for grp in range(GDN_NGRP)]

    def load_chunk(chunk, d, grp):
        rs = pl.ds(pl.multiple_of(chunk * CHUNK, CHUNK), CHUNK)
        return (qkv_ref[rs, grp * w:(grp + 1) * w], qkv_ref[rs, 256 + grp * w:256 + (grp + 1) * w],
                qkv_ref[rs, 512 + grp * w:512 + (grp + 1) * w], sm3_ref[rs, :])

    nc = len(combos)
    head_masks = [(lax.broadcasted_iota(jnp.int32, (1, w), 1) // HEAD_DIM) == hh for hh in range(GDN_GROUP)]

    items = [(d, grp) for _ in range(GDN_WINDOW) for d, grp in combos]
    ni = len(items)

    def transform_stages(raws, outs):
        exs, diffs, kqs = [], [], []
        a_mats, ps, rs_ = [], [], []

        def stage_products():
            ex_all = [jnp.dot(jnp.concatenate([raws(t * nc + c)[3] for t in range(GDN_WINDOW)], axis=0),
                              exp_ref[c], preferred_element_type=F32) for c in range(nc)]
            for n, (d, grp) in enumerate(items):
                q, k, vb, _ = raws(n)
                ex = ex_all[n % nc][(n // nc) * CHUNK:(n // nc + 1) * CHUNK]
                betax, bx = ex[:, 0:w], ex[:, w:2 * w]
                ebx = jnp.exp2(bx)
                b_end = bx[CHUNK - 1:CHUNK] if d == 0 else bx[0:1]
                exs.append([betax, betax * ebx, ebx, jnp.exp2(b_end - bx)])
                diffs.append(bx - jnp.sum(bx * tri_ref[4], axis=0, keepdims=True))
                kbd = jnp.concatenate([jnp.where(hm, k, jnp.zeros_like(k)) for hm in head_masks], axis=0)
                kqs.append(lax.dot_general(jnp.concatenate([k, q], axis=0), kbd, (((1,), (1,)), ((), ())),
                                           preferred_element_type=F32))

        def stage_square():
            tris = [tri_ref[d] > 0.5 for d in range(N_DIRS)]
            stris = [tri_ref[2 + d] > 0.5 for d in range(N_DIRS)]
            for n, (d, grp) in enumerate(items):
                tri = tris[d]
                stri = stris[d]
                decay = jnp.where(tri, jnp.exp2(jnp.where(tri, diffs[n], 0.0)), 0.0)
                low = jnp.where(stri, exs[n][0] * kqs[n][:CHUNK] * decay, 0.0)
                a_mats.append(kqs[n][CHUNK:] * decay)
                ps.append(_mm(low, _block_diag(low, bd_mask, GDN_GROUP)))
                rs_.append(tri_ref[4] - low)

        def stage_level(last):
            for n in range(ni):
                pbd = _block_diag(ps[n], bd_mask, GDN_GROUP)
                if last:
                    rs_[n] = rs_[n] + _mm(rs_[n], pbd)
                else:
                    rp = _mm(jnp.concatenate([rs_[n], ps[n]], axis=0), pbd)
                    rs_[n] = rs_[n] + rp[:CHUNK]
                    ps[n] = rp[CHUNK:]

        def stage_apply():
            for n, (d, grp) in enumerate(items):
                q, k, vb = raws(n)[:3]
                betax, bebx, ebx, edx = exs[n]
                kf = k.astype(F32)
                u = _mm(rs_[n], _block_diag(betax * vb.astype(F32), bd_mask, GDN_GROUP))
                wm = _mm(rs_[n], _block_diag(bebx * kf, bd_mask, GDN_GROUP))
                cd_row = ebx[CHUNK - 1:CHUNK] if d == 0 else ebx[0:1]
                outs.append((a_mats[n], u, _bf(wm), _bf(q.astype(F32) * ebx), _bf(kf * edx), cd_row))

        return ([stage_products, stage_square] + [functools.partial(stage_level, False)] * 4
                + [functools.partial(stage_level, True), stage_apply])

    def store_transformed(c, t):
        a_mat, u, wm, q_dec, k_dec, cd_row = t
        pa_s[2 * c] = a_mat
        pa_s[2 * c + 1] = u
        pb_s[3 * c] = wm
        pb_s[3 * c + 1] = q_dec
        pb_s[3 * c + 2] = k_dec
        cd_s[c] = jnp.broadcast_to(cd_row, (8, w))

    def load_transformed(c):
        return pa_s[2 * c], pa_s[2 * c + 1], pb_s[3 * c], pb_s[3 * c + 1], pb_s[3 * c + 2], cd_s[c][0:1]

    def scan_stages(states, cur, os_):
        tmp = {}

        def stage_ws(t):
            tmp["cur"] = [cur(t * nc + c) for c in range(nc)]
            tmp["ws"] = [_mm(jnp.concatenate([tmp["cur"][c][2], tmp["cur"][c][3]], axis=0), states[c])
                         for c in range(nc)]

        def stage_update(t):
            for c in range(nc):
                a_mat, u, _, _, k_dec, cd_row = tmp["cur"][c]
                v_new = u - tmp["ws"][c][:CHUNK]
                os_.append(tmp["ws"][c][CHUNK:] + _mm(a_mat, _block_diag(v_new, bd_mask, GDN_GROUP)))
                states[c] = cd_row * states[c] + jnp.where(bd_mask, _mm_tn(k_dec, v_new), 0.0)

        stages = []
        for t in range(GDN_WINDOW):
            stages += [functools.partial(stage_ws, t), functools.partial(stage_update, t)]
        return stages

    def chunk_of(j, t, d):
        i = j * GDN_WINDOW + t
        return i if d == 0 else nchunk - 1 - i

    def window_loader(j):
        def load(n):
            d, grp = items[n]
            return load_chunk(chunk_of(j, n // nc, d), d, grp)
        return load

    first = []
    for stage in transform_stages(window_loader(0), first):
        stage()
    for n in range(ni):
        store_transformed(n, first[n])

    nwin = nchunk // GDN_WINDOW

    def step(j, carry):
        states = [st_s[c] for c in range(nc)]
        os_, nexts = [], []
        t_stages = transform_stages(window_loader(jnp.minimum(j + 1, nwin - 1)), nexts)
        s_stages = scan_stages(states, load_transformed, os_)
        g_stages, gla_stores = gla_step(j)
        g_at = {2 * i: g for i, g in enumerate(g_stages)}
        for k in range(max(len(t_stages), len(s_stages))):
            if k < len(t_stages):
                t_stages[k]()
            if k < len(s_stages):
                s_stages[k]()
            if k in g_at:
                g_at[k]()
        for t in range(GDN_WINDOW):
            for c, (d, grp) in enumerate(combos):
                rs = pl.ds(pl.multiple_of(chunk_of(j, t, d) * CHUNK, CHUNK), CHUNK)
                acc_s[rs, grp * w:(grp + 1) * w] += os_[t * nc + c]
        for c in range(nc):
            st_s[c] = states[c]
        for n in range(ni):
            store_transformed(n, nexts[n])
        gla_stores()
        return carry

    lax.fori_loop(0, nwin, step, 0)
    o_ref[...] = _bf(acc_s[...])
    gla_finish()


def _recurrent(gdn_qkv, gdn_sm3, gla_in, small, wup, bg, gdn_consts, gla_consts, batch, seq):
    T = gdn_qkv.shape[0]
    assert GDN_WINDOW == GLA_WINDOW and seq % (CHUNK * GDN_WINDOW) == 0
    tri_c, bd, expand = gdn_consts
    tri, bd_v, st_mask, kq_mask = gla_consts
    whole = lambda b: (b, 0)
    c2 = lambda b: (0, 0)
    c3 = lambda b: (0, 0, 0)
    out = pl.BlockSpec((seq, 256), whole)
    return pl.pallas_call(
        _recurrent_kernel,
        grid=(batch,),
        in_specs=[
            pl.BlockSpec((seq, 768), whole),
            pl.BlockSpec((seq, LANES), whole),
            pl.BlockSpec(tri_c.shape, c3),
            pl.BlockSpec(bd.shape, c2),
            pl.BlockSpec(expand.shape, c3),
            pl.BlockSpec((seq, 512), whole),
            pl.BlockSpec((seq, LANES), whole),
            pl.BlockSpec(wup.shape, c3),
            pl.BlockSpec(bg.shape, c2),
            pl.BlockSpec(tri.shape, c3),
            pl.BlockSpec(bd_v.shape, c2),
            pl.BlockSpec(st_mask.shape, c2),
            pl.BlockSpec(kq_mask.shape, c2),
        ],
        out_specs=[out, out],
        out_shape=[jax.ShapeDtypeStruct((T, 256), BF16), jax.ShapeDtypeStruct((T, 256), BF16)],
        scratch_shapes=[
            pltpu.VMEM((N_DIRS * GDN_NGRP, GDN_W, GDN_W), F32),
            pltpu.VMEM((2 * GDN_WINDOW * N_DIRS * GDN_NGRP, CHUNK, GDN_W), F32),
            pltpu.VMEM((3 * GDN_WINDOW * N_DIRS * GDN_NGRP, CHUNK, GDN_W), BF16),
            pltpu.VMEM((GDN_WINDOW * N_DIRS * GDN_NGRP, 8, GDN_W), F32),
            pltpu.VMEM((seq, 256), F32),
            pltpu.VMEM((N_DIRS, GLA_QW, GLA_VW), F32),
            pltpu.VMEM((seq, 256), F32),
        ],
        compiler_params=pltpu.CompilerParams(
            dimension_semantics=("arbitrary",), vmem_limit_bytes=VMEM_LIMIT),
        name="recurrent",
    )(gdn_qkv, gdn_sm3, tri_c, bd, expand, gla_in, small, wup, bg, tri, bd_v, st_mask, kq_mask)


GLA_QW = N_HEADS * GLA_DK
GLA_VW = N_HEADS * HEAD_DIM
GLA_WINDOW = 4


def _gla_constants():
    i = np.arange(CHUNK)[:, None]
    lane = np.arange(GLA_VW)[None, :]
    j = lane % CHUNK
    tri = np.stack([i >= j, i <= j]).astype(np.float32)
    r = np.arange(GLA_VW)[:, None]
    bd_v = ((r // CHUNK) == (lane // HEAD_DIM)).astype(np.float32)
    rs = np.arange(GLA_QW)[:, None]
    st_mask = ((rs // GLA_DK) == (lane // HEAD_DIM)).astype(np.float32)
    kq_mask = ((r // CHUNK) == (np.arange(GLA_QW)[None, :] // GLA_DK)).astype(np.float32)
    return tri, bd_v, st_mask, kq_mask


def _gla_program(x_ref, sm_ref, wup_ref, bg_ref, tri_ref, bdv_ref, stm_ref, kqm_ref, o_ref, st_s, acc_s):
    seq = x_ref.shape[0]
    nchunk = seq // CHUNK
    bdv_mask = bdv_ref[...] > 0.5
    st_mask = stm_ref[...] > 0.5
    kq_mask = kqm_ref[...] > 0.5

    def init():
        acc_s[...] = jnp.zeros_like(acc_s)
        st_s[...] = jnp.zeros_like(st_s)

    def rows_of(chunk):
        return pl.ds(pl.multiple_of(chunk * CHUNK, CHUNK), CHUNK)

    def load_chunk(chunk):
        rs = rows_of(chunk)
        return (x_ref[rs, 0:GLA_QW].astype(F32), x_ref[rs, GLA_QW:2 * GLA_QW].astype(F32),
                x_ref[rs, 2 * GLA_QW:2 * GLA_QW + GLA_VW], sm_ref[rs, :])

    items = [(t, d) for t in range(GLA_WINDOW) for d in range(N_DIRS)]
    ni = len(items)

    def chunk_of(jw, t, d):
        i = jw * GLA_WINDOW + t
        return i if d == 0 else nchunk - 1 - i

    def step(jw):
        raws, logits, q_es, attns, dsts, cds, outs = [], [], [], [], [], [], []
        states = []

        def stage_logits():
            for n, (t, d) in enumerate(items):
                raws.append(load_chunk(chunk_of(jw, t, d)))
                logits.append(jnp.dot(_bf(raws[n][3]), wup_ref[d], preferred_element_type=F32)
                              + bg_ref[d:d + 1, :])

        def stage_intra():
            tri_masks = [tri_ref[d] > 0.5 for d in range(N_DIRS)]
            for n, (t, d) in enumerate(items):
                q, k, v, _ = raws[n]
                gk = (jnp.minimum(logits[n], 0.0) - jnp.log1p(jnp.exp(-jnp.abs(logits[n])))) \
                    * (LOG2_E / GLA_NORMALIZER)
                b = _cumsum_rows(gk, reverse=(d == 1))
                b_end = b[CHUNK - 1:CHUNK] if d == 0 else b[0:1]
                q_e = q * (GLA_DK ** -0.5) * jnp.exp2(b)
                keb = _bf(k * jnp.exp2(-b))
                kebd = jnp.where(kq_mask, jnp.concatenate([keb] * N_HEADS, axis=0),
                                 jnp.zeros((GLA_VW, GLA_QW), BF16))
                q_es.append(_bf(q_e))
                attns.append(jnp.where(tri_masks[d], _mm_nt(q_es[n], kebd), 0.0))
                dsts.append(jnp.where(st_mask, _mm_tn(k * jnp.exp2(b_end - b), v), 0.0))
                cd_col = jnp.transpose(jnp.broadcast_to(jnp.exp2(b_end), (GLA_QW, GLA_QW)))
                cds.append(jnp.concatenate([cd_col, cd_col], axis=1))

        def stage_out():
            states.extend(st_s[d] for d in range(N_DIRS))
            for n in range(ni):
                outs.append(_mm(attns[n], _block_diag(raws[n][2], bdv_mask, N_HEADS)))
            for n, (t, d) in enumerate(items):
                outs[n] = outs[n] + _mm(q_es[n], states[d])
                states[d] = cds[n] * states[d] + dsts[n]

        def stores():
            for n, (t, d) in enumerate(items):
                acc_s[rows_of(chunk_of(jw, t, d)), :] += outs[n]
            for d in range(N_DIRS):
                st_s[d] = states[d]

        return [stage_logits, stage_intra, stage_out], stores

    def finish():
        o_ref[...] = _bf(acc_s[...])

    return init, step, finish


def _outproj_kernel(ya_ref, ob_ref, oc_ref, ym_ref, gates_ref, gb_ref, gc_ref, ones_ref, w_ref, h_ref,
                    fg_ref, o_ref, *, final):
    ones256 = ones_ref[...]

    ob = ob_ref[...].astype(F32)
    oc = oc_ref[...].astype(F32)
    ssb, ssc = _group_sums([ob * ob, oc * oc], ones256)
    yb = ob * lax.rsqrt(ssb * (1.0 / HEAD_DIM) + EPS) * gb_ref[...] * gates_ref[:, 0:256].astype(F32)
    yc = oc * lax.rsqrt(ssc * (1.0 / HEAD_DIM) + EPS) * gc_ref[...] * gates_ref[:, 256:512].astype(F32)
    y = jnp.dot(ya_ref[...], w_ref[0:256, :], preferred_element_type=F32)
    y = y + jnp.dot(_bf(yb), w_ref[256:512, :], preferred_element_type=F32)
    y = y + jnp.dot(_bf(yc), w_ref[512:768, :], preferred_element_type=F32)
    y = y + jnp.dot(ym_ref[...], w_ref[768:1024, :], preferred_element_type=F32)
    hn = h_ref[...] + y
    if final:
        hn = hn * lax.rsqrt(jnp.mean(hn * hn, axis=-1, keepdims=True) + EPS) * fg_ref[...]
    o_ref[...] = hn


def _outproj(ya, ob, oc, ym, gates, gb, gc, ones256, w, h, fg, tm, final):
    T, D = h.shape
    row = lambda i: (i, 0)
    const = lambda i: (0, 0)
    return pl.pallas_call(
        functools.partial(_outproj_kernel, final=final),
        grid=(T // tm,),
        in_specs=[
            pl.BlockSpec((tm, 256), row), pl.BlockSpec((tm, 256), row),
            pl.BlockSpec((tm, 256), row), pl.BlockSpec((tm, 256), row),
            pl.BlockSpec((tm, 512), lambda i: (i, 1)),
            pl.BlockSpec((1, 256), const), pl.BlockSpec((1, 256), const),
            pl.BlockSpec((256, 256), const),
            pl.BlockSpec((D, D), const),
            pl.BlockSpec((tm, D), row),
            pl.BlockSpec((1, D), const),
        ],
        out_specs=pl.BlockSpec((tm, D), row),
        out_shape=jax.ShapeDtypeStruct((T, D), F32),
        compiler_params=pltpu.CompilerParams(
            dimension_semantics=("arbitrary",), vmem_limit_bytes=VMEM_LIMIT),
        name="outproj_final" if final else "outproj",
    )(ya, ob, oc, ym, gates, gb, gc, ones256, w, h, fg)


def _block_ones(n, group):
    i = np.arange(n)
    return jnp.asarray((i[:, None] // group) == (i[None, :] // group), dtype=BF16)


def _head_perm_cols():
    return np.concatenate([np.arange(h * HEAD_DIM, (h + 1) * HEAD_DIM) for h in ATT_HEAD_PERM])


def _prep_w_in(w_in):
    o = np.cumsum([0, 256, 128, 128, 256, 768, 8, 8, 256, 128, 128, 256, 32, 256, 256, 256])
    (a_q, a_k, a_v, a_g, d_qkv, d_beta, d_alpha, d_gate, l_q, l_k, l_v, l_low, l_gate, m_q, m_gate) = [
        w_in[:, :, o[i]:o[i + 1]] for i in range(15)]
    perm = _head_perm_cols()
    pad = jnp.zeros(w_in.shape[:2] + (N_COLS - C_SM - 48,), w_in.dtype)
    cols = [a_q[:, :, perm], a_k, a_v, a_g[:, :, perm], d_qkv, d_gate, l_q, l_k, l_v, l_gate, m_q, m_gate,
            d_beta, d_alpha, l_low, pad]
    return jnp.concatenate(cols, axis=-1).astype(BF16)


def _rope_tables(seq):
    rows = seq // GRID_W
    row_pos = jnp.repeat(jnp.arange(rows, dtype=F32), GRID_W)
    col_pos = jnp.tile(jnp.arange(GRID_W, dtype=F32), rows)
    half = HEAD_DIM // 4
    inv_freq = ROPE_THETA ** (-jnp.arange(0, 2 * half, 2, dtype=F32) / (2 * half))
    ang = jnp.stack([row_pos, col_pos], axis=-1)[..., None] * inv_freq
    cos, sin = jnp.cos(ang), jnp.sin(ang)
    cos_h = jnp.concatenate([cos, cos], axis=-1).reshape(seq, HEAD_DIM)
    sin_h = jnp.concatenate([-sin, sin], axis=-1).reshape(seq, HEAD_DIM)
    return jnp.tile(cos_h, (1, 2)), jnp.tile(sin_h, (1, 2))


def _trunk(x, mem, p, consts):
    batch, seq, d_model = x.shape
    T = batch * seq
    depth = p["w_in"].shape[0]
    tm = 1024 if seq % 1024 == 0 else (512 if seq % 512 == 0 else seq)
    tq = 256 if seq % 256 == 0 else seq
    kvc = 512 if seq % 512 == 0 else seq
    cos_t, sin_t = _rope_tables(seq)
    mem2d = mem.reshape(batch * mem.shape[1], d_model)
    memkv = _memkv(mem2d, p["mem_norm_g"], p["w_mem_kv"], min(512, mem2d.shape[0]))
    h = x.reshape(T, d_model)
    for d in range(depth):
        qkv, gates, gdn_qkv, gla_in, mq, small, gdn_sm3 = _inproj(
            h, p["norm_g"][d], p["w_in"][d], cos_t, sin_t, p["qkg"][d], consts["ones384"],
            p["conv_w"][d], p["gdn_par"][d], consts["ones256"], seq, tm)
        ya, ym = _attention(qkv, mq, memkv[d], gates, batch, seq, tq, kvc)
        ob, oc = _recurrent(gdn_qkv, gdn_sm3, gla_in, small, p["wup"][d], p["bg"][d],
                            consts["gdn"], consts["gla"], batch, seq)
        h = _outproj(ya, ob, oc, ym, gates, p["gb"][d], p["gc"][d],
                     consts["ones256"], p["w_out"][d], h, p["final_g"], tm, final=(d == depth - 1))
    return h.reshape(batch, seq, d_model)


def kernel(x_prompt, x_sample, mem_prompt, mem_sample, norm_g, w_in, att_q_norm_g, att_k_norm_g, gdn_conv_w,
           gdn_a_log, gdn_dt_bias, gdn_out_norm_g, gla_w_gate_up, gla_b_gate, gla_out_norm_g, mem_norm_g,
           w_mem_kv, w_out, final_norm_g):
    depth, d_model = norm_g.shape
    perm = _head_perm_cols()
    w_out_p = jnp.concatenate([w_out[:, :256][:, perm], w_out[:, 256:]], axis=1).astype(BF16)
    qkg = jnp.concatenate([jnp.tile(att_q_norm_g, (1, N_HEADS)), jnp.tile(att_k_norm_g, (1, KV_HEADS))], axis=1)
    zeros8 = jnp.zeros((depth, N_COMBO), F32)
    par_a = jnp.concatenate([zeros8, -jnp.exp(gdn_a_log.reshape(depth, N_COMBO)) * LOG2_E,
                             jnp.zeros((depth, LANES - 2 * N_COMBO), F32)], axis=1)
    par_b = jnp.concatenate([zeros8, gdn_dt_bias.reshape(depth, N_COMBO),
                             jnp.zeros((depth, LANES - 2 * N_COMBO), F32)], axis=1)
    gdn_par = jnp.concatenate([par_a[:, None], par_b[:, None], jnp.zeros((depth, 6, LANES), F32)], axis=1)
    conv_w = jnp.concatenate([gdn_conv_w, jnp.zeros((depth, 5, gdn_conv_w.shape[2]), F32)], axis=1)
    wup = jnp.zeros((depth, N_DIRS, LANES, GLA_QW), F32)
    for d in range(N_DIRS):
        lo = 2 * N_COMBO + d * GLA_RANK
        wup = wup.at[:, d, lo:lo + GLA_RANK, :].set(gla_w_gate_up[:, d])
    p = {
        "norm_g": norm_g.reshape(depth, 1, d_model),
        "w_in": _prep_w_in(w_in),
        "qkg": qkg.reshape(depth, 1, 384),
        "conv_w": conv_w,
        "gdn_par": gdn_par,
        "wup": wup.astype(BF16),
        "bg": jnp.concatenate([gla_b_gate, jnp.zeros((depth, 6, GLA_QW), F32)], axis=1),
        "gb": jnp.tile(gdn_out_norm_g, (1, N_HEADS)).reshape(depth, 1, 256),
        "gc": jnp.tile(gla_out_norm_g, (1, N_HEADS)).reshape(depth, 1, 256),
        "mem_norm_g": mem_norm_g.reshape(depth, 1, d_model),
        "w_mem_kv": w_mem_kv.astype(BF16),
        "w_out": w_out_p,
        "final_g": final_norm_g.reshape(1, d_model),
    }
    tri_c, bd, expand = _gdn_constants()
    consts = {
        "ones384": _block_ones(384, HEAD_DIM),
        "ones256": _block_ones(256, HEAD_DIM),
        "gdn": (jnp.asarray(tri_c), jnp.asarray(bd), jnp.asarray(expand, dtype=BF16)),
        "gla": tuple(jnp.asarray(a) for a in _gla_constants()),
    }
    y_prompt = _trunk(x_prompt, mem_prompt, p, consts)
    y_sample = _trunk(x_sample, mem_sample, p, consts)
    return (y_prompt, y_sample)
```

```python
import functools

import numpy as np
import jax
import jax.numpy as jnp
from jax import lax
from jax.experimental import pallas as pl
from jax.experimental.pallas import tpu as pltpu

F32 = jnp.float32
BF16 = jnp.bfloat16

N_HEADS = 4
HEAD_DIM = 64
KV_HEADS = 2
GRID_W = 64
ROPE_THETA = 10000.0
CHUNK = 64
GLA_DK = 32
GLA_RANK = 16
GLA_NORMALIZER = 16.0
EPS = 1e-6
N_DIRS = 2
LANES = 128
GDN_GROUP = 2
GDN_WINDOW = 4
VMEM_LIMIT = 56 * 1024 * 1024

C_AQ, C_AK, C_AV, C_AG = 0, 256, 384, 512
C_DQKV, C_DG = 768, 1536
C_LQ, C_LK, C_LV, C_LG = 1792, 1920, 2048, 2304
C_MQ, C_MG = 2560, 2816
C_SM = 3072
N_COLS = 3200
ATT_HEAD_PERM = (0, 2, 1, 3)
ATT_VT_ROWS = 2 * HEAD_DIM
LOG2_E = float(np.log2(np.e))
ATT_Q_SCALE = HEAD_DIM ** -0.5 * LOG2_E


def _bf(x):
    return x.astype(BF16)


def _mm(a, b):
    return jnp.dot(_bf(a), _bf(b), preferred_element_type=F32)


def _mm_nt(a, b):
    return lax.dot_general(_bf(a), _bf(b), (((1,), (1,)), ((), ())), preferred_element_type=F32)


def _mm_tn(a, b):
    return lax.dot_general(_bf(a), _bf(b), (((0,), (0,)), ((), ())), preferred_element_type=F32)


def _split2(x):
    hi = _bf(x)
    lo = _bf(x - hi.astype(F32))
    return hi, lo


def _split3(x):
    hi = _bf(x).astype(F32)
    r = x - hi
    mid = _bf(r).astype(F32)
    lo = _bf(r - mid).astype(F32)
    return hi, mid, lo


def _group_sums(xs, ones_blocks):
    rows = xs[0].shape[0]
    parts = []
    for x in xs:
        parts.extend(_split2(x))
    s = jnp.dot(jnp.concatenate(parts, axis=0), ones_blocks, preferred_element_type=F32)
    return [s[2 * i * rows:(2 * i + 1) * rows] + s[(2 * i + 1) * rows:(2 * i + 2) * rows] for i in range(len(xs))]


def _silu(x):
    return x * (0.5 * jnp.tanh(0.5 * x) + 0.5)


def _softplus(x):
    return jnp.maximum(x, 0.0) + jnp.log1p(jnp.exp(-jnp.abs(x)))


def _cumsum_rows(x, reverse):
    n = x.shape[0]
    row = lax.broadcasted_iota(jnp.int32, x.shape, 0)
    s = 1
    while s < n:
        if reverse:
            x = x + jnp.where(row < n - s, pltpu.roll(x, n - s, axis=0), 0.0)
        else:
            x = x + jnp.where(row >= s, pltpu.roll(x, s, axis=0), 0.0)
        s *= 2
    return x


def _block_diag(x, mask, reps):
    xb = _bf(x)
    t = jnp.concatenate([xb] * reps, axis=0)
    return jnp.where(mask, t, jnp.zeros_like(t))


def _inproj_kernel(h_ref, hprev_ref, hnext_ref, g_ref, w_ref, cos_ref, sin_ref, qkg_ref, ones_ref,
                   cw_ref, par_ref, ones256_ref,
                   qkv_ref, gates_ref, gdn_ref, gla_ref, mq_ref, small_ref, sm3_ref, *, nseq):
    tm = h_ref.shape[0]

    def normed(x):
        return _bf(x * lax.rsqrt(jnp.mean(x * x, axis=-1, keepdims=True) + EPS) * g_ref[...])

    halo = jnp.concatenate([hprev_ref[0], hnext_ref[0]], axis=0)
    hp = jnp.dot(normed(halo), w_ref[:, C_DQKV:C_DG], preferred_element_type=F32)
    xn = normed(h_ref[...])
    groups = ((C_DQKV, C_DG), (C_SM, N_COLS), (0, C_DQKV), (C_DG, C_SM))
    pieces = [jnp.dot(xn, w_ref[:, a:b], preferred_element_type=F32) for a, b in groups]

    class _Proj:
        def __getitem__(self, idx):
            rows, cols = idx
            for (a, b), piece in zip(groups, pieces):
                if a <= cols.start and cols.stop <= b:
                    return piece[rows, cols.start - a:cols.stop - a]
            raise IndexError(cols)

    proj = _Proj()

    pos = pl.program_id(0) % nseq
    prev_row = hp[7:8] * jnp.where(pos > 0, 1.0, 0.0)
    next_row = hp[8:9] * jnp.where(pos < nseq - 1, 1.0, 0.0)
    xg = proj[:, C_DQKV:C_DG]
    trow = lax.broadcasted_iota(jnp.int32, (tm, 1), 0)
    x_prev = jnp.where(trow == 0, prev_row, pltpu.roll(xg, 1, axis=0))
    x_next = jnp.where(trow == tm - 1, next_row, pltpu.roll(xg, tm - 1, axis=0))
    y = x_prev * cw_ref[0:1, :] + xg * cw_ref[1:2, :]
    y = _silu(y + x_next * cw_ref[2:3, :])
    sss = _group_sums([y[:, 0:256] * y[:, 0:256], y[:, 256:512] * y[:, 256:512]], ones256_ref[...])
    gdn_ref[:, 0:256] = _bf(y[:, 0:256] * lax.rsqrt(sss[0] + EPS) * (HEAD_DIM ** -0.5))
    gdn_ref[:, 256:512] = _bf(y[:, 256:512] * lax.rsqrt(sss[1] + EPS))
    gdn_ref[:, 512:768] = _bf(y[:, 512:768])

    lane = lax.broadcasted_iota(jnp.int32, (1, LANES), 1)
    lane8 = lane < N_COMBO
    fwd_lane = lane < N_HEADS
    for c in range(tm // CHUNK):
        sm = proj[c * CHUNK:(c + 1) * CHUNK, C_SM:N_COLS]
        beta = 0.5 * jnp.tanh(0.5 * sm) + 0.5
        g_all = par_ref[0:1, :] * _softplus(sm + par_ref[1:2, :])
        g = pltpu.roll(g_all, LANES - N_COMBO, axis=1)
        b = jnp.where(fwd_lane, _cumsum_rows(g, reverse=False), _cumsum_rows(g, reverse=True))
        zero = jnp.zeros_like(b)
        q2 = jnp.where(lane8, beta, zero) + pltpu.roll(jnp.where(lane8, b, zero), N_COMBO, axis=1)
        hi, mid, lo3 = _split3(q2)
        sm3_ref[c * CHUNK:(c + 1) * CHUNK, :] = _bf(
            hi + pltpu.roll(mid, 2 * N_COMBO, axis=1) + pltpu.roll(lo3, 4 * N_COMBO, axis=1))

    qk = proj[:, C_AQ:C_AV]
    ms = _group_sums([qk * qk], ones_ref[...])[0] * (1.0 / HEAD_DIM)
    qkn = qk * lax.rsqrt(ms + EPS) * qkg_ref[...]
    cos = cos_ref[...]
    sin = sin_ref[...]
    lane = lax.broadcasted_iota(jnp.int32, (1, LANES), 1)
    first_half = (lane % 32) < 16
    outs = []
    for c in range(3):
        xc = qkn[:, c * LANES:(c + 1) * LANES]
        partner = jnp.where(first_half, pltpu.roll(xc, LANES - 16, axis=1), pltpu.roll(xc, 16, axis=1))
        outs.append(xc * cos + partner * sin)
    qkv_ref[:, 0:128] = _bf(outs[0] * ATT_Q_SCALE)
    qkv_ref[:, 128:256] = _bf(outs[1] * ATT_Q_SCALE)
    qkv_ref[:, 256:384] = _bf(outs[2])
    qkv_ref[:, 384:512] = _bf(proj[:, C_AV:C_AG])

    gates_ref[:, 0:256] = _bf(_silu(proj[:, C_AG:C_AG + 256]))
    gates_ref[:, 256:512] = _bf(_silu(proj[:, C_MG:C_MG + 256]))
    gates_ref[:, 512:768] = _bf(_silu(proj[:, C_DG:C_DG + 256]))
    gates_ref[:, 768:1024] = _bf(_silu(proj[:, C_LG:C_LG + 256]))
    gla_ref[...] = _bf(proj[:, C_LQ:C_LG])
    mq_ref[...] = _bf(proj[:, C_MQ:C_MG] * ATT_Q_SCALE)
    small_ref[...] = proj[:, C_SM:N_COLS]


def _inproj(h, g, w, cos_t, sin_t, qkg, ones384, conv_w, par, ones256, seq, tm):
    T, D = h.shape
    nseq = seq // tm
    row = lambda i: (i, 0)
    const = lambda i: (0, 0)
    h8 = h.reshape(T // 8, 8, D)
    g8 = tm // 8
    return pl.pallas_call(
        functools.partial(_inproj_kernel, nseq=nseq),
        grid=(T // tm,),
        in_specs=[
            pl.BlockSpec((tm, D), row),
            pl.BlockSpec((1, 8, D), lambda i: (jnp.maximum(i * g8 - 1, 0), 0, 0)),
            pl.BlockSpec((1, 8, D), lambda i: (jnp.minimum((i + 1) * g8, T // 8 - 1), 0, 0)),
            pl.BlockSpec((1, D), const),
            pl.BlockSpec((D, N_COLS), const),
            pl.BlockSpec((tm, LANES), lambda i: (i % nseq, 0)),
            pl.BlockSpec((tm, LANES), lambda i: (i % nseq, 0)),
            pl.BlockSpec((1, 384), const),
            pl.BlockSpec((384, 384), const),
            pl.BlockSpec((8, 768), const),
            pl.BlockSpec((8, LANES), const),
            pl.BlockSpec((256, 256), const),
        ],
        out_specs=[
            pl.BlockSpec((tm, 512), row),
            pl.BlockSpec((tm, 1024), row),
            pl.BlockSpec((tm, 768), row),
            pl.BlockSpec((tm, 512), row),
            pl.BlockSpec((tm, 256), row),
            pl.BlockSpec((tm, LANES), row),
            pl.BlockSpec((tm, LANES), row),
        ],
        out_shape=[
            jax.ShapeDtypeStruct((T, 512), BF16),
            jax.ShapeDtypeStruct((T, 1024), BF16),
            jax.ShapeDtypeStruct((T, 768), BF16),
            jax.ShapeDtypeStruct((T, 512), BF16),
            jax.ShapeDtypeStruct((T, 256), BF16),
            jax.ShapeDtypeStruct((T, LANES), F32),
            jax.ShapeDtypeStruct((T, LANES), BF16),
        ],
        compiler_params=pltpu.CompilerParams(
            dimension_semantics=("arbitrary",), vmem_limit_bytes=VMEM_LIMIT),
        name="inproj",
    )(h, h8, h8, g, w, cos_t, sin_t, qkg, ones384, conv_w, par, ones256)


def _memkv_kernel(m_ref, g_ref, w_ref, o_ref):
    x = m_ref[...]
    xn = x * lax.rsqrt(jnp.mean(x * x, axis=-1, keepdims=True) + EPS) * g_ref[0]
    o_ref[0] = _bf(jnp.dot(_bf(xn), w_ref[0], preferred_element_type=F32))


def _memkv(mem2d, g, w, tr):
    R, D = mem2d.shape
    depth = w.shape[0]
    return pl.pallas_call(
        _memkv_kernel,
        grid=(depth, R // tr),
        in_specs=[
            pl.BlockSpec((tr, D), lambda d, i: (i, 0)),
            pl.BlockSpec((1, 1, D), lambda d, i: (d, 0, 0)),
            pl.BlockSpec((1, D, 512), lambda d, i: (d, 0, 0)),
        ],
        out_specs=pl.BlockSpec((1, tr, 512), lambda d, i: (d, i, 0)),
        out_shape=jax.ShapeDtypeStruct((depth, R, 512), BF16),
        compiler_params=pltpu.CompilerParams(
            dimension_semantics=("arbitrary", "arbitrary"), vmem_limit_bytes=VMEM_LIMIT),
        name="memkv",
    )(mem2d, g, w)


def _attn_kernel(q_ref, k_ref, v_ref, mq_ref, km_ref, vm_ref, ga_ref, gm_ref, ya_ref, ym_ref, vt_s, *, kvc):
    tq = q_ref.shape[0]
    seq = k_ref.shape[0]
    lane = lax.broadcasted_iota(jnp.int32, (1, LANES), 1)
    lo = lane < HEAD_DIM

    def stack_heads(x):
        z = jnp.zeros_like(x)
        return [jnp.where(lo, x, z), jnp.where(lo, z, x)]

    @pl.when(pl.program_id(1) == 0)
    def _():
        ones = jnp.ones((ATT_VT_ROWS - HEAD_DIM, kvc), F32)
        for c in range(seq // kvc):
            vt = jnp.transpose(v_ref[c * kvc:(c + 1) * kvc, :].astype(F32))
            vt_s[0, c] = _bf(jnp.concatenate([vt[0:HEAD_DIM], ones], axis=0))
            vt_s[1, c] = _bf(jnp.concatenate([vt[HEAD_DIM:LANES], ones], axis=0))

    q = q_ref[...]
    qm = stack_heads(q[:, :LANES]) + stack_heads(q[:, LANES:])
    n_heads = len(qm)

    half = kvc // 2

    def scores(c, h, part):
        r0 = c * kvc + part * half
        return lax.dot_general(k_ref[r0:r0 + half, :], qm[h], (((1,), (1,)), ((), ())),
                               preferred_element_type=F32)

    ms = [jnp.full((1, tq), -jnp.inf, F32) for _ in range(n_heads)]
    accs = [jnp.zeros((ATT_VT_ROWS, tq), F32) for _ in range(n_heads)]
    nsub = seq // kvc
    sts = [[scores(0, h, part) for part in range(2)] for h in range(n_heads)]
    for c in range(nsub):
        for h in range(n_heads):
            m_new = jnp.maximum(ms[h], jnp.maximum(jnp.max(sts[h][0], axis=0, keepdims=True),
                                                   jnp.max(sts[h][1], axis=0, keepdims=True)))
            acc = jnp.exp2(ms[h] - m_new) * accs[h]
            nxt = [None, None]
            for part in range(2):
                if c + 1 < nsub:
                    nxt[part] = scores(c + 1, h, part)
                p = _bf(jnp.exp2(sts[h][part] - m_new))
                acc = acc + jnp.dot(vt_s[h % 2, c][:, part * half:(part + 1) * half], p,
                                    preferred_element_type=F32)
            accs[h] = acc
            ms[h], sts[h] = m_new, nxt
    o_t = [accs[h][0:HEAD_DIM] / accs[h][HEAD_DIM:HEAD_DIM + 1] for h in range(n_heads)]
    ya = jnp.transpose(jnp.concatenate(o_t, axis=0))
    ya_ref[...] = _bf(ya * ga_ref[...].astype(F32))

    mq = mq_ref[...]
    outs = []
    for c in range(2):
        qc = jnp.concatenate(stack_heads(mq[:, c * LANES:(c + 1) * LANES]), axis=0)
        kc = km_ref[:, c * LANES:(c + 1) * LANES]
        vc = vm_ref[:, c * LANES:(c + 1) * LANES]
        s = lax.dot_general(qc, kc, (((1,), (1,)), ((), ())), preferred_element_type=F32)
        p = jnp.exp2(s - jnp.max(s, axis=-1, keepdims=True))
        om = jnp.dot(_bf(p), vc, preferred_element_type=F32) / jnp.sum(p, axis=-1, keepdims=True)
        outs.append(jnp.where(lo, om[0:tq], om[tq:2 * tq]))
    ym_ref[...] = _bf(jnp.concatenate(outs, axis=1) * gm_ref[...].astype(F32))


def _attention(qkv, mq, memkv, gates, batch, seq, tq, kvc):
    T = qkv.shape[0]
    nq = seq // tq
    mem_len = memkv.shape[0] // batch
    qrow = lambda b, i: (b * nq + i, 0)
    return pl.pallas_call(
        functools.partial(_attn_kernel, kvc=kvc),
        grid=(batch, nq),
        in_specs=[
            pl.BlockSpec((tq, 256), qrow),
            pl.BlockSpec((seq, LANES), lambda b, i: (b, 2)),
            pl.BlockSpec((seq, LANES), lambda b, i: (b, 3)),
            pl.BlockSpec((tq, 256), qrow),
            pl.BlockSpec((mem_len, 256), lambda b, i: (b, 0)),
            pl.BlockSpec((mem_len, 256), lambda b, i: (b, 1)),
            pl.BlockSpec((tq, 256), lambda b, i: (b * nq + i, 0)),
            pl.BlockSpec((tq, 256), lambda b, i: (b * nq + i, 1)),
        ],
        out_specs=[pl.BlockSpec((tq, 256), qrow), pl.BlockSpec((tq, 256), qrow)],
        out_shape=[jax.ShapeDtypeStruct((T, 256), BF16), jax.ShapeDtypeStruct((T, 256), BF16)],
        scratch_shapes=[pltpu.VMEM((KV_HEADS, seq // kvc, ATT_VT_ROWS, kvc), BF16)],
        compiler_params=pltpu.CompilerParams(
            dimension_semantics=("arbitrary", "arbitrary"), vmem_limit_bytes=VMEM_LIMIT),
        name="attention",
    )(qkv, qkv, qkv, mq, memkv, memkv, gates, gates)


GDN_W = GDN_GROUP * HEAD_DIM
GDN_NGRP = N_HEADS // GDN_GROUP
N_COMBO = N_DIRS * N_HEADS
N_QTY = 2


def _gdn_constants():
    w = GDN_W
    i = np.arange(CHUNK)[:, None]
    lane = np.arange(w)[None, :]
    j = lane % CHUNK
    tri = np.stack([i >= j, i <= j]).astype(np.float32)
    stri = np.stack([i > j, i < j]).astype(np.float32)
    eye = (i == j).astype(np.float32)[None]
    tri_c = np.concatenate([tri, stri, eye], axis=0)
    r = np.arange(w)[:, None]
    bd = ((r // CHUNK) == (lane // CHUNK)).astype(np.float32)
    expand = np.zeros((N_DIRS * GDN_NGRP, LANES, N_QTY * w), np.float32)
    for d in range(N_DIRS):
        for g in range(GDN_NGRP):
            for qty in range(N_QTY):
                for hh in range(GDN_GROUP):
                    combo = d * N_HEADS + g * GDN_GROUP + hh
                    for piece in range(3):
                        src = piece * (N_QTY * N_COMBO) + qty * N_COMBO + combo
                        expand[d * GDN_NGRP + g, src, qty * w + hh * CHUNK: qty * w + (hh + 1) * CHUNK] = 1.0
    return tri_c, bd, expand


def _recurrent_kernel(qkv_ref, sm3_ref, tri_ref, bd_ref, exp_ref,
                      gx_ref, gsm_ref, wup_ref, bg_ref, gtri_ref, bdv_ref, stm_ref, kqm_ref,
                      o_ref, oc_ref, st_s, pa_s, pb_s, cd_s, acc_s, gst_s, gacc_s):
    seq = qkv_ref.shape[0]
    nchunk = seq // CHUNK
    w = GDN_W
    gla_init, gla_step, gla_finish = _gla_program(gx_ref, gsm_ref, wup_ref, bg_ref, gtri_ref, bdv_ref, stm_ref,
                                                  kqm_ref, oc_ref, gst_s, gacc_s)
    gla_init()
    acc_s[...] = jnp.zeros_like(acc_s)
    st_s[...] = jnp.zeros_like(st_s)
    bd_mask = bd_ref[...] > 0.5

    combos = [(d, grp) for d in range(N_DIRS) for grp in range(GDN_NGRP)]

    def load_chunk(chunk, d, grp):
        rs = pl.ds(pl.multiple_of(chunk * CHUNK, CHUNK), CHUNK)
        return (qkv_ref[rs, grp * w:(grp + 1) * w], qkv_ref[rs, 256 + grp * w:256 + (grp + 1) * w],
                qkv_ref[rs, 512 + grp * w:512 + (grp + 1) * w], sm3_ref[rs, :])

    nc = len(combos)
    head_masks = [(lax.broadcasted_iota(jnp.int32, (1, w), 1) // HEAD_DIM) == hh for hh in range(GDN_GROUP)]

    items = [(d, grp) for _ in range(GDN_WINDOW) for d, grp in combos]
    ni = len(items)

    def transform_stages(raws, outs):
        exs, diffs, kqs = [], [], []
        a_mats, ps, rs_ = [], [], []

        def stage_products():
            ex_all = [jnp.dot(jnp.concatenate([raws(t * nc + c)[3] for t in range(GDN_WINDOW)], axis=0),
                              exp_ref[c], preferred_element_type=F32) for c in range(nc)]
            for n, (d, grp) in enumerate(items):
                q, k, vb, _ = raws(n)
                ex = ex_all[n % nc][(n // nc) * CHUNK:(n // nc + 1) * CHUNK]
                betax, bx = ex[:, 0:w], ex[:, w:2 * w]
                ebx = jnp.exp2(bx)
                b_end = bx[CHUNK - 1:CHUNK] if d == 0 else bx[0:1]
                exs.append([betax, betax * ebx, ebx, jnp.exp2(b_end - bx)])
                diffs.append(bx - jnp.sum(bx * tri_ref[4], axis=0, keepdims=True))
                kbd = jnp.concatenate([jnp.where(hm, k, jnp.zeros_like(k)) for hm in head_masks], axis=0)
                kqs.append(lax.dot_general(jnp.concatenate([k, q], axis=0), kbd, (((1,), (1,)), ((), ())),
                                           preferred_element_type=F32))

        def stage_square():
            tris = [tri_ref[d] > 0.5 for d in range(N_DIRS)]
            stris = [tri_ref[2 + d] > 0.5 for d in range(N_DIRS)]
            for n, (d, grp) in enumerate(items):
                tri = tris[d]
                stri = stris[d]
                decay = jnp.where(tri, jnp.exp2(jnp.where(tri, diffs[n], 0.0)), 0.0)
                low = jnp.where(stri, exs[n][0] * kqs[n][:CHUNK] * decay, 0.0)
                a_mats.append(kqs[n][CHUNK:] * decay)
                ps.append(_mm(low, _block_diag(low, bd_mask, GDN_GROUP)))
                rs_.append(tri_ref[4] - low)

        def stage_level(last):
            for n in range(ni):
                pbd = _block_diag(ps[n], bd_mask, GDN_GROUP)
                if last:
                    rs_[n] = rs_[n] + _mm(rs_[n], pbd)
                else:
                    rp = _mm(jnp.concatenate([rs_[n], ps[n]], axis=0), pbd)
                    rs_[n] = rs_[n] + rp[:CHUNK]
                    ps[n] = rp[CHUNK:]

        def stage_apply():
            for n, (d, grp) in enumerate(items):
                q, k, vb = raws(n)[:3]
                betax, bebx, ebx, edx = exs[n]
                kf = k.astype(F32)
                u = _mm(rs_[n], _block_diag(betax * vb.astype(F32), bd_mask, GDN_GROUP))
                wm = _mm(rs_[n], _block_diag(bebx * kf, bd_mask, GDN_GROUP))
                cd_row = ebx[CHUNK - 1:CHUNK] if d == 0 else ebx[0:1]
                outs.append((a_mats[n], u, _bf(wm), _bf(q.astype(F32) * ebx), _bf(kf * edx), cd_row))

        return ([stage_products, stage_square] + [functools.partial(stage_level, False)] * 4
                + [functools.partial(stage_level, True), stage_apply])

    def store_transformed(c, t):
        a_mat, u, wm, q_dec, k_dec, cd_row = t
        pa_s[2 * c] = a_mat
        pa_s[2 * c + 1] = u
        pb_s[3 * c] = wm
        pb_s[3 * c + 1] = q_dec
        pb_s[3 * c + 2] = k_dec
        cd_s[c] = jnp.broadcast_to(cd_row, (8, w))

    def load_transformed(c):
        return pa_s[2 * c], pa_s[2 * c + 1], pb_s[3 * c], pb_s[3 * c + 1], pb_s[3 * c + 2], cd_s[c][0:1]

    def scan_stages(states, cur, os_):
        tmp = {}

        def stage_ws(t):
            tmp["cur"] = [cur(t * nc + c) for c in range(nc)]
            tmp["ws"] = [_mm(jnp.concatenate([tmp["cur"][c][2], tmp["cur"][c][3]], axis=0), states[c])
                         for c in range(nc)]

        def stage_update(t):
            for c in range(nc):
                a_mat, u, _, _, k_dec, cd_row = tmp["cur"][c]
                v_new = u - tmp["ws"][c][:CHUNK]
                os_.append(tmp["ws"][c][CHUNK:] + _mm(a_mat, _block_diag(v_new, bd_mask, GDN_GROUP)))
                states[c] = cd_row * states[c] + jnp.where(bd_mask, _mm_tn(k_dec, v_new), 0.0)

        stages = []
        for t in range(GDN_WINDOW):
            stages += [functools.partial(stage_ws, t), functools.partial(stage_update, t)]
        return stages

    def chunk_of(j, t, d):
        i = j * GDN_WINDOW + t
        return i if d == 0 else nchunk - 1 - i

    def window_loader(j):
        def load(n):
            d, grp = items[n]
            return load_chunk(chunk_of(j, n // nc, d), d, grp)
        return load

    first = []
    for stage in transform_stages(window_loader(0), first):
        stage()
    for n in range(ni):
        store_transformed(n, first[n])

    nwin = nchunk // GDN_WINDOW

    def step(j, carry, transform_next=True):
        states = [st_s[c] for c in range(nc)]
        os_, nexts = [], []
        t_stages = transform_stages(window_loader(j + 1), nexts) if transform_next else []
        s_stages = scan_stages(states, load_transformed, os_)
        g_stages, gla_stores = gla_step(j)
        g_at = {2 * i: g for i, g in enumerate(g_stages)}
        for k in range(max(len(t_stages), len(s_stages))):
            if k < len(t_stages):
                t_stages[k]()
            if k < len(s_stages):
                s_stages[k]()
            if k in g_at:
                g_at[k]()
        for t in range(GDN_WINDOW):
            for c, (d, grp) in enumerate(combos):
                rs = pl.ds(pl.multiple_of(chunk_of(j, t, d) * CHUNK, CHUNK), CHUNK)
                acc_s[rs, grp * w:(grp + 1) * w] += os_[t * nc + c]
        for c in range(nc):
            st_s[c] = states[c]
        for n in range(len(nexts)):
            store_transformed(n, nexts[n])
        gla_stores()
        return carry

    lax.fori_loop(0, nwin - 1, step, 0)
    step(jnp.int32(nwin - 1), 0, transform_next=False)
    o_ref[...] = _bf(acc_s[...])
    gla_finish()


def _recurrent(gdn_qkv, gdn_sm3, gla_in, small, wup, bg, gdn_consts, gla_consts, batch, seq):
    T = gdn_qkv.shape[0]
    assert GDN_WINDOW == GLA_WINDOW and seq % (CHUNK * GDN_WINDOW) == 0
    tri_c, bd, expand = gdn_consts
    tri, bd_v, st_mask, kq_mask = gla_consts
    whole = lambda b: (b, 0)
    c2 = lambda b: (0, 0)
    c3 = lambda b: (0, 0, 0)
    out = pl.BlockSpec((seq, 256), whole)
    return pl.pallas_call(
        _recurrent_kernel,
        grid=(batch,),
        in_specs=[
            pl.BlockSpec((seq, 768), whole),
            pl.BlockSpec((seq, LANES), whole),
            pl.BlockSpec(tri_c.shape, c3),
            pl.BlockSpec(bd.shape, c2),
            pl.BlockSpec(expand.shape, c3),
            pl.BlockSpec((seq, 512), whole),
            pl.BlockSpec((seq, LANES), whole),
            pl.BlockSpec(wup.shape, c3),
            pl.BlockSpec(bg.shape, c2),
            pl.BlockSpec(tri.shape, c3),
            pl.BlockSpec(bd_v.shape, c2),
            pl.BlockSpec(st_mask.shape, c2),
            pl.BlockSpec(kq_mask.shape, c2),
        ],
        out_specs=[out, out],
        out_shape=[jax.ShapeDtypeStruct((T, 256), BF16), jax.ShapeDtypeStruct((T, 256), BF16)],
        scratch_shapes=[
            pltpu.VMEM((N_DIRS * GDN_NGRP, GDN_W, GDN_W), F32),
            pltpu.VMEM((2 * GDN_WINDOW * N_DIRS * GDN_NGRP, CHUNK, GDN_W), F32),
            pltpu.VMEM((3 * GDN_WINDOW * N_DIRS * GDN_NGRP, CHUNK, GDN_W), BF16),
            pltpu.VMEM((GDN_WINDOW * N_DIRS * GDN_NGRP, 8, GDN_W), F32),
            pltpu.VMEM((seq, 256), F32),
            pltpu.VMEM((N_DIRS, GLA_QW, GLA_VW), F32),
            pltpu.VMEM((seq, 256), F32),
        ],
        compiler_params=pltpu.CompilerParams(
            dimension_semantics=("arbitrary",), vmem_limit_bytes=VMEM_LIMIT),
        name="recurrent",
    )(gdn_qkv, gdn_sm3, tri_c, bd, expand, gla_in, small, wup, bg, tri, bd_v, st_mask, kq_mask)


GLA_QW = N_HEADS * GLA_DK
GLA_VW = N_HEADS * HEAD_DIM
GLA_WINDOW = 4


def _gla_constants():
    i = np.arange(CHUNK)[:, None]
    lane = np.arange(GLA_VW)[None, :]
    j = lane % CHUNK
    tri = np.stack([i >= j, i <= j]).astype(np.float32)
    r = np.arange(GLA_VW)[:, None]
    bd_v = ((r // CHUNK) == (lane // HEAD_DIM)).astype(np.float32)
    rs = np.arange(GLA_QW)[:, None]
    st_mask = ((rs // GLA_DK) == (lane // HEAD_DIM)).astype(np.float32)
    kq_mask = ((r // CHUNK) == (np.arange(GLA_QW)[None, :] // GLA_DK)).astype(np.float32)
    return tri, bd_v, st_mask, kq_mask


def _gla_program(x_ref, sm_ref, wup_ref, bg_ref, tri_ref, bdv_ref, stm_ref, kqm_ref, o_ref, st_s, acc_s):
    seq = x_ref.shape[0]
    nchunk = seq // CHUNK
    bdv_mask = bdv_ref[...] > 0.5
    st_mask = stm_ref[...] > 0.5
    kq_mask = kqm_ref[...] > 0.5

    def init():
        acc_s[...] = jnp.zeros_like(acc_s)
        st_s[...] = jnp.zeros_like(st_s)

    def rows_of(chunk):
        return pl.ds(pl.multiple_of(chunk * CHUNK, CHUNK), CHUNK)

    def load_chunk(chunk):
        rs = rows_of(chunk)
        return (x_ref[rs, 0:GLA_QW].astype(F32), x_ref[rs, GLA_QW:2 * GLA_QW].astype(F32),
                x_ref[rs, 2 * GLA_QW:2 * GLA_QW + GLA_VW], sm_ref[rs, :])

    items = [(t, d) for t in range(GLA_WINDOW) for d in range(N_DIRS)]
    ni = len(items)

    def chunk_of(jw, t, d):
        i = jw * GLA_WINDOW + t
        return i if d == 0 else nchunk - 1 - i

    def step(jw):
        raws, logits, q_es, attns, dsts, cds, outs = [], [], [], [], [], [], []
        states = []

        def stage_logits():
            for n, (t, d) in enumerate(items):
                raws.append(load_chunk(chunk_of(jw, t, d)))
                logits.append(jnp.dot(_bf(raws[n][3]), wup_ref[d], preferred_element_type=F32)
                              + bg_ref[d:d + 1, :])

        def stage_intra():
            tri_masks = [tri_ref[d] > 0.5 for d in range(N_DIRS)]
            for n, (t, d) in enumerate(items):
                q, k, v, _ = raws[n]
                gk = (jnp.minimum(logits[n], 0.0) - jnp.log1p(jnp.exp(-jnp.abs(logits[n])))) \
                    * (LOG2_E / GLA_NORMALIZER)
                b = _cumsum_rows(gk, reverse=(d == 1))
                b_end = b[CHUNK - 1:CHUNK] if d == 0 else b[0:1]
                q_e = q * (GLA_DK ** -0.5) * jnp.exp2(b)
                keb = _bf(k * jnp.exp2(-b))
                kebd = jnp.where(kq_mask, jnp.concatenate([keb] * N_HEADS, axis=0),
                                 jnp.zeros((GLA_VW, GLA_QW), BF16))
                q_es.append(_bf(q_e))
                attns.append(jnp.where(tri_masks[d], _mm_nt(q_es[n], kebd), 0.0))
                dsts.append(jnp.where(st_mask, _mm_tn(k * jnp.exp2(b_end - b), v), 0.0))
                cd_col = jnp.transpose(jnp.broadcast_to(jnp.exp2(b_end), (GLA_QW, GLA_QW)))
                cds.append(jnp.concatenate([cd_col, cd_col], axis=1))

        def stage_out():
            states.extend(st_s[d] for d in range(N_DIRS))
            for n in range(ni):
                outs.append(_mm(attns[n], _block_diag(raws[n][2], bdv_mask, N_HEADS)))
            for n, (t, d) in enumerate(items):
                outs[n] = outs[n] + _mm(q_es[n], states[d])
                states[d] = cds[n] * states[d] + dsts[n]

        def stores():
            for n, (t, d) in enumerate(items):
                acc_s[rows_of(chunk_of(jw, t, d)), :] += outs[n]
            for d in range(N_DIRS):
                st_s[d] = states[d]

        return [stage_logits, stage_intra, stage_out], stores

    def finish():
        o_ref[...] = _bf(acc_s[...])

    return init, step, finish


def _outproj_kernel(ya_ref, ob_ref, oc_ref, ym_ref, gates_ref, gb_ref, gc_ref, ones_ref, w_ref, h_ref,
                    fg_ref, o_ref, *, final):
    ones256 = ones_ref[...]

    ob = ob_ref[...].astype(F32)
    oc = oc_ref[...].astype(F32)
    ssb, ssc = _group_sums([ob * ob, oc * oc], ones256)
    yb = ob * lax.rsqrt(ssb * (1.0 / HEAD_DIM) + EPS) * gb_ref[...] * gates_ref[:, 0:256].astype(F32)
    yc = oc * lax.rsqrt(ssc * (1.0 / HEAD_DIM) + EPS) * gc_ref[...] * gates_ref[:, 256:512].astype(F32)
    y = jnp.dot(ya_ref[...], w_ref[0:256, :], preferred_element_type=F32)
    y = y + jnp.dot(_bf(yb), w_ref[256:512, :], preferred_element_type=F32)
    y = y + jnp.dot(_bf(yc), w_ref[512:768, :], preferred_element_type=F32)
    y = y + jnp.dot(ym_ref[...], w_ref[768:1024, :], preferred_element_type=F32)
    hn = h_ref[...] + y
    if final:
        hn = hn * lax.rsqrt(jnp.mean(hn * hn, axis=-1, keepdims=True) + EPS) * fg_ref[...]
    o_ref[...] = hn


def _outproj(ya, ob, oc, ym, gates, gb, gc, ones256, w, h, fg, tm, final):
    T, D = h.shape
    row = lambda i: (i, 0)
    const = lambda i: (0, 0)
    return pl.pallas_call(
        functools.partial(_outproj_kernel, final=final),
        grid=(T // tm,),
        in_specs=[
            pl.BlockSpec((tm, 256), row), pl.BlockSpec((tm, 256), row),
            pl.BlockSpec((tm, 256), row), pl.BlockSpec((tm, 256), row),
            pl.BlockSpec((tm, 512), lambda i: (i, 1)),
            pl.BlockSpec((1, 256), const), pl.BlockSpec((1, 256), const),
            pl.BlockSpec((256, 256), const),
            pl.BlockSpec((D, D), const),
            pl.BlockSpec((tm, D), row),
            pl.BlockSpec((1, D), const),
        ],
        out_specs=pl.BlockSpec((tm, D), row),
        out_shape=jax.ShapeDtypeStruct((T, D), F32),
        compiler_params=pltpu.CompilerParams(
            dimension_semantics=("arbitrary",), vmem_limit_bytes=VMEM_LIMIT),
        name="outproj_final" if final else "outproj",
    )(ya, ob, oc, ym, gates, gb, gc, ones256, w, h, fg)


def _block_ones(n, group):
    i = np.arange(n)
    return jnp.asarray((i[:, None] // group) == (i[None, :] // group), dtype=BF16)


def _head_perm_cols():
    return np.concatenate([np.arange(h * HEAD_DIM, (h + 1) * HEAD_DIM) for h in ATT_HEAD_PERM])


def _prep_w_in(w_in):
    o = np.cumsum([0, 256, 128, 128, 256, 768, 8, 8, 256, 128, 128, 256, 32, 256, 256, 256])
    (a_q, a_k, a_v, a_g, d_qkv, d_beta, d_alpha, d_gate, l_q, l_k, l_v, l_low, l_gate, m_q, m_gate) = [
        w_in[:, :, o[i]:o[i + 1]] for i in range(15)]
    perm = _head_perm_cols()
    pad = jnp.zeros(w_in.shape[:2] + (N_COLS - C_SM - 48,), w_in.dtype)
    cols = [a_q[:, :, perm], a_k, a_v, a_g[:, :, perm], d_qkv, d_gate, l_q, l_k, l_v, l_gate, m_q, m_gate,
            d_beta, d_alpha, l_low, pad]
    return jnp.concatenate(cols, axis=-1).astype(BF16)


def _rope_tables(seq):
    rows = seq // GRID_W
    row_pos = jnp.repeat(jnp.arange(rows, dtype=F32), GRID_W)
    col_pos = jnp.tile(jnp.arange(GRID_W, dtype=F32), rows)
    half = HEAD_DIM // 4
    inv_freq = ROPE_THETA ** (-jnp.arange(0, 2 * half, 2, dtype=F32) / (2 * half))
    ang = jnp.stack([row_pos, col_pos], axis=-1)[..., None] * inv_freq
    cos, sin = jnp.cos(ang), jnp.sin(ang)
    cos_h = jnp.concatenate([cos, cos], axis=-1).reshape(seq, HEAD_DIM)
    sin_h = jnp.concatenate([-sin, sin], axis=-1).reshape(seq, HEAD_DIM)
    return jnp.tile(cos_h, (1, 2)), jnp.tile(sin_h, (1, 2))


def _trunk(x, mem, p, consts):
    batch, seq, d_model = x.shape
    T = batch * seq
    depth = p["w_in"].shape[0]
    tm = 1024 if seq % 1024 == 0 else (512 if seq % 512 == 0 else seq)
    tq = 256 if seq % 256 == 0 else seq
    kvc = 512 if seq % 512 == 0 else seq
    cos_t, sin_t = _rope_tables(seq)
    mem2d = mem.reshape(batch * mem.shape[1], d_model)
    memkv = _memkv(mem2d, p["mem_norm_g"], p["w_mem_kv"], min(512, mem2d.shape[0]))
    h = x.reshape(T, d_model)
    for d in range(depth):
        qkv, gates, gdn_qkv, gla_in, mq, small, gdn_sm3 = _inproj(
            h, p["norm_g"][d], p["w_in"][d], cos_t, sin_t, p["qkg"][d], consts["ones384"],
            p["conv_w"][d], p["gdn_par"][d], consts["ones256"], seq, tm)
        ya, ym = _attention(qkv, mq, memkv[d], gates, batch, seq, tq, kvc)
        ob, oc = _recurrent(gdn_qkv, gdn_sm3, gla_in, small, p["wup"][d], p["bg"][d],
                            consts["gdn"], consts["gla"], batch, seq)
        h = _outproj(ya, ob, oc, ym, gates, p["gb"][d], p["gc"][d],
                     consts["ones256"], p["w_out"][d], h, p["final_g"], tm, final=(d == depth - 1))
    return h.reshape(batch, seq, d_model)


def kernel(x_prompt, x_sample, mem_prompt, mem_sample, norm_g, w_in, att_q_norm_g, att_k_norm_g, gdn_conv_w,
           gdn_a_log, gdn_dt_bias, gdn_out_norm_g, gla_w_gate_up, gla_b_gate, gla_out_norm_g, mem_norm_g,
           w_mem_kv, w_out, final_norm_g):
    depth, d_model = norm_g.shape
    perm = _head_perm_cols()
    w_out_p = jnp.concatenate([w_out[:, :256][:, perm], w_out[:, 256:]], axis=1).astype(BF16)
    qkg = jnp.concatenate([jnp.tile(att_q_norm_g, (1, N_HEADS)), jnp.tile(att_k_norm_g, (1, KV_HEADS))], axis=1)
    zeros8 = jnp.zeros((depth, N_COMBO), F32)
    par_a = jnp.concatenate([zeros8, -jnp.exp(gdn_a_log.reshape(depth, N_COMBO)) * LOG2_E,
                             jnp.zeros((depth, LANES - 2 * N_COMBO), F32)], axis=1)
    par_b = jnp.concatenate([zeros8, gdn_dt_bias.reshape(depth, N_COMBO),
                             jnp.zeros((depth, LANES - 2 * N_COMBO), F32)], axis=1)
    gdn_par = jnp.concatenate([par_a[:, None], par_b[:, None], jnp.zeros((depth, 6, LANES), F32)], axis=1)
    conv_w = jnp.concatenate([gdn_conv_w, jnp.zeros((depth, 5, gdn_conv_w.shape[2]), F32)], axis=1)
    wup = jnp.zeros((depth, N_DIRS, LANES, GLA_QW), F32)
    for d in range(N_DIRS):
        lo = 2 * N_COMBO + d * GLA_RANK
        wup = wup.at[:, d, lo:lo + GLA_RANK, :].set(gla_w_gate_up[:, d])
    p = {
        "norm_g": norm_g.reshape(depth, 1, d_model),
        "w_in": _prep_w_in(w_in),
        "qkg": qkg.reshape(depth, 1, 384),
        "conv_w": conv_w,
        "gdn_par": gdn_par,
        "wup": wup.astype(BF16),
        "bg": jnp.concatenate([gla_b_gate, jnp.zeros((depth, 6, GLA_QW), F32)], axis=1),
        "gb": jnp.tile(gdn_out_norm_g, (1, N_HEADS)).reshape(depth, 1, 256),
        "gc": jnp.tile(gla_out_norm_g, (1, N_HEADS)).reshape(depth, 1, 256),
        "mem_norm_g": mem_norm_g.reshape(depth, 1, d_model),
        "w_mem_kv": w_mem_kv.astype(BF16),
        "w_out": w_out_p,
        "final_g": final_norm_g.reshape(1, d_model),
    }
    tri_c, bd, expand = _gdn_constants()
    consts = {
        "ones384": _block_ones(384, HEAD_DIM),
        "ones256": _block_ones(256, HEAD_DIM),
        "gdn": (jnp.asarray(tri_c), jnp.asarray(bd), jnp.asarray(expand, dtype=BF16)),
        "gla": tuple(jnp.asarray(a) for a in _gla_constants()),
    }
    y_prompt = _trunk(x_prompt, mem_prompt, p, consts)
    y_sample = _trunk(x_sample, mem_sample, p, consts)
    return (y_prompt, y_sample)
```

```python
import functools

import numpy as np
import jax
import jax.numpy as jnp
from jax import lax
from jax.experimental import pallas as pl
from jax.experimental.pallas import tpu as pltpu

F32 = jnp.float32
BF16 = jnp.bfloat16

N_HEADS = 4
HEAD_DIM = 64
KV_HEADS = 2
GRID_W = 64
ROPE_THETA = 10000.0
CHUNK = 64
GLA_DK = 32
GLA_RANK = 16
GLA_NORMALIZER = 16.0
EPS = 1e-6
N_DIRS = 2
LANES = 128
GDN_GROUP = 2
GDN_WINDOW = 4
VMEM_LIMIT = 56 * 1024 * 1024

C_AQ, C_AK, C_AV, C_AG = 0, 256, 384, 512
C_DQKV, C_DG = 768, 1536
C_LQ, C_LK, C_LV, C_LG = 1792, 1920, 2048, 2304
C_MQ, C_MG = 2560, 2816
C_SM = 3072
N_COLS = 3200
ATT_HEAD_PERM = (0, 2, 1, 3)
ATT_VT_ROWS = 2 * HEAD_DIM
LOG2_E = float(np.log2(np.e))
ATT_Q_SCALE = HEAD_DIM ** -0.5 * LOG2_E


def _bf(x):
    return x.astype(BF16)


def _mm(a, b):
    return jnp.dot(_bf(a), _bf(b), preferred_element_type=F32)


def _mm_nt(a, b):
    return lax.dot_general(_bf(a), _bf(b), (((1,), (1,)), ((), ())), preferred_element_type=F32)


def _mm_tn(a, b):
    return lax.dot_general(_bf(a), _bf(b), (((0,), (0,)), ((), ())), preferred_element_type=F32)


def _split2(x):
    hi = _bf(x)
    lo = _bf(x - hi.astype(F32))
    return hi, lo


def _split3(x):
    hi = _bf(x).astype(F32)
    r = x - hi
    mid = _bf(r).astype(F32)
    lo = _bf(r - mid).astype(F32)
    return hi, mid, lo


def _group_sums(xs, ones_blocks):
    rows = xs[0].shape[0]
    s = jnp.dot(jnp.concatenate([_bf(x) for x in xs], axis=0), ones_blocks, preferred_element_type=F32)
    return [s[i * rows:(i + 1) * rows] for i in range(len(xs))]


def _silu(x):
    return x * (0.5 * jnp.tanh(0.5 * x) + 0.5)


def _softplus(x):
    return jnp.maximum(x, 0.0) + jnp.log1p(jnp.exp(-jnp.abs(x)))


def _cumsum_rows(x, reverse):
    n = x.shape[0]
    row = lax.broadcasted_iota(jnp.int32, x.shape, 0)
    s = 1
    while s < n:
        if reverse:
            x = x + jnp.where(row < n - s, pltpu.roll(x, n - s, axis=0), 0.0)
        else:
            x = x + jnp.where(row >= s, pltpu.roll(x, s, axis=0), 0.0)
        s *= 2
    return x


def _block_diag(x, mask, reps):
    xb = _bf(x)
    t = jnp.concatenate([xb] * reps, axis=0)
    return jnp.where(mask, t, jnp.zeros_like(t))


def _inproj_kernel(h_ref, hprev_ref, hnext_ref, g_ref, w_ref, cos_ref, sin_ref, qkg_ref, ones_ref,
                   cw_ref, par_ref, ones256_ref,
                   qkv_ref, gates_ref, gdn_ref, gla_ref, mq_ref, small_ref, sm3_ref, *, nseq):
    tm = h_ref.shape[0]

    def normed(x):
        return _bf(x * lax.rsqrt(jnp.mean(x * x, axis=-1, keepdims=True) + EPS) * g_ref[...])

    halo = jnp.concatenate([hprev_ref[0], hnext_ref[0]], axis=0)
    hp = jnp.dot(normed(halo), w_ref[:, C_DQKV:C_DG], preferred_element_type=F32)
    xn = normed(h_ref[...])
    groups = ((C_DQKV, C_DG), (C_SM, N_COLS), (0, C_DQKV), (C_DG, C_SM))
    pieces = [jnp.dot(xn, w_ref[:, a:b], preferred_element_type=F32) for a, b in groups]

    class _Proj:
        def __getitem__(self, idx):
            rows, cols = idx
            for (a, b), piece in zip(groups, pieces):
                if a <= cols.start and cols.stop <= b:
                    return piece[rows, cols.start - a:cols.stop - a]
            raise IndexError(cols)

    proj = _Proj()

    pos = pl.program_id(0) % nseq
    prev_row = hp[7:8] * jnp.where(pos > 0, 1.0, 0.0)
    next_row = hp[8:9] * jnp.where(pos < nseq - 1, 1.0, 0.0)
    xg = proj[:, C_DQKV:C_DG]
    trow = lax.broadcasted_iota(jnp.int32, (tm, 1), 0)
    x_prev = jnp.where(trow == 0, prev_row, pltpu.roll(xg, 1, axis=0))
    x_next = jnp.where(trow == tm - 1, next_row, pltpu.roll(xg, tm - 1, axis=0))
    y = x_prev * cw_ref[0:1, :] + xg * cw_ref[1:2, :]
    y = _silu(y + x_next * cw_ref[2:3, :])
    sss = _group_sums([y[:, 0:256] * y[:, 0:256], y[:, 256:512] * y[:, 256:512]], ones256_ref[...])
    gdn_ref[:, 0:256] = _bf(y[:, 0:256] * lax.rsqrt(sss[0] + EPS) * (HEAD_DIM ** -0.5))
    gdn_ref[:, 256:512] = _bf(y[:, 256:512] * lax.rsqrt(sss[1] + EPS))
    gdn_ref[:, 512:768] = _bf(y[:, 512:768])

    lane = lax.broadcasted_iota(jnp.int32, (1, LANES), 1)
    lane8 = lane < N_COMBO
    fwd_lane = lane < N_HEADS
    for c in range(tm // CHUNK):
        sm = proj[c * CHUNK:(c + 1) * CHUNK, C_SM:N_COLS]
        beta = 0.5 * jnp.tanh(0.5 * sm) + 0.5
        g_all = par_ref[0:1, :] * _softplus(sm + par_ref[1:2, :])
        g = pltpu.roll(g_all, LANES - N_COMBO, axis=1)
        b = jnp.where(fwd_lane, _cumsum_rows(g, reverse=False), _cumsum_rows(g, reverse=True))
        zero = jnp.zeros_like(b)
        q2 = jnp.where(lane8, beta, zero) + pltpu.roll(jnp.where(lane8, b, zero), N_COMBO, axis=1)
        hi, mid, lo3 = _split3(q2)
        sm3_ref[c * CHUNK:(c + 1) * CHUNK, :] = _bf(
            hi + pltpu.roll(mid, 2 * N_COMBO, axis=1) + pltpu.roll(lo3, 4 * N_COMBO, axis=1))

    qk = proj[:, C_AQ:C_AV]
    ms = _group_sums([qk * qk], ones_ref[...])[0] * (1.0 / HEAD_DIM)
    qkn = qk * lax.rsqrt(ms + EPS) * qkg_ref[...]
    cos = cos_ref[...]
    sin = sin_ref[...]
    lane = lax.broadcasted_iota(jnp.int32, (1, LANES), 1)
    first_half = (lane % 32) < 16
    outs = []
    for c in range(3):
        xc = qkn[:, c * LANES:(c + 1) * LANES]
        partner = jnp.where(first_half, pltpu.roll(xc, LANES - 16, axis=1), pltpu.roll(xc, 16, axis=1))
        outs.append(xc * cos + partner * sin)
    qkv_ref[:, 0:128] = _bf(outs[0] * ATT_Q_SCALE)
    qkv_ref[:, 128:256] = _bf(outs[1] * ATT_Q_SCALE)
    qkv_ref[:, 256:384] = _bf(outs[2])
    qkv_ref[:, 384:512] = _bf(proj[:, C_AV:C_AG])

    gates_ref[:, 0:256] = _bf(_silu(proj[:, C_AG:C_AG + 256]))
    gates_ref[:, 256:512] = _bf(_silu(proj[:, C_MG:C_MG + 256]))
    gates_ref[:, 512:768] = _bf(_silu(proj[:, C_DG:C_DG + 256]))
    gates_ref[:, 768:1024] = _bf(_silu(proj[:, C_LG:C_LG + 256]))
    gla_ref[...] = _bf(proj[:, C_LQ:C_LG])
    mq_ref[...] = _bf(proj[:, C_MQ:C_MG] * ATT_Q_SCALE)
    small_ref[...] = proj[:, C_SM:N_COLS]


def _inproj(h, g, w, cos_t, sin_t, qkg, ones384, conv_w, par, ones256, seq, tm):
    T, D = h.shape
    nseq = seq // tm
    row = lambda i: (i, 0)
    const = lambda i: (0, 0)
    h8 = h.reshape(T // 8, 8, D)
    g8 = tm // 8
    return pl.pallas_call(
        functools.partial(_inproj_kernel, nseq=nseq),
        grid=(T // tm,),
        in_specs=[
            pl.BlockSpec((tm, D), row),
            pl.BlockSpec((1, 8, D), lambda i: (jnp.maximum(i * g8 - 1, 0), 0, 0)),
            pl.BlockSpec((1, 8, D), lambda i: (jnp.minimum((i + 1) * g8, T // 8 - 1), 0, 0)),
            pl.BlockSpec((1, D), const),
            pl.BlockSpec((D, N_COLS), const),
            pl.BlockSpec((tm, LANES), lambda i: (i % nseq, 0)),
            pl.BlockSpec((tm, LANES), lambda i: (i % nseq, 0)),
            pl.BlockSpec((1, 384), const),
            pl.BlockSpec((384, 384), const),
            pl.BlockSpec((8, 768), const),
            pl.BlockSpec((8, LANES), const),
            pl.BlockSpec((256, 256), const),
        ],
        out_specs=[
            pl.BlockSpec((tm, 512), row),
            pl.BlockSpec((tm, 1024), row),
            pl.BlockSpec((tm, 768), row),
            pl.BlockSpec((tm, 512), row),
            pl.BlockSpec((tm, 256), row),
            pl.BlockSpec((tm, LANES), row),
            pl.BlockSpec((tm, LANES), row),
        ],
        out_shape=[
            jax.ShapeDtypeStruct((T, 512), BF16),
            jax.ShapeDtypeStruct((T, 1024), BF16),
            jax.ShapeDtypeStruct((T, 768), BF16),
            jax.ShapeDtypeStruct((T, 512), BF16),
            jax.ShapeDtypeStruct((T, 256), BF16),
            jax.ShapeDtypeStruct((T, LANES), F32),
            jax.ShapeDtypeStruct((T, LANES), BF16),
        ],
        compiler_params=pltpu.CompilerParams(
            dimension_semantics=("arbitrary",), vmem_limit_bytes=VMEM_LIMIT),
        name="inproj",
    )(h, h8, h8, g, w, cos_t, sin_t, qkg, ones384, conv_w, par, ones256)


def _memkv_kernel(m_ref, g_ref, w_ref, o_ref):
    x = m_ref[...]
    xn = x * lax.rsqrt(jnp.mean(x * x, axis=-1, keepdims=True) + EPS) * g_ref[0]
    o_ref[0] = _bf(jnp.dot(_bf(xn), w_ref[0], preferred_element_type=F32))


def _memkv(mem2d, g, w, tr):
    R, D = mem2d.shape
    depth = w.shape[0]
    return pl.pallas_call(
        _memkv_kernel,
        grid=(depth, R // tr),
        in_specs=[
            pl.BlockSpec((tr, D), lambda d, i: (i, 0)),
            pl.BlockSpec((1, 1, D), lambda d, i: (d, 0, 0)),
            pl.BlockSpec((1, D, 512), lambda d, i: (d, 0, 0)),
        ],
        out_specs=pl.BlockSpec((1, tr, 512), lambda d, i: (d, i, 0)),
        out_shape=jax.ShapeDtypeStruct((depth, R, 512), BF16),
        compiler_params=pltpu.CompilerParams(
            dimension_semantics=("arbitrary", "arbitrary"), vmem_limit_bytes=VMEM_LIMIT),
        name="memkv",
    )(mem2d, g, w)


def _attn_kernel(q_ref, k_ref, v_ref, mq_ref, km_ref, vm_ref, ga_ref, gm_ref, ya_ref, ym_ref, vt_s, *, kvc):
    tq = q_ref.shape[0]
    seq = k_ref.shape[0]
    lane = lax.broadcasted_iota(jnp.int32, (1, LANES), 1)
    lo = lane < HEAD_DIM

    def stack_heads(x):
        z = jnp.zeros_like(x)
        return [jnp.where(lo, x, z), jnp.where(lo, z, x)]

    @pl.when(pl.program_id(1) == 0)
    def _():
        ones = jnp.ones((ATT_VT_ROWS - HEAD_DIM, kvc), F32)
        for c in range(seq // kvc):
            vt = jnp.transpose(v_ref[c * kvc:(c + 1) * kvc, :].astype(F32))
            vt_s[0, c] = _bf(jnp.concatenate([vt[0:HEAD_DIM], ones], axis=0))
            vt_s[1, c] = _bf(jnp.concatenate([vt[HEAD_DIM:LANES], ones], axis=0))

    q = q_ref[...]
    qm = stack_heads(q[:, :LANES]) + stack_heads(q[:, LANES:])
    n_heads = len(qm)

    half = kvc // 2

    def scores(c, h, part):
        r0 = c * kvc + part * half
        return lax.dot_general(k_ref[r0:r0 + half, :], qm[h], (((1,), (1,)), ((), ())),
                               preferred_element_type=F32)

    ms = [jnp.full((1, tq), -jnp.inf, F32) for _ in range(n_heads)]
    accs = [jnp.zeros((ATT_VT_ROWS, tq), F32) for _ in range(n_heads)]
    nsub = seq // kvc
    sts = [[scores(0, h, part) for part in range(2)] for h in range(n_heads)]
    for c in range(nsub):
        for h in range(n_heads):
            m_new = jnp.maximum(ms[h], jnp.maximum(jnp.max(sts[h][0], axis=0, keepdims=True),
                                                   jnp.max(sts[h][1], axis=0, keepdims=True)))
            acc = jnp.exp2(ms[h] - m_new) * accs[h]
            nxt = [None, None]
            for part in range(2):
                if c + 1 < nsub:
                    nxt[part] = scores(c + 1, h, part)
                p = _bf(jnp.exp2(sts[h][part] - m_new))
                acc = acc + jnp.dot(vt_s[h % 2, c][:, part * half:(part + 1) * half], p,
                                    preferred_element_type=F32)
            accs[h] = acc
            ms[h], sts[h] = m_new, nxt
    o_t = [accs[h][0:HEAD_DIM] / accs[h][HEAD_DIM:HEAD_DIM + 1] for h in range(n_heads)]
    ya = jnp.transpose(jnp.concatenate(o_t, axis=0))
    ya_ref[...] = _bf(ya * ga_ref[...].astype(F32))

    mq = mq_ref[...]
    outs = []
    for c in range(2):
        qc = jnp.concatenate(stack_heads(mq[:, c * LANES:(c + 1) * LANES]), axis=0)
        kc = km_ref[:, c * LANES:(c + 1) * LANES]
        vc = vm_ref[:, c * LANES:(c + 1) * LANES]
        s = lax.dot_general(qc, kc, (((1,), (1,)), ((), ())), preferred_element_type=F32)
        p = jnp.exp2(s - jnp.max(s, axis=-1, keepdims=True))
        om = jnp.dot(_bf(p), vc, preferred_element_type=F32) / jnp.sum(p, axis=-1, keepdims=True)
        outs.append(jnp.where(lo, om[0:tq], om[tq:2 * tq]))
    ym_ref[...] = _bf(jnp.concatenate(outs, axis=1) * gm_ref[...].astype(F32))


def _attention(qkv, mq, memkv, gates, batch, seq, tq, kvc):
    T = qkv.shape[0]
    nq = seq // tq
    mem_len = memkv.shape[0] // batch
    qrow = lambda b, i: (b * nq + i, 0)
    return pl.pallas_call(
        functools.partial(_attn_kernel, kvc=kvc),
        grid=(batch, nq),
        in_specs=[
            pl.BlockSpec((tq, 256), qrow),
            pl.BlockSpec((seq, LANES), lambda b, i: (b, 2)),
            pl.BlockSpec((seq, LANES), lambda b, i: (b, 3)),
            pl.BlockSpec((tq, 256), qrow),
            pl.BlockSpec((mem_len, 256), lambda b, i: (b, 0)),
            pl.BlockSpec((mem_len, 256), lambda b, i: (b, 1)),
            pl.BlockSpec((tq, 256), lambda b, i: (b * nq + i, 0)),
            pl.BlockSpec((tq, 256), lambda b, i: (b * nq + i, 1)),
        ],
        out_specs=[pl.BlockSpec((tq, 256), qrow), pl.BlockSpec((tq, 256), qrow)],
        out_shape=[jax.ShapeDtypeStruct((T, 256), BF16), jax.ShapeDtypeStruct((T, 256), BF16)],
        scratch_shapes=[pltpu.VMEM((KV_HEADS, seq // kvc, ATT_VT_ROWS, kvc), BF16)],
        compiler_params=pltpu.CompilerParams(
            dimension_semantics=("arbitrary", "arbitrary"), vmem_limit_bytes=VMEM_LIMIT),
        name="attention",
    )(qkv, qkv, qkv, mq, memkv, memkv, gates, gates)


GDN_W = GDN_GROUP * HEAD_DIM
GDN_NGRP = N_HEADS // GDN_GROUP
N_COMBO = N_DIRS * N_HEADS
N_QTY = 2


def _gdn_constants():
    w = GDN_W
    i = np.arange(CHUNK)[:, None]
    lane = np.arange(w)[None, :]
    j = lane % CHUNK
    tri = np.stack([i >= j, i <= j]).astype(np.float32)
    stri = np.stack([i > j, i < j]).astype(np.float32)
    eye = (i == j).astype(np.float32)[None]
    tri_c = np.concatenate([tri, stri, eye], axis=0)
    r = np.arange(w)[:, None]
    bd = ((r // CHUNK) == (lane // CHUNK)).astype(np.float32)
    expand = np.zeros((N_DIRS * GDN_NGRP, LANES, N_QTY * w), np.float32)
    for d in range(N_DIRS):
        for g in range(GDN_NGRP):
            for qty in range(N_QTY):
                for hh in range(GDN_GROUP):
                    combo = d * N_HEADS + g * GDN_GROUP + hh
                    for piece in range(3):
                        src = piece * (N_QTY * N_COMBO) + qty * N_COMBO + combo
                        expand[d * GDN_NGRP + g, src, qty * w + hh * CHUNK: qty * w + (hh + 1) * CHUNK] = 1.0
    return tri_c, bd, expand


def _recurrent_kernel(qkv_ref, sm3_ref, tri_ref, bd_ref, exp_ref,
                      gx_ref, gsm_ref, wup_ref, bg_ref, gtri_ref, bdv_ref, stm_ref, kqm_ref,
                      o_ref, oc_ref, st_s, pa_s, pb_s, cd_s, acc_s, gst_s, gacc_s):
    seq = qkv_ref.shape[0]
    nchunk = seq // CHUNK
    w = GDN_W
    gla_init, gla_step, gla_finish = _gla_program(gx_ref, gsm_ref, wup_ref, bg_ref, gtri_ref, bdv_ref, stm_ref,
                                                  kqm_ref, oc_ref, gst_s, gacc_s)
    gla_init()
    acc_s[...] = jnp.zeros_like(acc_s)
    st_s[...] = jnp.zeros_like(st_s)
    bd_mask = bd_ref[...] > 0.5

    combos = [(d, grp) for d in range(N_DIRS) for grp in range(GDN_NGRP)]

    def load_chunk(chunk, d, grp):
        rs = pl.ds(pl.multiple_of(chunk * CHUNK, CHUNK), CHUNK)
        return (qkv_ref[rs, grp * w:(grp + 1) * w], qkv_ref[rs, 256 + grp * w:256 + (grp + 1) * w],
                qkv_ref[rs, 512 + grp * w:512 + (grp + 1) * w], sm3_ref[rs, :])

    nc = len(combos)
    head_masks = [(lax.broadcasted_iota(jnp.int32, (1, w), 1) // HEAD_DIM) == hh for hh in range(GDN_GROUP)]

    items = [(d, grp) for _ in range(GDN_WINDOW) for d, grp in combos]
    ni = len(items)

    def transform_stages(raws, outs):
        exs, diffs, kqs = [], [], []
        a_mats, ps, rs_ = [], [], []

        def stage_products():
            ex_all = [jnp.dot(jnp.concatenate([raws(t * nc + c)[3] for t in range(GDN_WINDOW)], axis=0),
                              exp_ref[c], preferred_element_type=F32) for c in range(nc)]
            for n, (d, grp) in enumerate(items):
                q, k, vb, _ = raws(n)
                ex = ex_all[n % nc][(n // nc) * CHUNK:(n // nc + 1) * CHUNK]
                betax, bx = ex[:, 0:w], ex[:, w:2 * w]
                ebx = jnp.exp2(bx)
                b_end = bx[CHUNK - 1:CHUNK] if d == 0 else bx[0:1]
                exs.append([betax, betax * ebx, ebx, jnp.exp2(b_end - bx)])
                diffs.append(bx - jnp.sum(bx * tri_ref[4], axis=0, keepdims=True))
                kbd = jnp.concatenate([jnp.where(hm, k, jnp.zeros_like(k)) for hm in head_masks], axis=0)
                kqs.append(lax.dot_general(jnp.concatenate([k, q], axis=0), kbd, (((1,), (1,)), ((), ())),
                                           preferred_element_type=F32))

        def stage_square():
            tris = [tri_ref[d] > 0.5 for d in range(N_DIRS)]
            stris = [tri_ref[2 + d] > 0.5 for d in range(N_DIRS)]
            for n, (d, grp) in enumerate(items):
                tri = tris[d]
                stri = stris[d]
                decay = jnp.where(tri, jnp.exp2(jnp.where(tri, diffs[n], 0.0)), 0.0)
                low = jnp.where(stri, exs[n][0] * kqs[n][:CHUNK] * decay, 0.0)
                a_mats.append(kqs[n][CHUNK:] * decay)
                ps.append(_mm(low, _block_diag(low, bd_mask, GDN_GROUP)))
                rs_.append(tri_ref[4] - low)

        def stage_level(last):
            for n in range(ni):
                pbd = _block_diag(ps[n], bd_mask, GDN_GROUP)
                if last:
                    rs_[n] = rs_[n] + _mm(rs_[n], pbd)
                else:
                    rp = _mm(jnp.concatenate([rs_[n], ps[n]], axis=0), pbd)
                    rs_[n] = rs_[n] + rp[:CHUNK]
                    ps[n] = rp[CHUNK:]

        def stage_apply():
            for n, (d, grp) in enumerate(items):
                q, k, vb = raws(n)[:3]
                betax, bebx, ebx, edx = exs[n]
                kf = k.astype(F32)
                u = _mm(rs_[n], _block_diag(betax * vb.astype(F32), bd_mask, GDN_GROUP))
                wm = _mm(rs_[n], _block_diag(bebx * kf, bd_mask, GDN_GROUP))
                cd_row = ebx[CHUNK - 1:CHUNK] if d == 0 else ebx[0:1]
                outs.append((a_mats[n], u, _bf(wm), _bf(q.astype(F32) * ebx), _bf(kf * edx), cd_row))

        return ([stage_products, stage_square] + [functools.partial(stage_level, False)] * 4
                + [functools.partial(stage_level, True), stage_apply])

    def store_transformed(c, t):
        a_mat, u, wm, q_dec, k_dec, cd_row = t
        pa_s[2 * c] = a_mat
        pa_s[2 * c + 1] = u
        pb_s[3 * c] = wm
        pb_s[3 * c + 1] = q_dec
        pb_s[3 * c + 2] = k_dec
        cd_s[c] = jnp.broadcast_to(cd_row, (8, w))

    def load_transformed(c):
        return pa_s[2 * c], pa_s[2 * c + 1], pb_s[3 * c], pb_s[3 * c + 1], pb_s[3 * c + 2], cd_s[c][0:1]

    def scan_stages(states, cur, os_):
        tmp = {}

        def stage_ws(t):
            tmp["cur"] = [cur(t * nc + c) for c in range(nc)]
            tmp["ws"] = [_mm(jnp.concatenate([tmp["cur"][c][2], tmp["cur"][c][3]], axis=0), states[c])
                         for c in range(nc)]

        def stage_update(t):
            for c in range(nc):
                a_mat, u, _, _, k_dec, cd_row = tmp["cur"][c]
                v_new = u - tmp["ws"][c][:CHUNK]
                os_.append(tmp["ws"][c][CHUNK:] + _mm(a_mat, _block_diag(v_new, bd_mask, GDN_GROUP)))
                states[c] = cd_row * states[c] + jnp.where(bd_mask, _mm_tn(k_dec, v_new), 0.0)

        stages = []
        for t in range(GDN_WINDOW):
            stages += [functools.partial(stage_ws, t), functools.partial(stage_update, t)]
        return stages

    def chunk_of(j, t, d):
        i = j * GDN_WINDOW + t
        return i if d == 0 else nchunk - 1 - i

    def window_loader(j):
        def load(n):
            d, grp = items[n]
            return load_chunk(chunk_of(j, n // nc, d), d, grp)
        return load

    first = []
    for stage in transform_stages(window_loader(0), first):
        stage()
    for n in range(ni):
        store_transformed(n, first[n])

    nwin = nchunk // GDN_WINDOW

    def step(j, carry, transform_next=True):
        states = [st_s[c] for c in range(nc)]
        os_, nexts = [], []
        t_stages = transform_stages(window_loader(j + 1), nexts) if transform_next else []
        s_stages = scan_stages(states, load_transformed, os_)
        g_stages, gla_stores = gla_step(j)
        g_at = {2 * i: g for i, g in enumerate(g_stages)}
        for k in range(max(len(t_stages), len(s_stages))):
            if k < len(t_stages):
                t_stages[k]()
            if k < len(s_stages):
                s_stages[k]()
            if k in g_at:
                g_at[k]()
        for t in range(GDN_WINDOW):
            for c, (d, grp) in enumerate(combos):
                rs = pl.ds(pl.multiple_of(chunk_of(j, t, d) * CHUNK, CHUNK), CHUNK)
                acc_s[rs, grp * w:(grp + 1) * w] += os_[t * nc + c]
        for c in range(nc):
            st_s[c] = states[c]
        for n in range(len(nexts)):
            store_transformed(n, nexts[n])
        gla_stores()
        return carry

    lax.fori_loop(0, nwin - 1, step, 0)
    step(jnp.int32(nwin - 1), 0, transform_next=False)
    o_ref[...] = _bf(acc_s[...])
    gla_finish()


def _recurrent(gdn_qkv, gdn_sm3, gla_in, small, wup, bg, gdn_consts, gla_consts, batch, seq):
    T = gdn_qkv.shape[0]
    assert GDN_WINDOW == GLA_WINDOW and seq % (CHUNK * GDN_WINDOW) == 0
    tri_c, bd, expand = gdn_consts
    tri, bd_v, st_mask, kq_mask = gla_consts
    whole = lambda b: (b, 0)
    c2 = lambda b: (0, 0)
    c3 = lambda b: (0, 0, 0)
    out = pl.BlockSpec((seq, 256), whole)
    return pl.pallas_call(
        _recurrent_kernel,
        grid=(batch,),
        in_specs=[
            pl.BlockSpec((seq, 768), whole),
            pl.BlockSpec((seq, LANES), whole),
            pl.BlockSpec(tri_c.shape, c3),
            pl.BlockSpec(bd.shape, c2),
            pl.BlockSpec(expand.shape, c3),
            pl.BlockSpec((seq, 512), whole),
            pl.BlockSpec((seq, LANES), whole),
            pl.BlockSpec(wup.shape, c3),
            pl.BlockSpec(bg.shape, c2),
            pl.BlockSpec(tri.shape, c3),
            pl.BlockSpec(bd_v.shape, c2),
            pl.BlockSpec(st_mask.shape, c2),
            pl.BlockSpec(kq_mask.shape, c2),
        ],
        out_specs=[out, out],
        out_shape=[jax.ShapeDtypeStruct((T, 256), BF16), jax.ShapeDtypeStruct((T, 256), BF16)],
        scratch_shapes=[
            pltpu.VMEM((N_DIRS * GDN_NGRP, GDN_W, GDN_W), F32),
            pltpu.VMEM((2 * GDN_WINDOW * N_DIRS * GDN_NGRP, CHUNK, GDN_W), F32),
            pltpu.VMEM((3 * GDN_WINDOW * N_DIRS * GDN_NGRP, CHUNK, GDN_W), BF16),
            pltpu.VMEM((GDN_WINDOW * N_DIRS * GDN_NGRP, 8, GDN_W), F32),
            pltpu.VMEM((seq, 256), F32),
            pltpu.VMEM((N_DIRS, GLA_QW, GLA_VW), F32),
            pltpu.VMEM((seq, 256), F32),
        ],
        compiler_params=pltpu.CompilerParams(
            dimension_semantics=("arbitrary",), vmem_limit_bytes=VMEM_LIMIT),
        name="recurrent",
    )(gdn_qkv, gdn_sm3, tri_c, bd, expand, gla_in, small, wup, bg, tri, bd_v, st_mask, kq_mask)


GLA_QW = N_HEADS * GLA_DK
GLA_VW = N_HEADS * HEAD_DIM
GLA_WINDOW = 4


def _gla_constants():
    i = np.arange(CHUNK)[:, None]
    lane = np.arange(GLA_VW)[None, :]
    j = lane % CHUNK
    tri = np.stack([i >= j, i <= j]).astype(np.float32)
    r = np.arange(GLA_VW)[:, None]
    bd_v = ((r // CHUNK) == (lane // HEAD_DIM)).astype(np.float32)
    rs = np.arange(GLA_QW)[:, None]
    st_mask = ((rs // GLA_DK) == (lane // HEAD_DIM)).astype(np.float32)
    kq_mask = ((r // CHUNK) == (np.arange(GLA_QW)[None, :] // GLA_DK)).astype(np.float32)
    return tri, bd_v, st_mask, kq_mask


def _gla_program(x_ref, sm_ref, wup_ref, bg_ref, tri_ref, bdv_ref, stm_ref, kqm_ref, o_ref, st_s, acc_s):
    seq = x_ref.shape[0]
    nchunk = seq // CHUNK
    bdv_mask = bdv_ref[...] > 0.5
    st_mask = stm_ref[...] > 0.5
    kq_mask = kqm_ref[...] > 0.5

    def init():
        acc_s[...] = jnp.zeros_like(acc_s)
        st_s[...] = jnp.zeros_like(st_s)

    def rows_of(chunk):
        return pl.ds(pl.multiple_of(chunk * CHUNK, CHUNK), CHUNK)

    def load_chunk(chunk):
        rs = rows_of(chunk)
        return (x_ref[rs, 0:GLA_QW].astype(F32), x_ref[rs, GLA_QW:2 * GLA_QW].astype(F32),
                x_ref[rs, 2 * GLA_QW:2 * GLA_QW + GLA_VW], sm_ref[rs, :])

    items = [(t, d) for t in range(GLA_WINDOW) for d in range(N_DIRS)]
    ni = len(items)

    def chunk_of(jw, t, d):
        i = jw * GLA_WINDOW + t
        return i if d == 0 else nchunk - 1 - i

    def step(jw):
        raws, logits, q_es, attns, dsts, cds, outs = [], [], [], [], [], [], []
        states = []

        def stage_logits():
            for n, (t, d) in enumerate(items):
                raws.append(load_chunk(chunk_of(jw, t, d)))
                logits.append(jnp.dot(_bf(raws[n][3]), wup_ref[d], preferred_element_type=F32)
                              + bg_ref[d:d + 1, :])

        def stage_intra():
            tri_masks = [tri_ref[d] > 0.5 for d in range(N_DIRS)]
            for n, (t, d) in enumerate(items):
                q, k, v, _ = raws[n]
                gk = (jnp.minimum(logits[n], 0.0) - jnp.log1p(jnp.exp(-jnp.abs(logits[n])))) \
                    * (LOG2_E / GLA_NORMALIZER)
                b = _cumsum_rows(gk, reverse=(d == 1))
                b_end = b[CHUNK - 1:CHUNK] if d == 0 else b[0:1]
                q_e = q * (GLA_DK ** -0.5) * jnp.exp2(b)
                keb = _bf(k * jnp.exp2(-b))
                kebd = jnp.where(kq_mask, jnp.concatenate([keb] * N_HEADS, axis=0),
                                 jnp.zeros((GLA_VW, GLA_QW), BF16))
                q_es.append(_bf(q_e))
                attns.append(jnp.where(tri_masks[d], _mm_nt(q_es[n], kebd), 0.0))
                dsts.append(jnp.where(st_mask, _mm_tn(k * jnp.exp2(b_end - b), v), 0.0))
                cd_col = jnp.transpose(jnp.broadcast_to(jnp.exp2(b_end), (GLA_QW, GLA_QW)))
                cds.append(jnp.concatenate([cd_col, cd_col], axis=1))

        def stage_out():
            states.extend(st_s[d] for d in range(N_DIRS))
            for n in range(ni):
                outs.append(_mm(attns[n], _block_diag(raws[n][2], bdv_mask, N_HEADS)))
            for n, (t, d) in enumerate(items):
                outs[n] = outs[n] + _mm(q_es[n], states[d])
                states[d] = cds[n] * states[d] + dsts[n]

        def stores():
            for n, (t, d) in enumerate(items):
                acc_s[rows_of(chunk_of(jw, t, d)), :] += outs[n]
            for d in range(N_DIRS):
                st_s[d] = states[d]

        return [stage_logits, stage_intra, stage_out], stores

    def finish():
        o_ref[...] = _bf(acc_s[...])

    return init, step, finish


def _outproj_kernel(ya_ref, ob_ref, oc_ref, ym_ref, gates_ref, gb_ref, gc_ref, ones_ref, w_ref, h_ref,
                    fg_ref, o_ref, *, final):
    ones256 = ones_ref[...]

    ob = ob_ref[...].astype(F32)
    oc = oc_ref[...].astype(F32)
    ssb, ssc = _group_sums([ob * ob, oc * oc], ones256)
    yb = ob * lax.rsqrt(ssb * (1.0 / HEAD_DIM) + EPS) * gb_ref[...] * gates_ref[:, 0:256].astype(F32)
    yc = oc * lax.rsqrt(ssc * (1.0 / HEAD_DIM) + EPS) * gc_ref[...] * gates_ref[:, 256:512].astype(F32)
    y = jnp.dot(ya_ref[...], w_ref[0:256, :], preferred_element_type=F32)
    y = y + jnp.dot(_bf(yb), w_ref[256:512, :], preferred_element_type=F32)
    y = y + jnp.dot(_bf(yc), w_ref[512:768, :], preferred_element_type=F32)
    y = y + jnp.dot(ym_ref[...], w_ref[768:1024, :], preferred_element_type=F32)
    hn = h_ref[...] + y
    if final:
        hn = hn * lax.rsqrt(jnp.mean(hn * hn, axis=-1, keepdims=True) + EPS) * fg_ref[...]
    o_ref[...] = hn


def _outproj(ya, ob, oc, ym, gates, gb, gc, ones256, w, h, fg, tm, final):
    T, D = h.shape
    row = lambda i: (i, 0)
    const = lambda i: (0, 0)
    return pl.pallas_call(
        functools.partial(_outproj_kernel, final=final),
        grid=(T // tm,),
        in_specs=[
            pl.BlockSpec((tm, 256), row), pl.BlockSpec((tm, 256), row),
            pl.BlockSpec((tm, 256), row), pl.BlockSpec((tm, 256), row),
            pl.BlockSpec((tm, 512), lambda i: (i, 1)),
            pl.BlockSpec((1, 256), const), pl.BlockSpec((1, 256), const),
            pl.BlockSpec((256, 256), const),
            pl.BlockSpec((D, D), const),
            pl.BlockSpec((tm, D), row),
            pl.BlockSpec((1, D), const),
        ],
        out_specs=pl.BlockSpec((tm, D), row),
        out_shape=jax.ShapeDtypeStruct((T, D), F32),
        compiler_params=pltpu.CompilerParams(
            dimension_semantics=("arbitrary",), vmem_limit_bytes=VMEM_LIMIT),
        name="outproj_final" if final else "outproj",
    )(ya, ob, oc, ym, gates, gb, gc, ones256, w, h, fg)


def _block_ones(n, group):
    i = np.arange(n)
    return jnp.asarray((i[:, None] // group) == (i[None, :] // group), dtype=BF16)


def _head_perm_cols():
    return np.concatenate([np.arange(h * HEAD_DIM, (h + 1) * HEAD_DIM) for h in ATT_HEAD_PERM])


def _prep_w_in(w_in):
    o = np.cumsum([0, 256, 128, 128, 256, 768, 8, 8, 256, 128, 128, 256, 32, 256, 256, 256])
    (a_q, a_k, a_v, a_g, d_qkv, d_beta, d_alpha, d_gate, l_q, l_k, l_v, l_low, l_gate, m_q, m_gate) = [
        w_in[:, :, o[i]:o[i + 1]] for i in range(15)]
    perm = _head_perm_cols()
    pad = jnp.zeros(w_in.shape[:2] + (N_COLS - C_SM - 48,), w_in.dtype)
    cols = [a_q[:, :, perm], a_k, a_v, a_g[:, :, perm], d_qkv, d_gate, l_q, l_k, l_v, l_gate, m_q, m_gate,
            d_beta, d_alpha, l_low, pad]
    return jnp.concatenate(cols, axis=-1).astype(BF16)


def _rope_tables(seq):
    rows = seq // GRID_W
    row_pos = jnp.repeat(jnp.arange(rows, dtype=F32), GRID_W)
    col_pos = jnp.tile(jnp.arange(GRID_W, dtype=F32), rows)
    half = HEAD_DIM // 4
    inv_freq = ROPE_THETA ** (-jnp.arange(0, 2 * half, 2, dtype=F32) / (2 * half))
    ang = jnp.stack([row_pos, col_pos], axis=-1)[..., None] * inv_freq
    cos, sin = jnp.cos(ang), jnp.sin(ang)
    cos_h = jnp.concatenate([cos, cos], axis=-1).reshape(seq, HEAD_DIM)
    sin_h = jnp.concatenate([-sin, sin], axis=-1).reshape(seq, HEAD_DIM)
    return jnp.tile(cos_h, (1, 2)), jnp.tile(sin_h, (1, 2))


def _trunk(x, mem, p, consts):
    batch, seq, d_model = x.shape
    T = batch * seq
    depth = p["w_in"].shape[0]
    tm = 1024 if seq % 1024 == 0 else (512 if seq % 512 == 0 else seq)
    tq = 256 if seq % 256 == 0 else seq
    kvc = 512 if seq % 512 == 0 else seq
    cos_t, sin_t = _rope_tables(seq)
    mem2d = mem.reshape(batch * mem.shape[1], d_model)
    memkv = _memkv(mem2d, p["mem_norm_g"], p["w_mem_kv"], min(512, mem2d.shape[0]))
    h = x.reshape(T, d_model)
    for d in range(depth):
        qkv, gates, gdn_qkv, gla_in, mq, small, gdn_sm3 = _inproj(
            h, p["norm_g"][d], p["w_in"][d], cos_t, sin_t, p["qkg"][d], consts["ones384"],
            p["conv_w"][d], p["gdn_par"][d], consts["ones256"], seq, tm)
        ya, ym = _attention(qkv, mq, memkv[d], gates, batch, seq, tq, kvc)
        ob, oc = _recurrent(gdn_qkv, gdn_sm3, gla_in, small, p["wup"][d], p["bg"][d],
                            consts["gdn"], consts["gla"], batch, seq)
        h = _outproj(ya, ob, oc, ym, gates, p["gb"][d], p["gc"][d],
                     consts["ones256"], p["w_out"][d], h, p["final_g"], tm, final=(d == depth - 1))
    return h.reshape(batch, seq, d_model)


def kernel(x_prompt, x_sample, mem_prompt, mem_sample, norm_g, w_in, att_q_norm_g, att_k_norm_g, gdn_conv_w,
           gdn_a_log, gdn_dt_bias, gdn_out_norm_g, gla_w_gate_up, gla_b_gate, gla_out_norm_g, mem_norm_g,
           w_mem_kv, w_out, final_norm_g):
    depth, d_model = norm_g.shape
    perm = _head_perm_cols()
    w_out_p = jnp.concatenate([w_out[:, :256][:, perm], w_out[:, 256:]], axis=1).astype(BF16)
    qkg = jnp.concatenate([jnp.tile(att_q_norm_g, (1, N_HEADS)), jnp.tile(att_k_norm_g, (1, KV_HEADS))], axis=1)
    zeros8 = jnp.zeros((depth, N_COMBO), F32)
    par_a = jnp.concatenate([zeros8, -jnp.exp(gdn_a_log.reshape(depth, N_COMBO)) * LOG2_E,
                             jnp.zeros((depth, LANES - 2 * N_COMBO), F32)], axis=1)
    par_b = jnp.concatenate([zeros8, gdn_dt_bias.reshape(depth, N_COMBO),
                             jnp.zeros((depth, LANES - 2 * N_COMBO), F32)], axis=1)
    gdn_par = jnp.concatenate([par_a[:, None], par_b[:, None], jnp.zeros((depth, 6, LANES), F32)], axis=1)
    conv_w = jnp.concatenate([gdn_conv_w, jnp.zeros((depth, 5, gdn_conv_w.shape[2]), F32)], axis=1)
    wup = jnp.zeros((depth, N_DIRS, LANES, GLA_QW), F32)
    for d in range(N_DIRS):
        lo = 2 * N_COMBO + d * GLA_RANK
        wup = wup.at[:, d, lo:lo + GLA_RANK, :].set(gla_w_gate_up[:, d])
    p = {
        "norm_g": norm_g.reshape(depth, 1, d_model),
        "w_in": _prep_w_in(w_in),
        "qkg": qkg.reshape(depth, 1, 384),
        "conv_w": conv_w,
        "gdn_par": gdn_par,
        "wup": wup.astype(BF16),
        "bg": jnp.concatenate([gla_b_gate, jnp.zeros((depth, 6, GLA_QW), F32)], axis=1),
        "gb": jnp.tile(gdn_out_norm_g, (1, N_HEADS)).reshape(depth, 1, 256),
        "gc": jnp.tile(gla_out_norm_g, (1, N_HEADS)).reshape(depth, 1, 256),
        "mem_norm_g": mem_norm_g.reshape(depth, 1, d_model),
        "w_mem_kv": w_mem_kv.astype(BF16),
        "w_out": w_out_p,
        "final_g": final_norm_g.reshape(1, d_model),
    }
    tri_c, bd, expand = _gdn_constants()
    consts = {
        "ones384": _block_ones(384, HEAD_DIM),
        "ones256": _block_ones(256, HEAD_DIM),
        "gdn": (jnp.asarray(tri_c), jnp.asarray(bd), jnp.asarray(expand, dtype=BF16)),
        "gla": tuple(jnp.asarray(a) for a in _gla_constants()),
    }
    y_prompt = _trunk(x_prompt, mem_prompt, p, consts)
    y_sample = _trunk(x_sample, mem_sample, p, consts)
    return (y_prompt, y_sample)
```
